```python
import math
import jax
import jax.numpy as jnp
from jax import lax
import numpy as np

D_MODEL = 1024
BATCH = 8
SEQ = 2048
DEPTH = 4

GRID_W = 64
CTX_LEN = 256
A_HEADS = 4
A_DQK = 64
A_DV = 128
B_HEADS = 8
B_KV_HEADS = 2
B_DH = 64
C_HEADS = 4
C_DK = 128
C_DV = 128
C_CONV = 3
BRANCH_W = A_HEADS * A_DV
D_FF = 2816
FFN_CONV = 3
CHUNK = 64
Q_BLOCK = 128
ROPE_BASE = 10000.0
EPS = 1e-6
M_INIT = -1e30
IN_SPLIT = (A_HEADS * A_DQK, A_HEADS * A_DQK, A_HEADS * A_DV, A_HEADS * A_DV, 4 * A_HEADS,
            B_HEADS * B_DH, B_KV_HEADS * B_DH, B_KV_HEADS * B_DH,
            C_HEADS * C_DK, C_HEADS * C_DK, C_HEADS * C_DV, C_HEADS * C_DV, 2 * C_HEADS, 2 * C_HEADS,
            3 * D_MODEL)
IN_COLS = sum(IN_SPLIT)

kernel_name = 'hybrid_mlstm_gqa_gdn_dit'


def rmsnorm(x, w):
    xf = x.astype(jnp.float32)
    y = xf * lax.rsqrt(jnp.mean(xf * xf, axis=-1, keepdims=True) + EPS)
    return (y * w.astype(jnp.float32)).astype(x.dtype)


def l2norm(x):
    return x * lax.rsqrt(jnp.sum(x * x, axis=-1, keepdims=True) + EPS)


def modulate(x, shift, scale):
    return x * (1 + scale) + shift


def to_heads(u, n):
    b, t, _ = u.shape
    return u.reshape(b, t, n, -1).transpose(0, 2, 1, 3)


def from_heads(u):
    b, n, t, d = u.shape
    return u.transpose(0, 2, 1, 3).reshape(b, t, n * d)


def flip_t(u):
    return jnp.flip(u, axis=2)


def split_in(u):
    idx = np.cumsum(IN_SPLIT)[:-1].tolist()
    return jnp.split(u, idx, axis=-1)


def dwconv(u, w):
    k = w.shape[0]
    p = k // 2
    t = u.shape[1]
    up = jnp.pad(u, ((0, 0), (p, p), (0, 0)))
    out = up[:, 0:t] * w[0]
    for j in range(1, k):
        out = out + up[:, j:j + t] * w[j]
    return out


def axial_rope_tables(n_tok):
    rows = n_tok // GRID_W
    row = jnp.repeat(jnp.arange(rows, dtype=jnp.float32), GRID_W)
    col = jnp.tile(jnp.arange(GRID_W, dtype=jnp.float32), rows)
    n_freq = B_DH // 4
    inv = ROPE_BASE ** (-jnp.arange(n_freq, dtype=jnp.float32) / n_freq)
    ang = jnp.stack([row[:, None] * inv, col[:, None] * inv], axis=1)
    return jnp.cos(ang), jnp.sin(ang)


def apply_rope(x, cos, sin):
    b, h, t, d = x.shape
    xr = x.astype(jnp.float32).reshape(b, h, t, 2, 2, d // 4)
    x1, x2 = xr[..., 0, :], xr[..., 1, :]
    out = jnp.stack([x1 * cos - x2 * sin, x2 * cos + x1 * sin], axis=-2)
    return out.reshape(b, h, t, d).astype(x.dtype)


def blocked_attention(q, k, v):
    b, hq, t, d = q.shape
    hkv = k.shape[1]
    g = hq // hkv
    nb = t // Q_BLOCK
    qb = q.reshape(b, hkv, g, nb, Q_BLOCK, d).transpose(3, 0, 1, 2, 4, 5).astype(jnp.float32) * (d ** -0.5)
    kf = k.astype(jnp.float32)
    vf = v.astype(jnp.float32)

    def one_block(qi):
        s = jnp.einsum('bkgqd,bktd->bkgqt', qi, kf)
        p = jax.nn.softmax(s, axis=-1)
        return jnp.einsum('bkgqt,bktd->bkgqd', p, vf)

    o = lax.map(one_block, qb)
    return o.transpose(1, 2, 3, 0, 4, 5).reshape(b, hq, t, d).astype(q.dtype)


def mlstm_scan(q, k, v, logi, logf, state):
    b, h, t, dk = q.shape
    dv = v.shape[-1]
    nc = t // CHUNK
    tri = jnp.tril(jnp.ones((CHUNK, CHUNK), bool))

    def chunks(u):
        return jnp.moveaxis(u.reshape(b, h, nc, CHUNK, *u.shape[3:]), 2, 0)

    def step(carry, inp):
        C, n, m = carry
        qc, kc, vc, ic, fc = inp
        bcum = jnp.cumsum(fc, axis=-1)
        dlog = jnp.where(tri, bcum[..., :, None] - bcum[..., None, :] + ic[..., None, :], -jnp.inf)
        inter = bcum + m[..., None]
        mt = jnp.maximum(inter, jnp.max(dlog, axis=-1))
        s = jnp.einsum('bhtd,bhsd->bhts', qc, kc) * jnp.exp(dlog - mt[..., None])
        e_inter = jnp.exp(inter - mt)
        num = jnp.einsum('bhts,bhse->bhte', s, vc) + e_inter[..., None] * jnp.einsum('bhtd,bhde->bhte', qc, C)
        den = jnp.sum(s, axis=-1) + e_inter * jnp.einsum('bhtd,bhd->bht', qc, n)
        hc = num / jnp.maximum(jnp.abs(den), jnp.exp(-mt))[..., None]
        btot = bcum[..., -1]
        glog = btot[..., None] - bcum + ic
        m_new = jnp.maximum(btot + m, jnp.max(glog, axis=-1))
        wk = jnp.exp(glog - m_new[..., None])
        decay = jnp.exp(btot + m - m_new)
        C_new = decay[..., None, None] * C + jnp.einsum('bhs,bhsd,bhse->bhde', wk, kc, vc)
        n_new = decay[..., None] * n + jnp.einsum('bhs,bhsd->bhd', wk, kc)
        return (C_new, n_new, m_new), hc

    carry, hs = lax.scan(step, state, (chunks(q), chunks(k), chunks(v), chunks(logi), chunks(logf)))
    return jnp.moveaxis(hs, 0, 2).reshape(b, h, t, dv), carry


def gdn_scan(q, k, v, g, beta, S0):
    b, h, t, dk = q.shape
    dv = v.shape[-1]
    nc = t // CHUNK

    def ch(u):
        return u.reshape(b, h, nc, CHUNK, *u.shape[3:])

    q, k, v, g, beta = ch(q), ch(k), ch(v), ch(g), ch(beta)
    tri = jnp.tril(jnp.ones((CHUNK, CHUNK), bool))
    stri = jnp.tril(jnp.ones((CHUNK, CHUNK), bool), -1)
    G = jnp.cumsum(g, axis=-1)
    decay = jnp.exp(jnp.where(tri, G[..., :, None] - G[..., None, :], -jnp.inf))
    kb = k * beta[..., None]
    A = jnp.where(stri, jnp.einsum('bhntd,bhnsd->bhnts', kb, k) * decay, 0.0)
    eye = jnp.eye(CHUNK, dtype=A.dtype)
    T = lax.linalg.triangular_solve(A + eye, jnp.broadcast_to(eye, A.shape),
                                    left_side=True, lower=True, unit_diagonal=True)
    u = T @ (v * beta[..., None])
    w = T @ (kb * jnp.exp(G)[..., None])
    qk = jnp.where(tri, jnp.einsum('bhntd,bhnsd->bhnts', q, k) * decay, 0.0)
    qg = q * jnp.exp(G)[..., None]
    g_last = G[..., -1]
    kg = k * jnp.exp(g_last[..., None] - G)[..., None]

    def step(S, inp):
        u_c, w_c, qk_c, qg_c, kg_c, gl_c = inp
        v_new = u_c - w_c @ S
        o = qg_c @ S + qk_c @ v_new
        S = S * jnp.exp(gl_c)[..., None, None] + jnp.einsum('bhsd,bhse->bhde', kg_c, v_new)
        return S, o

    xs = (jnp.moveaxis(u, 2, 0), jnp.moveaxis(w, 2, 0), jnp.moveaxis(qk, 2, 0),
          jnp.moveaxis(qg, 2, 0), jnp.moveaxis(kg, 2, 0), jnp.moveaxis(g_last, 2, 0))
    S, o = lax.scan(step, S0, xs)
    return jnp.moveaxis(o, 0, 2).reshape(b, h, t, dv), S


def mlstm_branch(pc, pl, gate_b, norm_w, need_ctx):
    def prep(p):
        q, k, v, o, g = p
        b, t, _ = q.shape
        q = to_heads(q, A_HEADS).astype(jnp.float32) * (A_DQK ** -0.5)
        k = to_heads(k, A_HEADS).astype(jnp.float32)
        v = to_heads(v, A_HEADS).astype(jnp.float32)
        g = (g.astype(jnp.float32) + gate_b.reshape(-1).astype(jnp.float32)).reshape(b, t, 4, A_HEADS).transpose(2, 0, 3, 1)
        return q, k, v, g, o

    qc, kc, vc, gc, oc = prep(pc)
    ql, kl, vl, gl, ol = prep(pl)
    b = ql.shape[0]
    init = (jnp.zeros((b, A_HEADS, A_DQK, A_DV), jnp.float32),
            jnp.zeros((b, A_HEADS, A_DQK), jnp.float32),
            jnp.full((b, A_HEADS), M_INIT, jnp.float32))
    lsig = jax.nn.log_sigmoid
    hcf, st = mlstm_scan(qc, kc, vc, gc[0], lsig(gc[1]), init)
    hlf, _ = mlstm_scan(ql, kl, vl, gl[0], lsig(gl[1]), st)
    hcb, st = mlstm_scan(flip_t(qc), flip_t(kc), flip_t(vc), flip_t(gc[2]), lsig(flip_t(gc[3])), init)
    hlb, _ = mlstm_scan(flip_t(ql), flip_t(kl), flip_t(vl), flip_t(gl[2]), lsig(flip_t(gl[3])), st)

    def post(hsum, o):
        hn = hsum * lax.rsqrt(jnp.mean(hsum * hsum, axis=-1, keepdims=True) + EPS)
        return (from_heads(hn) * norm_w.astype(jnp.float32) * jax.nn.sigmoid(o.astype(jnp.float32))).astype(o.dtype)

    y_lat = post(hlf + flip_t(hlb), ol)
    y_ctx = post(hcf + flip_t(hcb), oc) if need_ctx else None
    return y_ctx, y_lat


def gqa_branch(pc, pl, qn_w, kn_w, cos, sin, need_ctx):
    def prep(p):
        q, k, v = p
        return (rmsnorm(to_heads(q, B_HEADS), qn_w), rmsnorm(to_heads(k, B_KV_HEADS), kn_w),
                to_heads(v, B_KV_HEADS))

    qc, kc, vc = prep(pc)
    ql, kl, vl = prep(pl)
    ql = apply_rope(ql, cos, sin)
    kl = apply_rope(kl, cos, sin)
    k_all = jnp.concatenate([kc, kl], axis=2)
    v_all = jnp.concatenate([vc, vl], axis=2)
    y_lat = from_heads(blocked_attention(ql, k_all, v_all))
    y_ctx = from_heads(blocked_attention(qc, kc, vc)) if need_ctx else None
    return y_ctx, y_lat


def gdn_branch(pc, pl, conv_w, a_log, dt_bias, norm_w, need_ctx):
    def prep(p):
        q, k, v, z, a, beta = p
        b, t, _ = q.shape
        qkv = jax.nn.silu(dwconv(jnp.concatenate([q, k, v], axis=-1), conv_w))
        q, k, v = jnp.split(qkv, [C_HEADS * C_DK, 2 * C_HEADS * C_DK], axis=-1)
        q = l2norm(to_heads(q, C_HEADS).astype(jnp.float32)) * (C_DK ** -0.5)
        k = l2norm(to_heads(k, C_HEADS).astype(jnp.float32))
        v = to_heads(v, C_HEADS).astype(jnp.float32)
        a = a.astype(jnp.float32).reshape(b, t, 2, C_HEADS) + dt_bias.astype(jnp.float32)
        g = (-jnp.exp(a_log.astype(jnp.float32)) * jax.nn.softplus(a)).transpose(2, 0, 3, 1)
        beta = jax.nn.sigmoid(beta.astype(jnp.float32).reshape(b, t, 2, C_HEADS)).transpose(2, 0, 3, 1)
        return q, k, v, g, beta, z

    qc, kc, vc, gc, bc, zc = prep(pc)
    ql, kl, vl, gl, bl, zl = prep(pl)
    b = ql.shape[0]
    s0 = jnp.zeros((b, C_HEADS, C_DK, C_DV), jnp.float32)
    ocf, st = gdn_scan(qc, kc, vc, gc[0], bc[0], s0)
    olf, _ = gdn_scan(ql, kl, vl, gl[0], bl[0], st)
    ocb, st = gdn_scan(flip_t(qc), flip_t(kc), flip_t(vc), flip_t(gc[1]), flip_t(bc[1]), s0)
    olb, _ = gdn_scan(flip_t(ql), flip_t(kl), flip_t(vl), flip_t(gl[1]), flip_t(bl[1]), st)

    def post(osum, z):
        on = osum * lax.rsqrt(jnp.mean(osum * osum, axis=-1, keepdims=True) + EPS) * norm_w.astype(jnp.float32)
        return (from_heads(on) * jax.nn.silu(z.astype(jnp.float32))).astype(z.dtype)

    y_lat = post(olf + flip_t(olb), zl)
    y_ctx = post(ocf + flip_t(ocb), zc) if need_ctx else None
    return y_ctx, y_lat


def merge_branches(ya, yb, yc, gate_pre, wb, wo):
    ga, gb, gc = jnp.split(jax.nn.sigmoid(gate_pre), 3, axis=-1)
    y = ga * (ya @ wb[0]) + gb * (yb @ wb[1]) + gc * (yc @ wb[2])
    return y @ wo


def conv_ffn(xn, w_up, conv_w, w_down):
    u = dwconv(xn @ w_up, conv_w)
    a, gt = jnp.split(u, 2, axis=-1)
    return (a * jax.nn.silu(gt)) @ w_down


def setup_inputs(seed: int = 0) -> dict:
    key = jax.random.key(seed)
    ks = jax.random.split(key, 24)

    def nrm(k, shape, s):
        return jax.random.normal(k, shape, jnp.float32) * s

    dt = jnp.exp(jax.random.uniform(ks[15], (DEPTH, 2, C_HEADS), jnp.float32,
                                    minval=math.log(1e-3), maxval=math.log(1e-1)))
    return {
        'x': nrm(ks[0], (BATCH, SEQ, D_MODEL), 1.0),
        'c': nrm(ks[1], (BATCH, D_MODEL), 1.0),
        'ctx': nrm(ks[2], (BATCH, CTX_LEN, D_MODEL), 1.0),
        'c_ctx': nrm(ks[3], (D_MODEL,), 1.0),
        'norm1_w': 1.0 + nrm(ks[4], (DEPTH, D_MODEL), 0.05),
        'norm2_w': 1.0 + nrm(ks[5], (DEPTH, D_MODEL), 0.05),
        'ada_w': nrm(ks[6], (DEPTH, D_MODEL, 6 * D_MODEL), 0.5 * D_MODEL ** -0.5),
        'ada_b': nrm(ks[7], (DEPTH, 6 * D_MODEL), 0.02),
        'w_in': nrm(ks[8], (DEPTH, D_MODEL, IN_COLS), D_MODEL ** -0.5),
        'a_gate_b': jnp.array([0.0, 3.0, 0.0, 3.0], jnp.float32)[None, :, None] + nrm(ks[9], (DEPTH, 4, A_HEADS), 0.1),
        'a_norm_w': 1.0 + nrm(ks[10], (DEPTH, A_HEADS * A_DV), 0.05),
        'b_qnorm_w': 1.0 + nrm(ks[11], (DEPTH, B_DH), 0.05),
        'b_knorm_w': 1.0 + nrm(ks[12], (DEPTH, B_DH), 0.05),
        'c_conv_w': nrm(ks[13], (DEPTH, C_CONV, 2 * C_HEADS * C_DK + C_HEADS * C_DV), C_CONV ** -0.5),
        'c_a_log': jnp.log(jax.random.uniform(ks[14], (DEPTH, 2, C_HEADS), jnp.float32, minval=1.0, maxval=16.0)),
        'c_dt_bias': dt + jnp.log(-jnp.expm1(-dt)),
        'c_norm_w': 1.0 + nrm(ks[16], (DEPTH, C_DV), 0.05),
        'w_branch': nrm(ks[17], (DEPTH, 3, BRANCH_W, D_MODEL), BRANCH_W ** -0.5),
        'w_out': nrm(ks[18], (DEPTH, D_MODEL, D_MODEL), D_MODEL ** -0.5),
        'w_up': nrm(ks[19], (DEPTH, D_MODEL, 2 * D_FF), D_MODEL ** -0.5),
        'ffn_conv_w': nrm(ks[20], (DEPTH, FFN_CONV, 2 * D_FF), FFN_CONV ** -0.5),
        'w_down': nrm(ks[21], (DEPTH, D_FF, D_MODEL), D_FF ** -0.5),
    }


def reference(x, c, ctx, c_ctx, norm1_w, norm2_w, ada_w, ada_b, w_in, a_gate_b, a_norm_w,
              b_qnorm_w, b_knorm_w, c_conv_w, c_a_log, c_dt_bias, c_norm_w, w_branch, w_out,
              w_up, ffn_conv_w, w_down):
    cos, sin = axial_rope_tables(x.shape[1])
    silu_c = jax.nn.silu(c)
    silu_cc = jax.nn.silu(c_ctx)
    h_lat, h_ctx = x, ctx
    for l in range(DEPTH):
        need_ctx = l < DEPTH - 1
        m_lat = jnp.split((silu_c @ ada_w[l] + ada_b[l])[:, None, :], 6, axis=-1)
        m_ctx = jnp.split((silu_cc @ ada_w[l] + ada_b[l])[None, None, :], 6, axis=-1)
        xn_lat = modulate(rmsnorm(h_lat, norm1_w[l]), m_lat[0], m_lat[1])
        xn_ctx = modulate(rmsnorm(h_ctx, norm1_w[l]), m_ctx[0], m_ctx[1])
        p_lat = split_in(xn_lat @ w_in[l])
        p_ctx = split_in(xn_ctx @ w_in[l])
        ya_ctx, ya_lat = mlstm_branch(p_ctx[0:5], p_lat[0:5], a_gate_b[l], a_norm_w[l], need_ctx)
        yb_ctx, yb_lat = gqa_branch(p_ctx[5:8], p_lat[5:8], b_qnorm_w[l], b_knorm_w[l], cos, sin, need_ctx)
        yc_ctx, yc_lat = gdn_branch(p_ctx[8:14], p_lat[8:14], c_conv_w[l], c_a_log[l], c_dt_bias[l],
                                    c_norm_w[l], need_ctx)
        h_lat = h_lat + m_lat[2] * merge_branches(ya_lat, yb_lat, yc_lat, p_lat[14], w_branch[l], w_out[l])
        h_lat = h_lat + m_lat[5] * conv_ffn(modulate(rmsnorm(h_lat, norm2_w[l]), m_lat[3], m_lat[4]),
                                            w_up[l], ffn_conv_w[l], w_down[l])
        if need_ctx:
            h_ctx = h_ctx + m_ctx[2] * merge_branches(ya_ctx, yb_ctx, yc_ctx, p_ctx[14], w_branch[l], w_out[l])
            h_ctx = h_ctx + m_ctx[5] * conv_ffn(modulate(rmsnorm(h_ctx, norm2_w[l]), m_ctx[3], m_ctx[4]),
                                                w_up[l], ffn_conv_w[l], w_down[l])
    return h_lat
```

```python
import functools

import jax
import jax.numpy as jnp
from jax import lax
from jax.experimental import pallas as pl
from jax.experimental.pallas import tpu as pltpu

F32 = jnp.float32
BF16 = jnp.bfloat16

D_MODEL = 1024
GRID_W = 64
A_HEADS, A_DQK, A_DV = 4, 64, 128
B_HEADS, B_KV_HEADS, B_DH = 8, 2, 64
C_HEADS, C_DK, C_DV = 4, 128, 128
BRANCH_W = 512
D_FF = 2816
CHUNK = 64
ROPE_BASE = 10000.0
EPS = 1e-6
M_INIT = -1e30
NEG = -1e30

LANES = 128
P_GATE = 0
P_AQK = 3072
P_AV = 3584
P_AO = 4096
P_BQ = 4608
P_BK = 5120
P_BV = 5248
P_CQ = 5376
P_CK = 5888
P_CV = 6400
P_CZ = 6912
P_SMALL = 7424
P_COLS = 7680
PROJ_TN = 512
FFN_TF = 256
Q_TILE = 128
VMEM_LIMIT = 56 * 1024 * 1024


def _dot(a, b):
    return jnp.dot(a.astype(BF16), b.astype(BF16), preferred_element_type=F32)


def _dot_nt(a, b):
    return lax.dot_general(a.astype(BF16), b.astype(BF16), (((1,), (1,)), ((), ())),
                           preferred_element_type=F32)


def _dot_tn(a, b):
    return lax.dot_general(a.astype(BF16), b.astype(BF16), (((0,), (0,)), ((), ())),
                           preferred_element_type=F32)


def _split(a):
    hi = a.astype(BF16)
    lo = (a - hi.astype(F32)).astype(BF16)
    return hi, lo


def _dot3(a, b):
    ah, al = _split(a)
    bh, bl = _split(b)
    d = functools.partial(jnp.dot, preferred_element_type=F32)
    return d(ah, bh) + (d(al, bh) + d(ah, bl))


def _sigmoid(x):
    return 1.0 / (1.0 + jnp.exp(-x))


def _silu(x):
    return x * _sigmoid(x)


def _softplus(x):
    return jnp.maximum(x, 0.0) + jnp.log1p(jnp.exp(-jnp.abs(x)))


def _logsigmoid(x):
    return -_softplus(-x)


def _modnorm(x, w, shift, scale):
    y = x * lax.rsqrt(jnp.mean(x * x, axis=-1, keepdims=True) + EPS)
    return (y * w) * (1.0 + scale) + shift


def _params(sem, vmem=VMEM_LIMIT):
    return pltpu.CompilerParams(dimension_semantics=sem, vmem_limit_bytes=vmem)


def _resident(shape, index_map):
    return pl.BlockSpec(shape, index_map, pipeline_mode=pl.Buffered(1))


def _ada_kernel(c_ref, w_ref, b_ref, o_ref):
    o_ref[...] = _dot3(_silu(c_ref[...]), w_ref[...]) + b_ref[...]


def _ada_call(cc, ada_w, ada_b):
    depth, d, n = ada_w.shape
    tn = 1536
    return pl.pallas_call(
        _ada_kernel,
        grid=(depth, n // tn),
        in_specs=[pl.BlockSpec(cc.shape, lambda l, j: (0, 0)),
                  pl.BlockSpec((None, d, tn), lambda l, j: (l, 0, j)),
                  pl.BlockSpec((None, 1, tn), lambda l, j: (l, 0, j))],
        out_specs=pl.BlockSpec((None, cc.shape[0], tn), lambda l, j: (l, 0, j)),
        out_shape=jax.ShapeDtypeStruct((depth, cc.shape[0], n), F32),
        compiler_params=_params(("arbitrary", "arbitrary")),
        name="ada",
    )(cc, ada_w, ada_b.reshape(depth, 1, n))


def _proj_kernel(h_ref, mod_ref, nw_ref, w_ref, p_ref):
    mod = mod_ref[...]
    xn = _modnorm(h_ref[...], nw_ref[...], mod[0:1], mod[1:2]).astype(BF16)
    for j in range(P_COLS // PROJ_TN):
        sl = slice(j * PROJ_TN, (j + 1) * PROJ_TN)
        p_ref[:, sl] = jnp.dot(xn, w_ref[:, sl], preferred_element_type=F32)


def _proj_call(h, mod, nw, w, tm, nct):
    b, t, d = h.shape
    return pl.pallas_call(
        _proj_kernel,
        grid=(b, t // tm),
        in_specs=[pl.BlockSpec((None, tm, d), lambda i, j: (i, j, 0)),
                  pl.BlockSpec((None, None, 6, d), lambda i, j: (i, jnp.minimum(j // nct, 1), 0, 0)),
                  pl.BlockSpec((1, d), lambda i, j: (0, 0)),
                  _resident((d, P_COLS), lambda i, j: (0, 0))],
        out_specs=pl.BlockSpec((None, tm, P_COLS), lambda i, j: (i, j, 0)),
        out_shape=jax.ShapeDtypeStruct((b, t, P_COLS), F32),
        compiler_params=_params(("arbitrary", "arbitrary")),
        name="proj_in",
    )(h, mod, nw, w)


def _chunk_masks():
    row = lax.broadcasted_iota(jnp.int32, (CHUNK, CHUNK), 0)
    col = lax.broadcasted_iota(jnp.int32, (CHUNK, CHUNK), 1)
    return col <= row, col >= row, col < row, col > row


def _bwd_chunk(it, nc, ncc):
    return jnp.where(it < ncc, ncc - 1 - it, nc - 1 - (it - ncc))


def _mlstm_kernel(qk_ref, v_ref, o_ref, gcol_ref, grow_ref, brow_ref, bcol_ref, nw_ref, y_ref,
                  hf_ref, hb_ref, *, nc, ncc):
    L = CHUNK
    le, ge, _, _ = _chunk_masks()
    lane = lax.broadcasted_iota(jnp.int32, (L, LANES), 1)
    ones_blk = jnp.where(lane == 0, 1.0, 0.0).astype(F32)
    bias_row = brow_ref[...]
    bias_col = bcol_ref[...]

    def stream(c, d, C, m):
        mask_in, mask_t = (le, ge) if d == 0 else (ge, le)
        r0 = pl.multiple_of(c * L, L)
        x = qk_ref[pl.ds(r0, L), :]
        q = x[:, :A_DQK] * (A_DQK ** -0.5)
        k = x[:, A_DQK:]
        v = v_ref[pl.ds(r0, L), :]
        gc = gcol_ref[pl.ds(r0, L), :] + bias_row
        gr = grow_ref[c] + bias_col
        i_col = gc[:, 2 * d:2 * d + 1]
        f_col = _logsigmoid(gc[:, 2 * d + 1:2 * d + 2])
        i_row = gr[2 * d:2 * d + 1, :]
        f_row = _logsigmoid(gr[2 * d + 1:2 * d + 2, :])
        bcum_col = jnp.sum(jnp.where(mask_in, f_row, 0.0), axis=1, keepdims=True)
        bcum_row = jnp.sum(jnp.where(mask_t, f_col, 0.0), axis=0, keepdims=True)
        a_row = i_row - bcum_row
        dlog = jnp.where(mask_in, bcum_col + a_row, NEG)
        dmax = jnp.max(dlog, axis=1, keepdims=True)
        inter = bcum_col + m
        mt = jnp.maximum(inter, dmax)
        s = _dot_nt(q, k) * jnp.exp(dlog - mt)
        e_inter = jnp.exp(inter - mt)
        v_ext = jnp.concatenate([v, ones_blk], axis=1)
        num_ext = _dot(s, v_ext) + e_inter * _dot(q, C)
        num = num_ext[:, :A_DV]
        den = num_ext[:, A_DV:A_DV + 1]
        hc = num / jnp.maximum(jnp.abs(den), jnp.exp(-mt))
        btot = jnp.sum(f_row, axis=1, keepdims=True)
        glog = btot - bcum_col + i_col
        m_new = jnp.maximum(btot + m, jnp.max(glog, axis=0, keepdims=True))
        wk = jnp.exp(glog - m_new)
        decay = jnp.exp(btot + m - m_new)
        C_new = decay * C + _dot_tn(k * wk, v_ext)
        return hc, C_new, m_new

    def body(it, carry):
        Cf, mf, Cb, mb = carry
        cb = _bwd_chunk(it, nc, ncc)
        hf, Cf, mf = stream(it, 0, Cf, mf)
        hf_ref[pl.ds(pl.multiple_of(it * L, L), L), :] = hf
        hb, Cb, mb = stream(cb, 1, Cb, mb)
        hb_ref[pl.ds(pl.multiple_of(cb * L, L), L), :] = hb
        return Cf, mf, Cb, mb

    c0 = jnp.zeros((A_DQK, 2 * LANES), F32)
    m0 = jnp.full((1, 1), M_INIT, F32)
    lax.fori_loop(0, nc, body, (c0, m0, c0, m0))

    hsum = hf_ref[...] + hb_ref[...]
    hn = hsum * lax.rsqrt(jnp.mean(hsum * hsum, axis=-1, keepdims=True) + EPS)
    y_ref[...] = hn * nw_ref[...] * _sigmoid(o_ref[...])


def _mlstm_call(p, gcol, grow, brow, bcol, nw, ncc):
    b, t, _ = p.shape
    nc = t // CHUNK
    blk = lambda base: pl.BlockSpec((None, t, LANES), lambda i, h: (i, 0, base // LANES + h))
    return pl.pallas_call(
        functools.partial(_mlstm_kernel, nc=nc, ncc=ncc),
        grid=(b, A_HEADS),
        in_specs=[blk(P_AQK), blk(P_AV), blk(P_AO),
                  pl.BlockSpec((None, None, t, 4), lambda i, h: (i, h, 0, 0)),
                  pl.BlockSpec((None, None, nc, 4, CHUNK), lambda i, h: (i, h, 0, 0, 0)),
                  pl.BlockSpec((None, 1, 4), lambda i, h: (h, 0, 0)),
                  pl.BlockSpec((None, 4, 1), lambda i, h: (h, 0, 0)),
                  pl.BlockSpec((1, LANES), lambda i, h: (0, h))],
        out_specs=pl.BlockSpec((None, t, LANES), lambda i, h: (i, 0, h)),
        out_shape=jax.ShapeDtypeStruct((b, t, BRANCH_W), F32),
        scratch_shapes=[pltpu.VMEM((t, LANES), F32), pltpu.VMEM((t, LANES), F32)],
        compiler_params=_params(("arbitrary", "arbitrary")),
        name="mlstm",
    )(p, p, p, gcol, grow, brow, bcol, nw)


def _headnorm_rope(x, bd, w, cos, sin_signed, lane_half):
    hi, lo = _split(x * x)
    ssum = jnp.dot(hi, bd, preferred_element_type=F32) + jnp.dot(lo, bd, preferred_element_type=F32)
    y = x * lax.rsqrt(ssum * (1.0 / B_DH) + EPS) * w
    swapped = jnp.where(lane_half, pltpu.roll(y, LANES - 16, axis=1), pltpu.roll(y, 16, axis=1))
    return y * cos + swapped * sin_signed


def _gqa_kernel(q_ref, k_ref, v_ref, qw_ref, kw_ref, cosq_ref, sinq_ref, cosk_ref, sink_ref, y_ref,
                kn_ref, vb_ref, *, tc, nqc):
    tq = q_ref.shape[0]
    t = k_ref.shape[0]
    qb = pl.program_id(1)
    r = lax.broadcasted_iota(jnp.int32, (LANES, LANES), 0)
    c = lax.broadcasted_iota(jnp.int32, (LANES, LANES), 1)
    bd = jnp.where(r // B_DH == c // B_DH, 1.0, 0.0).astype(BF16)

    def lane_half(n):
        return (lax.broadcasted_iota(jnp.int32, (n, LANES), 1) % 32) < 16

    @pl.when(qb == 0)
    def _():
        kn = _headnorm_rope(k_ref[...], bd, kw_ref[...], cosk_ref[...], sink_ref[...], lane_half(t))
        kn_ref[...] = kn.astype(BF16)
        vb_ref[...] = v_ref[...].astype(BF16)

    qn = []
    for j in range(B_HEADS * B_DH // LANES):
        sl = slice(j * LANES, (j + 1) * LANES)
        y = _headnorm_rope(q_ref[:, sl], bd, qw_ref[...], cosq_ref[...], sinq_ref[...], lane_half(tq))
        qn.append((y * (B_DH ** -0.5)).astype(BF16))

    group = B_HEADS // B_KV_HEADS

    def attend(nk):
        for kvh in range(B_KV_HEADS):
            kh = kn_ref[0:nk, kvh * B_DH:(kvh + 1) * B_DH]
            vh = vb_ref[0:nk, kvh * B_DH:(kvh + 1) * B_DH]
            heads = [kvh * group + g for g in range(group)]
            qs = jnp.concatenate(
                [qn[h // 2][:, (h % 2) * B_DH:(h % 2 + 1) * B_DH] for h in heads], axis=0)
            s = lax.dot_general(qs, kh, (((1,), (1,)), ((), ())), preferred_element_type=F32)
            p = jnp.exp(s - jnp.max(s, axis=-1, keepdims=True))
            l = jnp.sum(p, axis=-1, keepdims=True)
            o = jnp.dot(p.astype(BF16), vh, preferred_element_type=F32) / l
            for g, h in enumerate(heads):
                y_ref[:, h * B_DH:(h + 1) * B_DH] = o[g * tq:(g + 1) * tq]

    @pl.when(qb < nqc)
    def _():
        attend(tc)

    @pl.when(qb >= nqc)
    def _():
        attend(t)


def _gqa_call(p, qw, kw, cos2, sin2, tc):
    b, t, _ = p.shape
    tq = Q_TILE
    row_blk = lambda: pl.BlockSpec((tq, LANES), lambda i, j: (j, 0))
    full = lambda: pl.BlockSpec((t, LANES), lambda i, j: (0, 0))
    return pl.pallas_call(
        functools.partial(_gqa_kernel, tc=tc, nqc=tc // tq),
        grid=(b, t // tq),
        in_specs=[pl.BlockSpec((None, tq, B_HEADS * B_DH), lambda i, j: (i, j, P_BQ // (B_HEADS * B_DH))),
                  pl.BlockSpec((None, t, LANES), lambda i, j: (i, 0, P_BK // LANES)),
                  pl.BlockSpec((None, t, LANES), lambda i, j: (i, 0, P_BV // LANES)),
                  pl.BlockSpec((1, LANES), lambda i, j: (0, 0)),
                  pl.BlockSpec((1, LANES), lambda i, j: (0, 0)),
                  row_blk(), row_blk(), full(), full()],
        out_specs=pl.BlockSpec((None, tq, BRANCH_W), lambda i, j: (i, j, 0)),
        out_shape=jax.ShapeDtypeStruct((b, t, BRANCH_W), F32),
        scratch_shapes=[pltpu.VMEM((t, LANES), BF16), pltpu.VMEM((t, LANES), BF16)],
        compiler_params=_params(("arbitrary", "arbitrary")),
        name="gqa",
    )(p, p, p, qw, kw, cos2, sin2, cos2, sin2)


def _unit_tri_inverse(a):
    row = lax.broadcasted_iota(jnp.int32, a.shape, 0)
    col = lax.broadcasted_iota(jnp.int32, a.shape, 1)
    x = -a
    inv = jnp.where(row == col, 1.0, 0.0).astype(F32) + x
    pw = x
    n = 1
    while 2 * n < a.shape[0]:
        pw = _dot(pw, pw)
        inv = inv + _dot(inv, pw)
        n *= 2
    return inv


def _gdn_kernel(q_ref, k_ref, v_ref, z_ref, cwq_ref, cwk_ref, cwv_ref, gcol_ref, grow_ref, par_ref,
                nw_ref, y_ref, qn_ref, kn_ref, vn_ref, u_ref, w_ref, qg_ref, kg_ref, qkm_ref,
                egl_ref, o_ref, *, nc, ncc):
    L = CHUNK
    t = q_ref.shape[0]
    tc = ncc * L
    le, ge, lt, gt = _chunk_masks()
    par = par_ref[...]

    rows = lax.broadcasted_iota(jnp.int32, (t, 1), 0)
    has_prev = jnp.logical_and(rows != 0, rows != tc)
    has_next = jnp.logical_and(rows != tc - 1, rows != t - 1)

    def conv_silu(x_ref, cw_ref):
        x = x_ref[...]
        cw = cw_ref[...]
        prev = jnp.where(has_prev, pltpu.roll(x, 1, axis=0), 0.0)
        nxt = jnp.where(has_next, pltpu.roll(x, t - 1, axis=0), 0.0)
        return _silu(prev * cw[0:1] + x * cw[1:2] + nxt * cw[2:3])

    def l2n(x):
        return x * lax.rsqrt(jnp.sum(x * x, axis=-1, keepdims=True) + EPS)

    qn_ref[...] = l2n(conv_silu(q_ref, cwq_ref)) * (C_DK ** -0.5)
    kn_ref[...] = l2n(conv_silu(k_ref, cwk_ref))
    vn_ref[...] = conv_silu(v_ref, cwv_ref)

    def prep(c, carry):
        r0 = pl.multiple_of(c * L, L)
        q = qn_ref[pl.ds(r0, L), :]
        k = kn_ref[pl.ds(r0, L), :]
        v = vn_ref[pl.ds(r0, L), :]
        gc = gcol_ref[pl.ds(r0, L), :]
        gr = grow_ref[c]
        kk_qk = _dot_nt(jnp.concatenate([k, q], axis=0), k)
        kk = kk_qk[:L]
        qk = kk_qk[L:]
        for d in range(2):
            mask_in, mask_t, strict = (le, ge, lt) if d == 0 else (ge, le, gt)
            neg_rate = -jnp.exp(par[:, d:d + 1])
            g_col = neg_rate * _softplus(gc[:, d:d + 1] + par[:, 2 + d:3 + d])
            g_row = neg_rate * _softplus(gr[d:d + 1, :] + par[:, 2 + d:3 + d])
            beta = _sigmoid(gc[:, 2 + d:3 + d])
            G_col = jnp.sum(jnp.where(mask_in, g_row, 0.0), axis=1, keepdims=True)
            G_row = jnp.sum(jnp.where(mask_t, g_col, 0.0), axis=0, keepdims=True)
            dec = jnp.exp(jnp.where(mask_in, G_col - G_row, NEG))
            tinv = _unit_tri_inverse(jnp.where(strict, beta * kk * dec, 0.0))
            eG = jnp.exp(G_col)
            kb = k * beta
            uw = _dot(tinv, jnp.concatenate([v * beta, kb * eG], axis=1))
            g_last = jnp.sum(g_row, axis=1, keepdims=True)
            u_ref[d, pl.ds(r0, L), :] = uw[:, :C_DV]
            w_ref[d, pl.ds(r0, L), :] = uw[:, C_DV:].astype(BF16)
            qg_ref[d, pl.ds(r0, L), :] = (q * eG).astype(BF16)
            kg_ref[d, pl.ds(r0, L), :] = (k * jnp.exp(g_last - G_col)).astype(BF16)
            qkm_ref[d, pl.ds(r0, L), :] = (qk * dec).astype(BF16)
            egl_ref[d, c] = jnp.broadcast_to(jnp.exp(g_last), (8, LANES))
        return carry

    lax.fori_loop(0, nc, prep, 0)

    def step(c, d, S):
        r0 = pl.multiple_of(c * L, L)
        wq = jnp.concatenate([w_ref[d, pl.ds(r0, L), :], qg_ref[d, pl.ds(r0, L), :]], axis=0)
        ws_qs = jnp.dot(wq, S.astype(BF16), preferred_element_type=F32)
        v_new = (u_ref[d, pl.ds(r0, L), :] - ws_qs[:L]).astype(BF16)
        o = ws_qs[L:] + jnp.dot(qkm_ref[d, pl.ds(r0, L), :], v_new, preferred_element_type=F32)
        o_ref[d, pl.ds(r0, L), :] = o
        egl = egl_ref[d, c][0:1, 0:1]
        return S * egl + lax.dot_general(kg_ref[d, pl.ds(r0, L), :], v_new, (((0,), (0,)), ((), ())),
                                         preferred_element_type=F32)

    def body(it, carry):
        Sf, Sb = carry
        return step(it, 0, Sf), step(_bwd_chunk(it, nc, ncc), 1, Sb)

    s0 = jnp.zeros((C_DK, C_DV), F32)
    lax.fori_loop(0, nc, body, (s0, s0))

    osum = o_ref[0] + o_ref[1]
    on = osum * lax.rsqrt(jnp.mean(osum * osum, axis=-1, keepdims=True) + EPS) * nw_ref[...]
    y_ref[...] = on * _silu(z_ref[...])


def _gdn_call(p, cw, gcol, grow, par, nw, ncc):
    b, t, _ = p.shape
    nc = t // CHUNK
    blk = lambda base: pl.BlockSpec((None, t, LANES), lambda i, h: (i, 0, base // LANES + h))
    cwb = lambda off: pl.BlockSpec((3, LANES), lambda i, h: (0, off + h))
    big = lambda dt: pltpu.VMEM((2, t, LANES), dt)
    return pl.pallas_call(
        functools.partial(_gdn_kernel, nc=nc, ncc=ncc),
        grid=(b, C_HEADS),
        in_specs=[blk(P_CQ), blk(P_CK), blk(P_CV), blk(P_CZ), cwb(0), cwb(C_HEADS), cwb(2 * C_HEADS),
                  pl.BlockSpec((None, None, t, 4), lambda i, h: (i, h, 0, 0)),
                  pl.BlockSpec((None, None, nc, 4, CHUNK), lambda i, h: (i, h, 0, 0, 0)),
                  pl.BlockSpec((None, 1, 4), lambda i, h: (h, 0, 0)),
                  pl.BlockSpec((1, LANES), lambda i, h: (0, 0))],
        out_specs=pl.BlockSpec((None, t, LANES), lambda i, h: (i, 0, h)),
        out_shape=jax.ShapeDtypeStruct((b, t, BRANCH_W), F32),
        scratch_shapes=[pltpu.VMEM((t, LANES), F32), pltpu.VMEM((t, LANES), F32),
                        pltpu.VMEM((t, LANES), F32),
                        big(F32), big(BF16), big(BF16), big(BF16),
                        pltpu.VMEM((2, t, CHUNK), BF16),
                        pltpu.VMEM((2, nc, 8, LANES), F32),
                        big(F32)],
        compiler_params=_params(("arbitrary", "arbitrary")),
        name="gdn",
    )(p, p, p, p, cw, cw, cw, gcol, grow, par, nw)


def _merge_kernel(ya_ref, yb_ref, yc_ref, ga_ref, gb_ref, gc_ref, h_ref, mod_ref, wb_ref, wo_ref, o_ref):
    y = (_sigmoid(ga_ref[...]) * _dot(ya_ref[...], wb_ref[0])
         + _sigmoid(gb_ref[...]) * _dot(yb_ref[...], wb_ref[1])
         + _sigmoid(gc_ref[...]) * _dot(yc_ref[...], wb_ref[2]))
    o_ref[...] = h_ref[...] + mod_ref[2:3, :] * _dot(y, wo_ref[...])


def _merge_call(ya, yb, yc, p, h, mod, wb, wo, tm, nct):
    b, t, d = h.shape
    yblk = lambda: pl.BlockSpec((None, tm, BRANCH_W), lambda i, j: (i, j, 0))
    gblk = lambda g: pl.BlockSpec((None, tm, d), lambda i, j: (i, j, g))
    return pl.pallas_call(
        _merge_kernel,
        grid=(b, t // tm),
        in_specs=[yblk(), yblk(), yblk(), gblk(0), gblk(1), gblk(2),
                  pl.BlockSpec((None, tm, d), lambda i, j: (i, j, 0)),
                  pl.BlockSpec((None, None, 6, d), lambda i, j: (i, jnp.minimum(j // nct, 1), 0, 0)),
                  _resident((3, BRANCH_W, d), lambda i, j: (0, 0, 0)),
                  _resident((d, d), lambda i, j: (0, 0))],
        out_specs=pl.BlockSpec((None, tm, d), lambda i, j: (i, j, 0)),
        out_shape=jax.ShapeDtypeStruct((b, t, d), F32),
        compiler_params=_params(("arbitrary", "arbitrary")),
        name="merge",
    )(ya, yb, yc, p, p, p, h, mod, wb, wo)


def _ffn_kernel(h_ref, hp_ref, hn_ref, mod_ref, nw_ref, wup_ref, cw_ref, wdn_ref, o_ref, acc_ref,
                *, nct, ntiles):
    tm = h_ref.shape[0]
    j = pl.program_id(1)
    mod = mod_ref[...]
    nw = nw_ref[...]
    h = h_ref[...]
    xn = _modnorm(h, nw, mod[3:4], mod[4:5]).astype(BF16)
    halo = jnp.concatenate([hp_ref[...], hn_ref[...]], axis=0)
    xh = _modnorm(halo, nw, mod[3:4], mod[4:5]).astype(BF16)
    has_prev = jnp.logical_and(j != 0, j != nct).astype(F32)
    has_next = jnp.logical_and(j != nct - 1, j != ntiles - 1).astype(F32)
    rows = lax.broadcasted_iota(jnp.int32, (tm, 1), 0)
    first = rows == 0
    last = rows == tm - 1

    def up_conv(kind, jf):
        u = jnp.dot(xn, wup_ref[kind, jf], preferred_element_type=F32)
        uh = jnp.dot(xh, wup_ref[kind, jf], preferred_element_type=F32)
        cw = cw_ref[kind, jf]
        prev = jnp.where(first, uh[7:8] * has_prev, pltpu.roll(u, 1, axis=0))
        nxt = jnp.where(last, uh[8:9] * has_next, pltpu.roll(u, tm - 1, axis=0))
        return prev * cw[0:1] + u * cw[1:2] + nxt * cw[2:3]

    for jf in range(wup_ref.shape[1]):
        act = (up_conv(0, jf) * _silu(up_conv(1, jf))).astype(BF16)
        part = jnp.dot(act, wdn_ref[jf], preferred_element_type=F32)
        if jf == 0:
            acc_ref[...] = part
        else:
            acc_ref[...] += part
    o_ref[...] = h + mod[5:6] * acc_ref[...]


def _ffn_call(h, mod, nw, wup, cw, wdn, tm, nct):
    b, t, d = h.shape
    ntiles = t // tm
    hb = tm // 8
    return pl.pallas_call(
        functools.partial(_ffn_kernel, nct=nct, ntiles=ntiles),
        grid=(b, ntiles),
        in_specs=[pl.BlockSpec((None, tm, d), lambda i, j: (i, j, 0)),
                  pl.BlockSpec((None, 8, d), lambda i, j: (i, jnp.maximum(j * hb - 1, 0), 0)),
                  pl.BlockSpec((None, 8, d), lambda i, j: (i, jnp.minimum((j + 1) * hb, t // 8 - 1), 0)),
                  pl.BlockSpec((None, None, 6, d), lambda i, j: (i, jnp.minimum(j // nct, 1), 0, 0)),
                  pl.BlockSpec((1, d), lambda i, j: (0, 0)),
                  _resident(wup.shape, lambda i, j: (0, 0, 0, 0)),
                  _resident(cw.shape, lambda i, j: (0, 0, 0, 0)),
                  _resident(wdn.shape, lambda i, j: (0, 0, 0))],
        out_specs=pl.BlockSpec((None, tm, d), lambda i, j: (i, j, 0)),
        out_shape=jax.ShapeDtypeStruct((b, t, d), F32),
        scratch_shapes=[pltpu.VMEM((tm, d), F32)],
        compiler_params=_params(("arbitrary", "arbitrary")),
        name="ffn",
    )(h, h, h, mod, nw, wup, cw, wdn)


def _rope_tables(tc, tl):
    rows = tl // GRID_W
    row = jnp.repeat(jnp.arange(rows, dtype=F32), GRID_W)
    col = jnp.tile(jnp.arange(GRID_W, dtype=F32), rows)
    n_freq = B_DH // 4
    inv = ROPE_BASE ** (-jnp.arange(n_freq, dtype=F32) / n_freq)
    ang_r = row[:, None] * inv
    ang_c = col[:, None] * inv
    cos = jnp.concatenate([jnp.cos(ang_r)] * 2 + [jnp.cos(ang_c)] * 2, axis=1)
    sin = jnp.concatenate([-jnp.sin(ang_r), jnp.sin(ang_r), -jnp.sin(ang_c), jnp.sin(ang_c)], axis=1)
    cos = jnp.concatenate([jnp.ones((tc, B_DH), F32), cos], axis=0)
    sin = jnp.concatenate([jnp.zeros((tc, B_DH), F32), sin], axis=0)
    return jnp.tile(cos, (1, 2)), jnp.tile(sin, (1, 2))


def _pack_w_in(w_in):
    depth, d, _ = w_in.shape
    o = 0
    parts = {}
    for name, width in (("aq", 256), ("ak", 256), ("av", 512), ("ao", 512), ("ag", 16),
                        ("bq", 512), ("bk", 128), ("bv", 128),
                        ("cq", 512), ("ck", 512), ("cv", 512), ("cz", 512), ("ca", 8), ("cb", 8),
                        ("gate", 3 * D_MODEL)):
        parts[name] = w_in[:, :, o:o + width]
        o += width
    aqk = jnp.concatenate([parts["aq"].reshape(depth, d, A_HEADS, A_DQK),
                           parts["ak"].reshape(depth, d, A_HEADS, A_DQK)], axis=3).reshape(depth, d, 512)
    small = jnp.concatenate([parts["ag"], parts["ca"], parts["cb"]], axis=2)
    pad = jnp.zeros((depth, d, P_COLS - P_SMALL - small.shape[2]), w_in.dtype)
    cols = [parts["gate"], aqk, parts["av"], parts["ao"], parts["bq"], parts["bk"], parts["bv"],
            parts["cq"], parts["ck"], parts["cv"], parts["cz"], small, pad]
    return jnp.concatenate(cols, axis=2).astype(BF16)


def _gate_layouts(x, heads, kinds):
    b, t, _ = x.shape
    g = x.reshape(b, t, kinds, heads)
    col = g.transpose(0, 3, 1, 2)
    row = g.reshape(b, t // CHUNK, CHUNK, kinds, heads).transpose(0, 4, 1, 3, 2)
    return col, row


def kernel(x, c, ctx, c_ctx, norm1_w, norm2_w, ada_w, ada_b, w_in, a_gate_b, a_norm_w, b_qnorm_w,
           b_knorm_w, c_conv_w, c_a_log, c_dt_bias, c_norm_w, w_branch, w_out, w_up, ffn_conv_w, w_down):
    b, tl, d = x.shape
    tc = ctx.shape[1]
    depth = w_in.shape[0]
    t = tc + tl
    tm = 256 if (tc % 256 == 0 and tl % 256 == 0) else 128
    nct = tc // tm
    ncc = tc // CHUNK
    nj = D_FF // FFN_TF

    cc = jnp.zeros((16, d), F32).at[:b].set(c).at[b].set(c_ctx)
    mods = _ada_call(cc, ada_w, ada_b).reshape(depth, 16, 6, d)
    mod_all = jnp.stack([jnp.broadcast_to(mods[:, b][:, None], (depth, b, 6, d)), mods[:, :b]], axis=2)

    w_in_p = _pack_w_in(w_in)
    wb = w_branch.astype(BF16)
    wo = w_out.astype(BF16)
    wup = w_up.astype(BF16).reshape(depth, d, 2, nj, FFN_TF).transpose(0, 2, 3, 1, 4)
    fcw = ffn_conv_w.reshape(depth, 3, 2, nj, FFN_TF).transpose(0, 2, 3, 1, 4)
    wdn = w_down.astype(BF16).reshape(depth, nj, FFN_TF, d)
    cos2, sin2 = _rope_tables(tc, tl)
    a_brow = a_gate_b.transpose(0, 2, 1)[:, :, None, :]
    a_bcol = a_gate_b.transpose(0, 2, 1)[:, :, :, None]
    c_par = jnp.concatenate([c_a_log, c_dt_bias], axis=1).transpose(0, 2, 1)[:, :, None, :]

    h = jnp.concatenate([ctx, x], axis=1)
    for l in range(depth):
        mod = mod_all[l]
        p = _proj_call(h, mod, norm1_w[l][None], w_in_p[l], tm, nct)
        small = p[:, :, P_SMALL:P_SMALL + 32]
        a_gcol, a_grow = _gate_layouts(small[:, :, :16], A_HEADS, 4)
        c_gcol, c_grow = _gate_layouts(small[:, :, 16:32], C_HEADS, 4)
        ya = _mlstm_call(p, a_gcol, a_grow, a_brow[l], a_bcol[l], a_norm_w[l][None], ncc)
        yb = _gqa_call(p, jnp.tile(b_qnorm_w[l], 2)[None], jnp.tile(b_knorm_w[l], 2)[None], cos2, sin2, tc)
        yc = _gdn_call(p, c_conv_w[l], c_gcol, c_grow, c_par[l], c_norm_w[l][None], ncc)
        h = _merge_call(ya, yb, yc, p, h, mod, wb[l], wo[l], tm, nct)
        h = _ffn_call(h, mod, norm2_w[l][None], wup[l], fcw[l], wdn[l], tm, nct)
    return h[:, tc:, :]
```

```python
import functools

import jax
import jax.numpy as jnp
from jax import lax
from jax.experimental import pallas as pl
from jax.experimental.pallas import tpu as pltpu

F32 = jnp.float32
BF16 = jnp.bfloat16

D_MODEL = 1024
GRID_W = 64
A_HEADS, A_DQK, A_DV = 4, 64, 128
B_HEADS, B_KV_HEADS, B_DH = 8, 2, 64
C_HEADS, C_DK, C_DV = 4, 128, 128
BRANCH_W = 512
D_FF = 2816
CHUNK = 64
ROPE_BASE = 10000.0
EPS = 1e-6
M_INIT = -1e30
NEG = -1e30

LANES = 128
P_GATE = 0
P_AQK = 3072
P_AV = 3584
P_AO = 4096
P_BQ = 4608
P_BK = 5120
P_BV = 5248
P_CQ = 5376
P_CK = 5888
P_CV = 6400
P_CZ = 6912
P_SMALL = 7424
P_COLS = 7680
PROJ_TN = 512
FFN_TF = 256
Q_TILE = 128
SCAN_GROUP = 6
VMEM_LIMIT = 56 * 1024 * 1024


def _dot(a, b):
    return jnp.dot(a.astype(BF16), b.astype(BF16), preferred_element_type=F32)


def _dot_nt(a, b):
    return lax.dot_general(a.astype(BF16), b.astype(BF16), (((1,), (1,)), ((), ())),
                           preferred_element_type=F32)


def _dot_tn(a, b):
    return lax.dot_general(a.astype(BF16), b.astype(BF16), (((0,), (0,)), ((), ())),
                           preferred_element_type=F32)


def _split(a):
    hi = a.astype(BF16)
    lo = (a - hi.astype(F32)).astype(BF16)
    return hi, lo


def _dot3(a, b):
    ah, al = _split(a)
    bh, bl = _split(b)
    d = functools.partial(jnp.dot, preferred_element_type=F32)
    return d(ah, bh) + (d(al, bh) + d(ah, bl))


def _sigmoid(x):
    return 1.0 / (1.0 + jnp.exp(-x))


def _silu(x):
    return x * _sigmoid(x)


def _softplus(x):
    return jnp.maximum(x, 0.0) + jnp.log1p(jnp.exp(-jnp.abs(x)))


def _logsigmoid(x):
    return -_softplus(-x)


def _modnorm(x, w, shift, scale):
    y = x * lax.rsqrt(jnp.mean(x * x, axis=-1, keepdims=True) + EPS)
    return (y * w) * (1.0 + scale) + shift


def _params(sem, vmem=VMEM_LIMIT):
    return pltpu.CompilerParams(dimension_semantics=sem, vmem_limit_bytes=vmem)


def _resident(shape, index_map):
    return pl.BlockSpec(shape, index_map, pipeline_mode=pl.Buffered(1))


def _ada_kernel(c_ref, w_ref, b_ref, o_ref):
    o_ref[...] = _dot3(_silu(c_ref[...]), w_ref[...]) + b_ref[...]


def _ada_call(cc, ada_w, ada_b):
    depth, d, n = ada_w.shape
    tn = 1536
    return pl.pallas_call(
        _ada_kernel,
        grid=(depth, n // tn),
        in_specs=[pl.BlockSpec(cc.shape, lambda l, j: (0, 0)),
                  pl.BlockSpec((None, d, tn), lambda l, j: (l, 0, j)),
                  pl.BlockSpec((None, 1, tn), lambda l, j: (l, 0, j))],
        out_specs=pl.BlockSpec((None, cc.shape[0], tn), lambda l, j: (l, 0, j)),
        out_shape=jax.ShapeDtypeStruct((depth, cc.shape[0], n), F32),
        compiler_params=_params(("arbitrary", "arbitrary")),
        name="ada",
    )(cc, ada_w, ada_b.reshape(depth, 1, n))


def _proj_kernel(h_ref, mod_ref, nw_ref, w_ref, p_ref):
    mod = mod_ref[...]
    xn = _modnorm(h_ref[...], nw_ref[...], mod[0:1], mod[1:2]).astype(BF16)
    for j in range(P_COLS // PROJ_TN):
        sl = slice(j * PROJ_TN, (j + 1) * PROJ_TN)
        p_ref[:, sl] = jnp.dot(xn, w_ref[:, sl], preferred_element_type=F32)


def _proj_call(h, mod, nw, w, tm, nct):
    b, t, d = h.shape
    return pl.pallas_call(
        _proj_kernel,
        grid=(b, t // tm),
        in_specs=[pl.BlockSpec((None, tm, d), lambda i, j: (i, j, 0)),
                  pl.BlockSpec((None, None, 6, d), lambda i, j: (i, jnp.minimum(j // nct, 1), 0, 0)),
                  pl.BlockSpec((1, d), lambda i, j: (0, 0)),
                  _resident((d, P_COLS), lambda i, j: (0, 0))],
        out_specs=pl.BlockSpec((None, tm, P_COLS), lambda i, j: (i, j, 0)),
        out_shape=jax.ShapeDtypeStruct((b, t, P_COLS), F32),
        compiler_params=_params(("arbitrary", "arbitrary")),
        name="proj_in",
    )(h, mod, nw, w)


def _chunk_masks():
    row = lax.broadcasted_iota(jnp.int32, (CHUNK, CHUNK), 0)
    col = lax.broadcasted_iota(jnp.int32, (CHUNK, CHUNK), 1)
    return col <= row, col >= row, col < row, col > row


def _bwd_chunk(it, nc, ncc):
    return jnp.where(it < ncc, ncc - 1 - it, nc - 1 - (it - ncc))


def _mlstm_kernel(qk_ref, v_ref, o_ref, gcol_ref, grow_ref, brow_ref, bcol_ref, nw_ref, y_ref,
                  hf_ref, hb_ref, *, nc, ncc):
    L = CHUNK
    le, ge, _, _ = _chunk_masks()
    lane = lax.broadcasted_iota(jnp.int32, (L, LANES), 1)
    ones_blk = jnp.where(lane == 0, 1.0, 0.0).astype(F32)
    bias_row = brow_ref[...]
    bias_col = bcol_ref[...]

    grp = _group_size(nc)
    h_refs = (hf_ref, hb_ref)

    def gates(c, d, m):
        mask_in, mask_t = (le, ge) if d == 0 else (ge, le)
        r0 = pl.multiple_of(c * L, L)
        gc = gcol_ref[pl.ds(r0, L), :] + bias_row
        gr = grow_ref[c] + bias_col
        i_col = gc[:, 2 * d:2 * d + 1]
        f_col = _logsigmoid(gc[:, 2 * d + 1:2 * d + 2])
        i_row = gr[2 * d:2 * d + 1, :]
        f_row = _logsigmoid(gr[2 * d + 1:2 * d + 2, :])
        bcum_col = jnp.sum(jnp.where(mask_in, f_row, 0.0), axis=1, keepdims=True)
        bcum_row = jnp.sum(jnp.where(mask_t, f_col, 0.0), axis=0, keepdims=True)
        dlog = jnp.where(mask_in, bcum_col + (i_row - bcum_row), NEG)
        inter = bcum_col + m
        mt = jnp.maximum(inter, jnp.max(dlog, axis=1, keepdims=True))
        btot = jnp.sum(f_row, axis=1, keepdims=True)
        glog = btot - bcum_col + i_col
        m_new = jnp.maximum(btot + m, jnp.max(glog, axis=0, keepdims=True))
        return dict(r0=r0, d=d, p=jnp.exp(dlog - mt), e_inter=jnp.exp(inter - mt), e_mt=jnp.exp(-mt),
                    wk=jnp.exp(glog - m_new), decay=jnp.exp(btot + m - m_new)), m_new

    def body(gi, carry):
        C = [carry[0], carry[2]]
        m = [carry[1], carry[3]]
        chains = []
        for j in range(grp):
            it = gi * grp + j
            for d, c in ((0, it), (1, _bwd_chunk(it, nc, ncc))):
                ch, m[d] = gates(c, d, m[d])
                x = qk_ref[pl.ds(ch["r0"], L), :]
                ch["q"] = x[:, :A_DQK] * (A_DQK ** -0.5)
                ch["k"] = x[:, A_DQK:]
                ch["v_ext"] = jnp.concatenate([v_ref[pl.ds(ch["r0"], L), :], ones_blk], axis=1)
                chains.append(ch)
        sq = [_dot_nt(ch["q"], ch["k"]) for ch in chains]
        kv = [_dot_tn(ch["k"] * ch["wk"], ch["v_ext"]) for ch in chains]
        sv = [_dot(s * ch["p"], ch["v_ext"]) for s, ch in zip(sq, chains)]
        qc = []
        for ch, kv_c in zip(chains, kv):
            d = ch["d"]
            qc.append(_dot(ch["q"], C[d]))
            C[d] = ch["decay"] * C[d] + kv_c
        for ch, sv_c, qc_c in zip(chains, sv, qc):
            num_ext = sv_c + ch["e_inter"] * qc_c
            den = num_ext[:, A_DV:A_DV + 1]
            hc = num_ext[:, :A_DV] / jnp.maximum(jnp.abs(den), ch["e_mt"])
            h_refs[ch["d"]][pl.ds(ch["r0"], L), :] = hc
        return C[0], m[0], C[1], m[1]

    c0 = jnp.zeros((A_DQK, 2 * LANES), F32)
    m0 = jnp.full((1, 1), M_INIT, F32)
    lax.fori_loop(0, nc // grp, body, (c0, m0, c0, m0))

    hsum = hf_ref[...] + hb_ref[...]
    hn = hsum * lax.rsqrt(jnp.mean(hsum * hsum, axis=-1, keepdims=True) + EPS)
    y_ref[...] = hn * nw_ref[...] * _sigmoid(o_ref[...])


def _mlstm_call(p, gcol, grow, brow, bcol, nw, ncc):
    b, t, _ = p.shape
    nc = t // CHUNK
    blk = lambda base: pl.BlockSpec((None, t, LANES), lambda i, h: (i, 0, base // LANES + h))
    return pl.pallas_call(
        functools.partial(_mlstm_kernel, nc=nc, ncc=ncc),
        grid=(b, A_HEADS),
        in_specs=[blk(P_AQK), blk(P_AV), blk(P_AO),
                  pl.BlockSpec((None, None, t, 4), lambda i, h: (i, h, 0, 0)),
                  pl.BlockSpec((None, None, nc, 4, CHUNK), lambda i, h: (i, h, 0, 0, 0)),
                  pl.BlockSpec((None, 1, 4), lambda i, h: (h, 0, 0)),
                  pl.BlockSpec((None, 4, 1), lambda i, h: (h, 0, 0)),
                  pl.BlockSpec((1, LANES), lambda i, h: (0, h))],
        out_specs=pl.BlockSpec((None, t, LANES), lambda i, h: (i, 0, h)),
        out_shape=jax.ShapeDtypeStruct((b, t, BRANCH_W), F32),
        scratch_shapes=[pltpu.VMEM((t, LANES), F32), pltpu.VMEM((t, LANES), F32)],
        compiler_params=_params(("arbitrary", "arbitrary")),
        name="mlstm",
    )(p, p, p, gcol, grow, brow, bcol, nw)


def _headnorm_rope(x, bd, w, cos, sin_signed, lane_half):
    hi, lo = _split(x * x)
    ssum = jnp.dot(hi, bd, preferred_element_type=F32) + jnp.dot(lo, bd, preferred_element_type=F32)
    y = x * lax.rsqrt(ssum * (1.0 / B_DH) + EPS) * w
    swapped = jnp.where(lane_half, pltpu.roll(y, LANES - 16, axis=1), pltpu.roll(y, 16, axis=1))
    return y * cos + swapped * sin_signed


def _gqa_kernel(q_ref, k_ref, v_ref, qw_ref, kw_ref, cosq_ref, sinq_ref, cosk_ref, sink_ref, y_ref,
                kn_ref, vb_ref, *, tc, nqc):
    tq = q_ref.shape[0]
    t = k_ref.shape[0]
    qb = pl.program_id(1)
    r = lax.broadcasted_iota(jnp.int32, (LANES, LANES), 0)
    c = lax.broadcasted_iota(jnp.int32, (LANES, LANES), 1)
    bd = jnp.where(r // B_DH == c // B_DH, 1.0, 0.0).astype(BF16)

    def lane_half(n):
        return (lax.broadcasted_iota(jnp.int32, (n, LANES), 1) % 32) < 16

    @pl.when(qb == 0)
    def _():
        kn = _headnorm_rope(k_ref[...], bd, kw_ref[...], cosk_ref[...], sink_ref[...], lane_half(t))
        kn_ref[...] = kn.astype(BF16)
        vb_ref[...] = v_ref[...].astype(BF16)

    qn = []
    for j in range(B_HEADS * B_DH // LANES):
        sl = slice(j * LANES, (j + 1) * LANES)
        y = _headnorm_rope(q_ref[:, sl], bd, qw_ref[...], cosq_ref[...], sinq_ref[...], lane_half(tq))
        qn.append((y * (B_DH ** -0.5)).astype(BF16))

    group = B_HEADS // B_KV_HEADS

    def attend(nk):
        for kvh in range(B_KV_HEADS):
            kh = kn_ref[0:nk, kvh * B_DH:(kvh + 1) * B_DH]
            vh = vb_ref[0:nk, kvh * B_DH:(kvh + 1) * B_DH]
            heads = [kvh * group + g for g in range(group)]
            qs = jnp.concatenate(
                [qn[h // 2][:, (h % 2) * B_DH:(h % 2 + 1) * B_DH] for h in heads], axis=0)
            s = lax.dot_general(qs, kh, (((1,), (1,)), ((), ())), preferred_element_type=F32)
            p = jnp.exp(s - jnp.max(s, axis=-1, keepdims=True))
            l = jnp.sum(p, axis=-1, keepdims=True)
            o = jnp.dot(p.astype(BF16), vh, preferred_element_type=F32) / l
            for g, h in enumerate(heads):
                y_ref[:, h * B_DH:(h + 1) * B_DH] = o[g * tq:(g + 1) * tq]

    @pl.when(qb < nqc)
    def _():
        attend(tc)

    @pl.when(qb >= nqc)
    def _():
        attend(t)


def _gqa_call(p, qw, kw, cos2, sin2, tc):
    b, t, _ = p.shape
    tq = Q_TILE
    row_blk = lambda: pl.BlockSpec((tq, LANES), lambda i, j: (j, 0))
    full = lambda: pl.BlockSpec((t, LANES), lambda i, j: (0, 0))
    return pl.pallas_call(
        functools.partial(_gqa_kernel, tc=tc, nqc=tc // tq),
        grid=(b, t // tq),
        in_specs=[pl.BlockSpec((None, tq, B_HEADS * B_DH), lambda i, j: (i, j, P_BQ // (B_HEADS * B_DH))),
                  pl.BlockSpec((None, t, LANES), lambda i, j: (i, 0, P_BK // LANES)),
                  pl.BlockSpec((None, t, LANES), lambda i, j: (i, 0, P_BV // LANES)),
                  pl.BlockSpec((1, LANES), lambda i, j: (0, 0)),
                  pl.BlockSpec((1, LANES), lambda i, j: (0, 0)),
                  row_blk(), row_blk(), full(), full()],
        out_specs=pl.BlockSpec((None, tq, BRANCH_W), lambda i, j: (i, j, 0)),
        out_shape=jax.ShapeDtypeStruct((b, t, BRANCH_W), F32),
        scratch_shapes=[pltpu.VMEM((t, LANES), BF16), pltpu.VMEM((t, LANES), BF16)],
        compiler_params=_params(("arbitrary", "arbitrary")),
        name="gqa",
    )(p, p, p, qw, kw, cos2, sin2, cos2, sin2)


def _group_size(nc):
    return max(g for g in range(1, SCAN_GROUP + 1) if nc % g == 0)


def _gdn_kernel(q_ref, k_ref, v_ref, z_ref, cwq_ref, cwk_ref, cwv_ref, gcol_ref, grow_ref, par_ref,
                nw_ref, y_ref, qn_ref, kn_ref, vn_ref, n_ref, kq_ref, egl_ref, o_ref, *, nc, ncc):
    L = CHUNK
    t = q_ref.shape[0]
    tc = ncc * L
    le, ge, lt, gt = _chunk_masks()
    par = par_ref[...]

    rows = lax.broadcasted_iota(jnp.int32, (t, 1), 0)
    has_prev = jnp.logical_and(rows != 0, rows != tc)
    has_next = jnp.logical_and(rows != tc - 1, rows != t - 1)

    def conv_silu(x_ref, cw_ref):
        x = x_ref[...]
        cw = cw_ref[...]
        prev = jnp.where(has_prev, pltpu.roll(x, 1, axis=0), 0.0)
        nxt = jnp.where(has_next, pltpu.roll(x, t - 1, axis=0), 0.0)
        return _silu(prev * cw[0:1] + x * cw[1:2] + nxt * cw[2:3])

    def l2n(x):
        return x * lax.rsqrt(jnp.sum(x * x, axis=-1, keepdims=True) + EPS)

    qn_ref[...] = l2n(conv_silu(q_ref, cwq_ref)) * (C_DK ** -0.5)
    kn_ref[...] = l2n(conv_silu(k_ref, cwk_ref))
    vn_ref[...] = conv_silu(v_ref, cwv_ref)

    grp = _group_size(nc)
    eye = jnp.where(le & ge, 1.0, 0.0).astype(F32)

    def prep(gi, carry):
        chunks = [gi * grp + j for j in range(grp)]
        rows0 = [pl.multiple_of(c * L, L) for c in chunks]
        qs = [qn_ref[pl.ds(r0, L), :] for r0 in rows0]
        ks = [kn_ref[pl.ds(r0, L), :] for r0 in rows0]
        vs = [vn_ref[pl.ds(r0, L), :] for r0 in rows0]
        kkqk = [_dot_nt(jnp.concatenate([k, q], axis=0), k) for k, q in zip(ks, qs)]
        chains = []
        for j, c in enumerate(chunks):
            gc = gcol_ref[pl.ds(rows0[j], L), :]
            gr = grow_ref[c]
            for d in range(2):
                mask_in, mask_t, strict = (le, ge, lt) if d == 0 else (ge, le, gt)
                neg_rate = -jnp.exp(par[:, d:d + 1])
                g_col = neg_rate * _softplus(gc[:, d:d + 1] + par[:, 2 + d:3 + d])
                g_row = neg_rate * _softplus(gr[d:d + 1, :] + par[:, 2 + d:3 + d])
                beta = _sigmoid(gc[:, 2 + d:3 + d])
                G_col = jnp.sum(jnp.where(mask_in, g_row, 0.0), axis=1, keepdims=True)
                G_row = jnp.sum(jnp.where(mask_t, g_col, 0.0), axis=0, keepdims=True)
                dec = jnp.exp(jnp.where(mask_in, G_col - G_row, NEG))
                g_last = jnp.sum(g_row, axis=1, keepdims=True)
                chains.append(dict(j=j, c=c, d=d, beta=beta, dec=dec, eG=jnp.exp(G_col),
                                   kscale=jnp.exp(g_last - G_col), egl=jnp.exp(g_last),
                                   x=-jnp.where(strict, beta * kkqk[j][:L] * dec, 0.0)))
        invs = [eye + ch["x"] for ch in chains]
        pws = [ch["x"] for ch in chains]
        n = 1
        while 2 * n < L:
            pws = [_dot(pw, pw) for pw in pws]
            invs = [inv + _dot(inv, pw) for inv, pw in zip(invs, pws)]
            n *= 2
        uws = []
        for ch, inv in zip(chains, invs):
            k, v = ks[ch["j"]], vs[ch["j"]]
            rhs = jnp.concatenate([v * ch["beta"], k * (ch["beta"] * ch["eG"])], axis=1)
            uws.append(_dot(inv, rhs))
        kns = [_dot_tn(ks[ch["j"]] * ch["kscale"], uw) for ch, uw in zip(chains, uws)]
        qos = [_dot(kkqk[ch["j"]][L:] * ch["dec"], uw) for ch, uw in zip(chains, uws)]
        for ch, kn, qo in zip(chains, kns, qos):
            d, c, r0 = ch["d"], ch["c"], rows0[ch["j"]]
            n_ref[d, c] = kn[:, :C_DV]
            kq_ref[d, c, 0:C_DK, :] = kn[:, C_DV:].astype(BF16)
            kq_ref[d, c, C_DK:C_DK + L, :] = (qs[ch["j"]] * ch["eG"] - qo[:, C_DV:]).astype(BF16)
            o_ref[d, pl.ds(r0, L), :] = qo[:, :C_DV]
            egl_ref[d, c] = jnp.broadcast_to(ch["egl"], (8, LANES))
        return carry

    lax.fori_loop(0, nc // grp, prep, 0)

    def step(c, d, S):
        r0 = pl.multiple_of(c * L, L)
        ks_qs = jnp.dot(kq_ref[d, c], S.astype(BF16), preferred_element_type=F32)
        o_ref[d, pl.ds(r0, L), :] += ks_qs[C_DK:]
        return S * egl_ref[d, c][0:1, 0:1] + (n_ref[d, c] - ks_qs[:C_DK])

    def body(it, carry):
        Sf, Sb = carry
        return step(it, 0, Sf), step(_bwd_chunk(it, nc, ncc), 1, Sb)

    s0 = jnp.zeros((C_DK, C_DV), F32)
    lax.fori_loop(0, nc, body, (s0, s0))

    osum = o_ref[0] + o_ref[1]
    on = osum * lax.rsqrt(jnp.mean(osum * osum, axis=-1, keepdims=True) + EPS) * nw_ref[...]
    y_ref[...] = on * _silu(z_ref[...])


def _gdn_call(p, cw, gcol, grow, par, nw, ncc):
    b, t, _ = p.shape
    nc = t // CHUNK
    blk = lambda base: pl.BlockSpec((None, t, LANES), lambda i, h: (i, 0, base // LANES + h))
    cwb = lambda off: pl.BlockSpec((3, LANES), lambda i, h: (0, off + h))
    big = lambda dt: pltpu.VMEM((2, t, LANES), dt)
    return pl.pallas_call(
        functools.partial(_gdn_kernel, nc=nc, ncc=ncc),
        grid=(b, C_HEADS),
        in_specs=[blk(P_CQ), blk(P_CK), blk(P_CV), blk(P_CZ), cwb(0), cwb(C_HEADS), cwb(2 * C_HEADS),
                  pl.BlockSpec((None, None, t, 4), lambda i, h: (i, h, 0, 0)),
                  pl.BlockSpec((None, None, nc, 4, CHUNK), lambda i, h: (i, h, 0, 0, 0)),
                  pl.BlockSpec((None, 1, 4), lambda i, h: (h, 0, 0)),
                  pl.BlockSpec((1, LANES), lambda i, h: (0, 0))],
        out_specs=pl.BlockSpec((None, t, LANES), lambda i, h: (i, 0, h)),
        out_shape=jax.ShapeDtypeStruct((b, t, BRANCH_W), F32),
        scratch_shapes=[pltpu.VMEM((t, LANES), F32), pltpu.VMEM((t, LANES), F32),
                        pltpu.VMEM((t, LANES), F32),
                        pltpu.VMEM((2, nc, C_DK, C_DV), F32),
                        pltpu.VMEM((2, nc, C_DK + CHUNK, C_DV), BF16),
                        pltpu.VMEM((2, nc, 8, LANES), F32),
                        big(F32)],
        compiler_params=_params(("arbitrary", "arbitrary")),
        name="gdn",
    )(p, p, p, p, cw, cw, cw, gcol, grow, par, nw)


def _merge_kernel(ya_ref, yb_ref, yc_ref, ga_ref, gb_ref, gc_ref, h_ref, mod_ref, wb_ref, wo_ref, o_ref):
    y = (_sigmoid(ga_ref[...]) * _dot(ya_ref[...], wb_ref[0])
         + _sigmoid(gb_ref[...]) * _dot(yb_ref[...], wb_ref[1])
         + _sigmoid(gc_ref[...]) * _dot(yc_ref[...], wb_ref[2]))
    o_ref[...] = h_ref[...] + mod_ref[2:3, :] * _dot(y, wo_ref[...])


def _merge_call(ya, yb, yc, p, h, mod, wb, wo, tm, nct):
    b, t, d = h.shape
    yblk = lambda: pl.BlockSpec((None, tm, BRANCH_W), lambda i, j: (i, j, 0))
    gblk = lambda g: pl.BlockSpec((None, tm, d), lambda i, j: (i, j, g))
    return pl.pallas_call(
        _merge_kernel,
        grid=(b, t // tm),
        in_specs=[yblk(), yblk(), yblk(), gblk(0), gblk(1), gblk(2),
                  pl.BlockSpec((None, tm, d), lambda i, j: (i, j, 0)),
                  pl.BlockSpec((None, None, 6, d), lambda i, j: (i, jnp.minimum(j // nct, 1), 0, 0)),
                  _resident((3, BRANCH_W, d), lambda i, j: (0, 0, 0)),
                  _resident((d, d), lambda i, j: (0, 0))],
        out_specs=pl.BlockSpec((None, tm, d), lambda i, j: (i, j, 0)),
        out_shape=jax.ShapeDtypeStruct((b, t, d), F32),
        compiler_params=_params(("arbitrary", "arbitrary")),
        name="merge",
    )(ya, yb, yc, p, p, p, h, mod, wb, wo)


def _ffn_kernel(h_ref, hp_ref, hn_ref, mod_ref, nw_ref, wup_ref, cw_ref, wdn_ref, o_ref, acc_ref,
                *, nct, ntiles):
    tm = h_ref.shape[0]
    j = pl.program_id(1)
    mod = mod_ref[...]
    nw = nw_ref[...]
    h = h_ref[...]
    xn = _modnorm(h, nw, mod[3:4], mod[4:5]).astype(BF16)
    halo = jnp.concatenate([hp_ref[...], hn_ref[...]], axis=0)
    xh = _modnorm(halo, nw, mod[3:4], mod[4:5]).astype(BF16)
    has_prev = jnp.logical_and(j != 0, j != nct).astype(F32)
    has_next = jnp.logical_and(j != nct - 1, j != ntiles - 1).astype(F32)
    rows = lax.broadcasted_iota(jnp.int32, (tm, 1), 0)
    first = rows == 0
    last = rows == tm - 1

    def up_conv(kind, jf):
        u = jnp.dot(xn, wup_ref[kind, jf], preferred_element_type=F32)
        uh = jnp.dot(xh, wup_ref[kind, jf], preferred_element_type=F32)
        cw = cw_ref[kind, jf]
        prev = jnp.where(first, uh[7:8] * has_prev, pltpu.roll(u, 1, axis=0))
        nxt = jnp.where(last, uh[8:9] * has_next, pltpu.roll(u, tm - 1, axis=0))
        return prev * cw[0:1] + u * cw[1:2] + nxt * cw[2:3]

    for jf in range(wup_ref.shape[1]):
        act = (up_conv(0, jf) * _silu(up_conv(1, jf))).astype(BF16)
        part = jnp.dot(act, wdn_ref[jf], preferred_element_type=F32)
        if jf == 0:
            acc_ref[...] = part
        else:
            acc_ref[...] += part
    o_ref[...] = h + mod[5:6] * acc_ref[...]


def _ffn_call(h, mod, nw, wup, cw, wdn, tm, nct):
    b, t, d = h.shape
    ntiles = t // tm
    hb = tm // 8
    return pl.pallas_call(
        functools.partial(_ffn_kernel, nct=nct, ntiles=ntiles),
        grid=(b, ntiles),
        in_specs=[pl.BlockSpec((None, tm, d), lambda i, j: (i, j, 0)),
                  pl.BlockSpec((None, 8, d), lambda i, j: (i, jnp.maximum(j * hb - 1, 0), 0)),
                  pl.BlockSpec((None, 8, d), lambda i, j: (i, jnp.minimum((j + 1) * hb, t // 8 - 1), 0)),
                  pl.BlockSpec((None, None, 6, d), lambda i, j: (i, jnp.minimum(j // nct, 1), 0, 0)),
                  pl.BlockSpec((1, d), lambda i, j: (0, 0)),
                  _resident(wup.shape, lambda i, j: (0, 0, 0, 0)),
                  _resident(cw.shape, lambda i, j: (0, 0, 0, 0)),
                  _resident(wdn.shape, lambda i, j: (0, 0, 0))],
        out_specs=pl.BlockSpec((None, tm, d), lambda i, j: (i, j, 0)),
        out_shape=jax.ShapeDtypeStruct((b, t, d), F32),
        scratch_shapes=[pltpu.VMEM((tm, d), F32)],
        compiler_params=_params(("arbitrary", "arbitrary")),
        name="ffn",
    )(h, h, h, mod, nw, wup, cw, wdn)


def _rope_tables(tc, tl):
    rows = tl // GRID_W
    row = jnp.repeat(jnp.arange(rows, dtype=F32), GRID_W)
    col = jnp.tile(jnp.arange(GRID_W, dtype=F32), rows)
    n_freq = B_DH // 4
    inv = ROPE_BASE ** (-jnp.arange(n_freq, dtype=F32) / n_freq)
    ang_r = row[:, None] * inv
    ang_c = col[:, None] * inv
    cos = jnp.concatenate([jnp.cos(ang_r)] * 2 + [jnp.cos(ang_c)] * 2, axis=1)
    sin = jnp.concatenate([-jnp.sin(ang_r), jnp.sin(ang_r), -jnp.sin(ang_c), jnp.sin(ang_c)], axis=1)
    cos = jnp.concatenate([jnp.ones((tc, B_DH), F32), cos], axis=0)
    sin = jnp.concatenate([jnp.zeros((tc, B_DH), F32), sin], axis=0)
    return jnp.tile(cos, (1, 2)), jnp.tile(sin, (1, 2))


def _pack_w_in(w_in):
    depth, d, _ = w_in.shape
    o = 0
    parts = {}
    for name, width in (("aq", 256), ("ak", 256), ("av", 512), ("ao", 512), ("ag", 16),
                        ("bq", 512), ("bk", 128), ("bv", 128),
                        ("cq", 512), ("ck", 512), ("cv", 512), ("cz", 512), ("ca", 8), ("cb", 8),
                        ("gate", 3 * D_MODEL)):
        parts[name] = w_in[:, :, o:o + width]
        o += width
    aqk = jnp.concatenate([parts["aq"].reshape(depth, d, A_HEADS, A_DQK),
                           parts["ak"].reshape(depth, d, A_HEADS, A_DQK)], axis=3).reshape(depth, d, 512)
    small = jnp.concatenate([parts["ag"], parts["ca"], parts["cb"]], axis=2)
    pad = jnp.zeros((depth, d, P_COLS - P_SMALL - small.shape[2]), w_in.dtype)
    cols = [parts["gate"], aqk, parts["av"], parts["ao"], parts["bq"], parts["bk"], parts["bv"],
            parts["cq"], parts["ck"], parts["cv"], parts["cz"], small, pad]
    return jnp.concatenate(cols, axis=2).astype(BF16)


def _gate_layouts(x, heads, kinds):
    b, t, _ = x.shape
    g = x.reshape(b, t, kinds, heads)
    col = g.transpose(0, 3, 1, 2)
    row = g.reshape(b, t // CHUNK, CHUNK, kinds, heads).transpose(0, 4, 1, 3, 2)
    return col, row


def kernel(x, c, ctx, c_ctx, norm1_w, norm2_w, ada_w, ada_b, w_in, a_gate_b, a_norm_w, b_qnorm_w,
           b_knorm_w, c_conv_w, c_a_log, c_dt_bias, c_norm_w, w_branch, w_out, w_up, ffn_conv_w, w_down):
    b, tl, d = x.shape
    tc = ctx.shape[1]
    depth = w_in.shape[0]
    t = tc + tl
    tm = 256 if (tc % 256 == 0 and tl % 256 == 0) else 128
    nct = tc // tm
    ncc = tc // CHUNK
    nj = D_FF // FFN_TF

    cc = jnp.zeros((16, d), F32).at[:b].set(c).at[b].set(c_ctx)
    mods = _ada_call(cc, ada_w, ada_b).reshape(depth, 16, 6, d)
    mod_all = jnp.stack([jnp.broadcast_to(mods[:, b][:, None], (depth, b, 6, d)), mods[:, :b]], axis=2)

    w_in_p = _pack_w_in(w_in)
    wb = w_branch.astype(BF16)
    wo = w_out.astype(BF16)
    wup = w_up.astype(BF16).reshape(depth, d, 2, nj, FFN_TF).transpose(0, 2, 3, 1, 4)
    fcw = ffn_conv_w.reshape(depth, 3, 2, nj, FFN_TF).transpose(0, 2, 3, 1, 4)
    wdn = w_down.astype(BF16).reshape(depth, nj, FFN_TF, d)
    cos2, sin2 = _rope_tables(tc, tl)
    a_brow = a_gate_b.transpose(0, 2, 1)[:, :, None, :]
    a_bcol = a_gate_b.transpose(0, 2, 1)[:, :, :, None]
    c_par = jnp.concatenate([c_a_log, c_dt_bias], axis=1).transpose(0, 2, 1)[:, :, None, :]

    h = jnp.concatenate([ctx, x], axis=1)
    for l in range(depth):
        mod = mod_all[l]
        p = _proj_call(h, mod, norm1_w[l][None], w_in_p[l], tm, nct)
        small = p[:, :, P_SMALL:P_SMALL + 32]
        a_gcol, a_grow = _gate_layouts(small[:, :, :16], A_HEADS, 4)
        c_gcol, c_grow = _gate_layouts(small[:, :, 16:32], C_HEADS, 4)
        ya = _mlstm_call(p, a_gcol, a_grow, a_brow[l], a_bcol[l], a_norm_w[l][None], ncc)
        yb = _gqa_call(p, jnp.tile(b_qnorm_w[l], 2)[None], jnp.tile(b_knorm_w[l], 2)[None], cos2, sin2, tc)
        yc = _gdn_call(p, c_conv_w[l], c_gcol, c_grow, c_par[l], c_norm_w[l][None], ncc)
        h = _merge_call(ya, yb, yc, p, h, mod, wb[l], wo[l], tm, nct)
        h = _ffn_call(h, mod, norm2_w[l][None], wup[l], fcw[l], wdn[l], tm, nct)
    return h[:, tc:, :]
```

```python
import functools

import jax
import jax.numpy as jnp
from jax import lax
from jax.experimental import pallas as pl
from jax.experimental.pallas import tpu as pltpu

F32 = jnp.float32
BF16 = jnp.bfloat16

D_MODEL = 1024
GRID_W = 64
A_HEADS, A_DQK, A_DV = 4, 64, 128
B_HEADS, B_KV_HEADS, B_DH = 8, 2, 64
C_HEADS, C_DK, C_DV = 4, 128, 128
BRANCH_W = 512
D_FF = 2816
CHUNK = 64
ROPE_BASE = 10000.0
EPS = 1e-6
M_INIT = -1e30
NEG = -1e30

LANES = 128
P_GATE = 0
P_AQK = 3072
P_AV = 3584
P_AO = 4096
P_BQ = 4608
P_BK = 5120
P_BV = 5248
P_CQ = 5376
P_CK = 5888
P_CV = 6400
P_CZ = 6912
P_SMALL = 7424
P_COLS = 7680
PROJ_TN = 512
FFN_TF = 256
Q_TILE = 128
SCAN_GROUP = 12
VMEM_LIMIT = 56 * 1024 * 1024


def _dot(a, b):
    return jnp.dot(a.astype(BF16), b.astype(BF16), preferred_element_type=F32)


def _dot_nt(a, b):
    return lax.dot_general(a.astype(BF16), b.astype(BF16), (((1,), (1,)), ((), ())),
                           preferred_element_type=F32)


def _dot_tn(a, b):
    return lax.dot_general(a.astype(BF16), b.astype(BF16), (((0,), (0,)), ((), ())),
                           preferred_element_type=F32)


def _split(a):
    hi = a.astype(BF16)
    lo = (a - hi.astype(F32)).astype(BF16)
    return hi, lo


def _dot3(a, b):
    ah, al = _split(a)
    bh, bl = _split(b)
    d = functools.partial(jnp.dot, preferred_element_type=F32)
    return d(ah, bh) + (d(al, bh) + d(ah, bl))


def _sigmoid(x):
    return 1.0 / (1.0 + jnp.exp(-x))


def _silu(x):
    return x * _sigmoid(x)


def _softplus(x):
    return jnp.maximum(x, 0.0) + jnp.log1p(jnp.exp(-jnp.abs(x)))


def _logsigmoid(x):
    return -_softplus(-x)


def _modnorm(x, w, shift, scale):
    y = x * lax.rsqrt(jnp.mean(x * x, axis=-1, keepdims=True) + EPS)
    return (y * w) * (1.0 + scale) + shift


def _params(sem, vmem=VMEM_LIMIT):
    return pltpu.CompilerParams(dimension_semantics=sem, vmem_limit_bytes=vmem)


def _resident(shape, index_map):
    return pl.BlockSpec(shape, index_map, pipeline_mode=pl.Buffered(1))


def _ada_kernel(c_ref, w_ref, b_ref, o_ref):
    o_ref[...] = _dot3(_silu(c_ref[...]), w_ref[...]) + b_ref[...]


def _ada_call(cc, ada_w, ada_b):
    depth, d, n = ada_w.shape
    tn = 1536
    return pl.pallas_call(
        _ada_kernel,
        grid=(depth, n // tn),
        in_specs=[pl.BlockSpec(cc.shape, lambda l, j: (0, 0)),
                  pl.BlockSpec((None, d, tn), lambda l, j: (l, 0, j)),
                  pl.BlockSpec((None, 1, tn), lambda l, j: (l, 0, j))],
        out_specs=pl.BlockSpec((None, cc.shape[0], tn), lambda l, j: (l, 0, j)),
        out_shape=jax.ShapeDtypeStruct((depth, cc.shape[0], n), F32),
        compiler_params=_params(("arbitrary", "arbitrary")),
        name="ada",
    )(cc, ada_w, ada_b.reshape(depth, 1, n))


def _proj_kernel(h_ref, mod_ref, nw_ref, w_ref, p_ref):
    mod = mod_ref[...]
    xn = _modnorm(h_ref[...], nw_ref[...], mod[0:1], mod[1:2]).astype(BF16)
    for j in range(P_COLS // PROJ_TN):
        sl = slice(j * PROJ_TN, (j + 1) * PROJ_TN)
        p_ref[:, sl] = jnp.dot(xn, w_ref[:, sl], preferred_element_type=F32)


def _proj_call(h, mod, nw, w, tm, nct):
    b, t, d = h.shape
    return pl.pallas_call(
        _proj_kernel,
        grid=(b, t // tm),
        in_specs=[pl.BlockSpec((None, tm, d), lambda i, j: (i, j, 0)),
                  pl.BlockSpec((None, None, 6, d), lambda i, j: (i, jnp.minimum(j // nct, 1), 0, 0)),
                  pl.BlockSpec((1, d), lambda i, j: (0, 0)),
                  _resident((d, P_COLS), lambda i, j: (0, 0))],
        out_specs=pl.BlockSpec((None, tm, P_COLS), lambda i, j: (i, j, 0)),
        out_shape=jax.ShapeDtypeStruct((b, t, P_COLS), F32),
        compiler_params=_params(("arbitrary", "arbitrary")),
        name="proj_in",
    )(h, mod, nw, w)


def _chunk_masks():
    row = lax.broadcasted_iota(jnp.int32, (CHUNK, CHUNK), 0)
    col = lax.broadcasted_iota(jnp.int32, (CHUNK, CHUNK), 1)
    return col <= row, col >= row, col < row, col > row


def _bwd_chunk(it, nc, ncc):
    return jnp.where(it < ncc, ncc - 1 - it, nc - 1 - (it - ncc))


def _rows3(x, n_rows=16):
    hi = x.astype(BF16).astype(F32)
    mid = (x - hi).astype(BF16).astype(F32)
    lo = ((x - hi) - mid).astype(BF16).astype(F32)
    r = lax.broadcasted_iota(jnp.int32, (n_rows, x.shape[1]), 0)
    return jnp.where(r == 0, hi, jnp.where(r == 1, mid, jnp.where(r == 2, lo, 0.0))).astype(BF16)


def _mlstm_kernel(qkt_ref, v_ref, o_ref, grow_ref, bcol_ref, nw_ref, y_ref, hf_ref, hb_ref, *, nc, ncc):
    L = CHUNK
    le, ge, _, _ = _chunk_masks()
    vis = (ge, le)
    row_l = lax.broadcasted_iota(jnp.int32, (L, LANES), 0)
    col_l = lax.broadcasted_iota(jnp.int32, (L, LANES), 1)
    ones_l = jnp.ones((L, LANES), BF16)
    zeros_l = jnp.zeros((L, LANES), BF16)
    stat_rhs = [jnp.concatenate(
        [ones_l, jnp.where((col_l < L) & ((col_l >= row_l) if d == 0 else (col_l <= row_l)), 1.0, 0.0).astype(BF16)],
        axis=1) for d in range(2)]
    ones3 = jnp.where(lax.broadcasted_iota(jnp.int32, (16, LANES), 0) < 3, 1.0, 0.0).astype(BF16)
    tail_rhs = jnp.concatenate([jnp.zeros((16, 2 * LANES), BF16), ones3], axis=1)
    bias_col = bcol_ref[...]
    grp = _group_size(nc)
    h_refs = (hf_ref, hb_ref)

    def sum3(x):
        return x[0:1] + x[1:2] + x[2:3]

    def body(gi, carry):
        C = [carry[0], carry[2]]
        m = [carry[1], carry[3]]
        chains = []
        for j in range(grp):
            it = gi * grp + j
            for d, c in ((0, it), (1, _bwd_chunk(it, nc, ncc))):
                gr = grow_ref[c] + bias_col
                x = qkt_ref[c]
                chains.append(dict(d=d, r0=pl.multiple_of(c * L, L), i_row=gr[2 * d:2 * d + 1, :],
                                   f3=_rows3(_logsigmoid(gr[2 * d + 1:2 * d + 2, :])),
                                   qt=x[:A_DQK] * (A_DQK ** -0.5), kt=x[A_DQK:]))
        for ch in chains:
            ch["v"] = v_ref[pl.ds(ch["r0"], L), :].astype(BF16)
            ch["sqt"] = _dot_tn(ch["kt"], ch["qt"])
        for ch in chains:
            st = jnp.dot(ch["f3"], stat_rhs[ch["d"]], preferred_element_type=F32)
            ch["btot"] = sum3(st[:, :LANES])
            ch["bcum"] = sum3(st[:, LANES:LANES + L])
            ch["a_row"] = ch["i_row"] - ch["bcum"]
        for ch in chains:
            ch["a_col"] = lax.dot_general(_rows3(ch["a_row"]), ones3, (((0,), (0,)), ((), ())),
                                          preferred_element_type=F32)
        for ch in chains:
            d = ch["d"]
            m_new = jnp.maximum(ch["btot"] + m[d], ch["btot"] + jnp.max(ch["a_col"], axis=0, keepdims=True))
            dlog = jnp.where(vis[d], ch["bcum"] + ch["a_col"][:, :L], NEG)
            inter = ch["bcum"] + m[d][:, :L]
            mt = jnp.maximum(inter, jnp.max(dlog, axis=0, keepdims=True))
            ch["st"] = (ch["sqt"] * jnp.exp(dlog - mt)).astype(BF16)
            ch["qe"] = (ch["qt"] * jnp.exp(inter - mt)).astype(BF16)
            ch["e3"] = _rows3(jnp.exp(-mt))
            ch["kw"] = (ch["kt"] * jnp.exp(ch["btot"][:, :L] + ch["a_row"] - m_new[:, :L])).astype(BF16)
            decay = jnp.exp(ch["btot"] + m[d] - m_new)
            ch["decay"] = jnp.concatenate([decay, decay], axis=1)
            m[d] = m_new
        for ch in chains:
            ch["kv"] = jnp.dot(ch["kw"], jnp.concatenate([ch["v"], ones_l], axis=1), preferred_element_type=F32)
            lhs = jnp.concatenate([ch["st"], ch["e3"]], axis=0)
            rhs = jnp.concatenate([jnp.concatenate([ch["v"], ones_l, zeros_l], axis=1), tail_rhs], axis=0)
            ch["intra"] = lax.dot_general(lhs, rhs, (((0,), (0,)), ((), ())), preferred_element_type=F32)
        for ch in chains:
            d = ch["d"]
            ch["inter"] = lax.dot_general(ch["qe"], C[d].astype(BF16), (((0,), (0,)), ((), ())),
                                          preferred_element_type=F32)
            C[d] = ch["decay"] * C[d] + ch["kv"]
        for ch in chains:
            num = ch["intra"][:, :A_DV] + ch["inter"][:, :A_DV]
            den = ch["intra"][:, A_DV:2 * A_DV] + ch["inter"][:, A_DV:]
            h_refs[ch["d"]][pl.ds(ch["r0"], L), :] = num / jnp.maximum(jnp.abs(den), ch["intra"][:, 2 * A_DV:])
        return C[0], m[0], C[1], m[1]

    c0 = jnp.zeros((A_DQK, 2 * LANES), F32)
    m0 = jnp.full((1, LANES), M_INIT, F32)
    lax.fori_loop(0, nc // grp, body, (c0, m0, c0, m0))

    hsum = hf_ref[...] + hb_ref[...]
    hn = hsum * lax.rsqrt(jnp.mean(hsum * hsum, axis=-1, keepdims=True) + EPS)
    y_ref[...] = hn * nw_ref[...] * _sigmoid(o_ref[...])


def _mlstm_call(p, qkt, grow, bcol, nw, ncc):
    b, t, _ = p.shape
    nc = t // CHUNK
    blk = lambda base: pl.BlockSpec((None, t, LANES), lambda i, h: (i, 0, base // LANES + h))
    return pl.pallas_call(
        functools.partial(_mlstm_kernel, nc=nc, ncc=ncc),
        grid=(b, A_HEADS),
        in_specs=[pl.BlockSpec((None, None, nc, 2 * A_DQK, CHUNK), lambda i, h: (i, h, 0, 0, 0)),
                  blk(P_AV), blk(P_AO),
                  pl.BlockSpec((None, None, nc, 4, CHUNK), lambda i, h: (i, h, 0, 0, 0)),
                  pl.BlockSpec((None, 4, 1), lambda i, h: (h, 0, 0)),
                  pl.BlockSpec((1, LANES), lambda i, h: (0, h))],
        out_specs=pl.BlockSpec((None, t, LANES), lambda i, h: (i, 0, h)),
        out_shape=jax.ShapeDtypeStruct((b, t, BRANCH_W), F32),
        scratch_shapes=[pltpu.VMEM((t, LANES), F32), pltpu.VMEM((t, LANES), F32)],
        compiler_params=_params(("arbitrary", "arbitrary")),
        name="mlstm",
    )(qkt, p, p, grow, bcol, nw)


def _headnorm_rope(x, bd, w, cos, sin_signed, lane_half):
    hi, lo = _split(x * x)
    ssum = jnp.dot(hi, bd, preferred_element_type=F32) + jnp.dot(lo, bd, preferred_element_type=F32)
    y = x * lax.rsqrt(ssum * (1.0 / B_DH) + EPS) * w
    swapped = jnp.where(lane_half, pltpu.roll(y, LANES - 16, axis=1), pltpu.roll(y, 16, axis=1))
    return y * cos + swapped * sin_signed


def _gqa_kernel(q_ref, k_ref, v_ref, qw_ref, kw_ref, cosq_ref, sinq_ref, cosk_ref, sink_ref, y_ref,
                kn_ref, vb_ref, *, tc, nqc):
    tq = q_ref.shape[0]
    t = k_ref.shape[0]
    qb = pl.program_id(1)
    r = lax.broadcasted_iota(jnp.int32, (LANES, LANES), 0)
    c = lax.broadcasted_iota(jnp.int32, (LANES, LANES), 1)
    bd = jnp.where(r // B_DH == c // B_DH, 1.0, 0.0).astype(BF16)

    def lane_half(n):
        return (lax.broadcasted_iota(jnp.int32, (n, LANES), 1) % 32) < 16

    @pl.when(qb == 0)
    def _():
        kn = _headnorm_rope(k_ref[...], bd, kw_ref[...], cosk_ref[...], sink_ref[...], lane_half(t))
        kn_ref[...] = kn.astype(BF16)
        vb_ref[...] = v_ref[...].astype(BF16)

    qn = []
    for j in range(B_HEADS * B_DH // LANES):
        sl = slice(j * LANES, (j + 1) * LANES)
        y = _headnorm_rope(q_ref[:, sl], bd, qw_ref[...], cosq_ref[...], sinq_ref[...], lane_half(tq))
        qn.append((y * (B_DH ** -0.5)).astype(BF16))

    group = B_HEADS // B_KV_HEADS

    def attend(nk):
        for kvh in range(B_KV_HEADS):
            kh = kn_ref[0:nk, kvh * B_DH:(kvh + 1) * B_DH]
            vh = vb_ref[0:nk, kvh * B_DH:(kvh + 1) * B_DH]
            heads = [kvh * group + g for g in range(group)]
            qs = jnp.concatenate(
                [qn[h // 2][:, (h % 2) * B_DH:(h % 2 + 1) * B_DH] for h in heads], axis=0)
            s = lax.dot_general(qs, kh, (((1,), (1,)), ((), ())), preferred_element_type=F32)
            p = jnp.exp(s - jnp.max(s, axis=-1, keepdims=True))
            l = jnp.sum(p, axis=-1, keepdims=True)
            o = jnp.dot(p.astype(BF16), vh, preferred_element_type=F32) / l
            for g, h in enumerate(heads):
                y_ref[:, h * B_DH:(h + 1) * B_DH] = o[g * tq:(g + 1) * tq]

    @pl.when(qb < nqc)
    def _():
        attend(tc)

    @pl.when(qb >= nqc)
    def _():
        attend(t)


def _gqa_call(p, qw, kw, cos2, sin2, tc):
    b, t, _ = p.shape
    tq = Q_TILE
    row_blk = lambda: pl.BlockSpec((tq, LANES), lambda i, j: (j, 0))
    full = lambda: pl.BlockSpec((t, LANES), lambda i, j: (0, 0))
    return pl.pallas_call(
        functools.partial(_gqa_kernel, tc=tc, nqc=tc // tq),
        grid=(b, t // tq),
        in_specs=[pl.BlockSpec((None, tq, B_HEADS * B_DH), lambda i, j: (i, j, P_BQ // (B_HEADS * B_DH))),
                  pl.BlockSpec((None, t, LANES), lambda i, j: (i, 0, P_BK // LANES)),
                  pl.BlockSpec((None, t, LANES), lambda i, j: (i, 0, P_BV // LANES)),
                  pl.BlockSpec((1, LANES), lambda i, j: (0, 0)),
                  pl.BlockSpec((1, LANES), lambda i, j: (0, 0)),
                  row_blk(), row_blk(), full(), full()],
        out_specs=pl.BlockSpec((None, tq, BRANCH_W), lambda i, j: (i, j, 0)),
        out_shape=jax.ShapeDtypeStruct((b, t, BRANCH_W), F32),
        scratch_shapes=[pltpu.VMEM((t, LANES), BF16), pltpu.VMEM((t, LANES), BF16)],
        compiler_params=_params(("arbitrary", "arbitrary")),
        name="gqa",
    )(p, p, p, qw, kw, cos2, sin2, cos2, sin2)


def _group_size(nc):
    return max(g for g in range(1, SCAN_GROUP + 1) if nc % g == 0)


def _gdn_kernel(q_ref, k_ref, v_ref, z_ref, cwq_ref, cwk_ref, cwv_ref, gcol_ref, grow_ref, par_ref,
                nw_ref, y_ref, qn_ref, kn_ref, vn_ref, n_ref, kq_ref, egl_ref, o_ref, *, nc, ncc):
    L = CHUNK
    t = q_ref.shape[0]
    tc = ncc * L
    le, ge, lt, gt = _chunk_masks()
    par = par_ref[...]

    rows = lax.broadcasted_iota(jnp.int32, (t, 1), 0)
    has_prev = jnp.logical_and(rows != 0, rows != tc)
    has_next = jnp.logical_and(rows != tc - 1, rows != t - 1)

    def conv_silu(x_ref, cw_ref):
        x = x_ref[...]
        cw = cw_ref[...]
        prev = jnp.where(has_prev, pltpu.roll(x, 1, axis=0), 0.0)
        nxt = jnp.where(has_next, pltpu.roll(x, t - 1, axis=0), 0.0)
        return _silu(prev * cw[0:1] + x * cw[1:2] + nxt * cw[2:3])

    def l2n(x):
        return x * lax.rsqrt(jnp.sum(x * x, axis=-1, keepdims=True) + EPS)

    qn_ref[...] = l2n(conv_silu(q_ref, cwq_ref)) * (C_DK ** -0.5)
    kn_ref[...] = l2n(conv_silu(k_ref, cwk_ref))
    vn_ref[...] = conv_silu(v_ref, cwv_ref)

    grp = _group_size(nc)
    eye = jnp.where(le & ge, 1.0, 0.0).astype(F32)

    def prep(gi, carry):
        chunks = [gi * grp + j for j in range(grp)]
        rows0 = [pl.multiple_of(c * L, L) for c in chunks]
        qs = [qn_ref[pl.ds(r0, L), :] for r0 in rows0]
        ks = [kn_ref[pl.ds(r0, L), :] for r0 in rows0]
        vs = [vn_ref[pl.ds(r0, L), :] for r0 in rows0]
        kkqk = [_dot_nt(jnp.concatenate([k, q], axis=0), k) for k, q in zip(ks, qs)]
        chains = []
        for j, c in enumerate(chunks):
            gc = gcol_ref[pl.ds(rows0[j], L), :]
            gr = grow_ref[c]
            for d in range(2):
                mask_in, mask_t, strict = (le, ge, lt) if d == 0 else (ge, le, gt)
                neg_rate = -jnp.exp(par[:, d:d + 1])
                g_col = neg_rate * _softplus(gc[:, d:d + 1] + par[:, 2 + d:3 + d])
                g_row = neg_rate * _softplus(gr[d:d + 1, :] + par[:, 2 + d:3 + d])
                beta = _sigmoid(gc[:, 2 + d:3 + d])
                G_col = jnp.sum(jnp.where(mask_in, g_row, 0.0), axis=1, keepdims=True)
                G_row = jnp.sum(jnp.where(mask_t, g_col, 0.0), axis=0, keepdims=True)
                dec = jnp.exp(jnp.where(mask_in, G_col - G_row, NEG))
                g_last = jnp.sum(g_row, axis=1, keepdims=True)
                chains.append(dict(j=j, c=c, d=d, beta=beta, dec=dec, eG=jnp.exp(G_col),
                                   kscale=jnp.exp(g_last - G_col), egl=jnp.exp(g_last),
                                   x=-jnp.where(strict, beta * kkqk[j][:L] * dec, 0.0)))
        invs = [eye + ch["x"] for ch in chains]
        pws = [ch["x"] for ch in chains]
        n = 1
        while 2 * n < L:
            pws = [_dot(pw, pw) for pw in pws]
            invs = [inv + _dot(inv, pw) for inv, pw in zip(invs, pws)]
            n *= 2
        uws = []
        for ch, inv in zip(chains, invs):
            k, v = ks[ch["j"]], vs[ch["j"]]
            rhs = jnp.concatenate([v * ch["beta"], k * (ch["beta"] * ch["eG"])], axis=1)
            uws.append(_dot(inv, rhs))
        kns = [_dot_tn(ks[ch["j"]] * ch["kscale"], uw) for ch, uw in zip(chains, uws)]
        qos = [_dot(kkqk[ch["j"]][L:] * ch["dec"], uw) for ch, uw in zip(chains, uws)]
        for ch, kn, qo in zip(chains, kns, qos):
            d, c, r0 = ch["d"], ch["c"], rows0[ch["j"]]
            n_ref[d, c] = kn[:, :C_DV]
            kq_ref[d, c, 0:C_DK, :] = kn[:, C_DV:].astype(BF16)
            kq_ref[d, c, C_DK:C_DK + L, :] = (qs[ch["j"]] * ch["eG"] - qo[:, C_DV:]).astype(BF16)
            o_ref[d, pl.ds(r0, L), :] = qo[:, :C_DV]
            egl_ref[d, c] = jnp.broadcast_to(ch["egl"], (8, LANES))
        return carry

    lax.fori_loop(0, nc // grp, prep, 0)

    def step(c, d, S):
        r0 = pl.multiple_of(c * L, L)
        ks_qs = jnp.dot(kq_ref[d, c], S.astype(BF16), preferred_element_type=F32)
        o_ref[d, pl.ds(r0, L), :] += ks_qs[C_DK:]
        return S * egl_ref[d, c][0:1, 0:1] + (n_ref[d, c] - ks_qs[:C_DK])

    def body(it, carry):
        Sf, Sb = carry
        return step(it, 0, Sf), step(_bwd_chunk(it, nc, ncc), 1, Sb)

    s0 = jnp.zeros((C_DK, C_DV), F32)
    lax.fori_loop(0, nc, body, (s0, s0))

    osum = o_ref[0] + o_ref[1]
    on = osum * lax.rsqrt(jnp.mean(osum * osum, axis=-1, keepdims=True) + EPS) * nw_ref[...]
    y_ref[...] = on * _silu(z_ref[...])


def _gdn_call(p, cw, gcol, grow, par, nw, ncc):
    b, t, _ = p.shape
    nc = t // CHUNK
    blk = lambda base: pl.BlockSpec((None, t, LANES), lambda i, h: (i, 0, base // LANES + h))
    cwb = lambda off: pl.BlockSpec((3, LANES), lambda i, h: (0, off + h))
    big = lambda dt: pltpu.VMEM((2, t, LANES), dt)
    return pl.pallas_call(
        functools.partial(_gdn_kernel, nc=nc, ncc=ncc),
        grid=(b, C_HEADS),
        in_specs=[blk(P_CQ), blk(P_CK), blk(P_CV), blk(P_CZ), cwb(0), cwb(C_HEADS), cwb(2 * C_HEADS),
                  pl.BlockSpec((None, None, t, 4), lambda i, h: (i, h, 0, 0)),
                  pl.BlockSpec((None, None, nc, 4, CHUNK), lambda i, h: (i, h, 0, 0, 0)),
                  pl.BlockSpec((None, 1, 4), lambda i, h: (h, 0, 0)),
                  pl.BlockSpec((1, LANES), lambda i, h: (0, 0))],
        out_specs=pl.BlockSpec((None, t, LANES), lambda i, h: (i, 0, h)),
        out_shape=jax.ShapeDtypeStruct((b, t, BRANCH_W), F32),
        scratch_shapes=[pltpu.VMEM((t, LANES), F32), pltpu.VMEM((t, LANES), F32),
                        pltpu.VMEM((t, LANES), F32),
                        pltpu.VMEM((2, nc, C_DK, C_DV), F32),
                        pltpu.VMEM((2, nc, C_DK + CHUNK, C_DV), BF16),
                        pltpu.VMEM((2, nc, 8, LANES), F32),
                        big(F32)],
        compiler_params=_params(("arbitrary", "arbitrary")),
        name="gdn",
    )(p, p, p, p, cw, cw, cw, gcol, grow, par, nw)


def _merge_kernel(ya_ref, yb_ref, yc_ref, ga_ref, gb_ref, gc_ref, h_ref, mod_ref, wb_ref, wo_ref, o_ref):
    y = (_sigmoid(ga_ref[...]) * _dot(ya_ref[...], wb_ref[0])
         + _sigmoid(gb_ref[...]) * _dot(yb_ref[...], wb_ref[1])
         + _sigmoid(gc_ref[...]) * _dot(yc_ref[...], wb_ref[2]))
    o_ref[...] = h_ref[...] + mod_ref[2:3, :] * _dot(y, wo_ref[...])


def _merge_call(ya, yb, yc, p, h, mod, wb, wo, tm, nct):
    b, t, d = h.shape
    yblk = lambda: pl.BlockSpec((None, tm, BRANCH_W), lambda i, j: (i, j, 0))
    gblk = lambda g: pl.BlockSpec((None, tm, d), lambda i, j: (i, j, g))
    return pl.pallas_call(
        _merge_kernel,
        grid=(b, t // tm),
        in_specs=[yblk(), yblk(), yblk(), gblk(0), gblk(1), gblk(2),
                  pl.BlockSpec((None, tm, d), lambda i, j: (i, j, 0)),
                  pl.BlockSpec((None, None, 6, d), lambda i, j: (i, jnp.minimum(j // nct, 1), 0, 0)),
                  _resident((3, BRANCH_W, d), lambda i, j: (0, 0, 0)),
                  _resident((d, d), lambda i, j: (0, 0))],
        out_specs=pl.BlockSpec((None, tm, d), lambda i, j: (i, j, 0)),
        out_shape=jax.ShapeDtypeStruct((b, t, d), F32),
        compiler_params=_params(("arbitrary", "arbitrary")),
        name="merge",
    )(ya, yb, yc, p, p, p, h, mod, wb, wo)


def _ffn_kernel(h_ref, hp_ref, hn_ref, mod_ref, nw_ref, wup_ref, cw_ref, wdn_ref, o_ref, acc_ref,
                *, nct, ntiles):
    tm = h_ref.shape[0]
    j = pl.program_id(1)
    mod = mod_ref[...]
    nw = nw_ref[...]
    h = h_ref[...]
    xn = _modnorm(h, nw, mod[3:4], mod[4:5]).astype(BF16)
    halo = jnp.concatenate([hp_ref[...], hn_ref[...]], axis=0)
    xh = _modnorm(halo, nw, mod[3:4], mod[4:5]).astype(BF16)
    has_prev = jnp.logical_and(j != 0, j != nct).astype(F32)
    has_next = jnp.logical_and(j != nct - 1, j != ntiles - 1).astype(F32)
    rows = lax.broadcasted_iota(jnp.int32, (tm, 1), 0)
    first = rows == 0
    last = rows == tm - 1

    def up_conv(kind, jf):
        u = jnp.dot(xn, wup_ref[kind, jf], preferred_element_type=F32)
        uh = jnp.dot(xh, wup_ref[kind, jf], preferred_element_type=F32)
        cw = cw_ref[kind, jf]
        prev = jnp.where(first, uh[7:8] * has_prev, pltpu.roll(u, 1, axis=0))
        nxt = jnp.where(last, uh[8:9] * has_next, pltpu.roll(u, tm - 1, axis=0))
        return prev * cw[0:1] + u * cw[1:2] + nxt * cw[2:3]

    for jf in range(wup_ref.shape[1]):
        act = (up_conv(0, jf) * _silu(up_conv(1, jf))).astype(BF16)
        part = jnp.dot(act, wdn_ref[jf], preferred_element_type=F32)
        if jf == 0:
            acc_ref[...] = part
        else:
            acc_ref[...] += part
    o_ref[...] = h + mod[5:6] * acc_ref[...]


def _ffn_call(h, mod, nw, wup, cw, wdn, tm, nct):
    b, t, d = h.shape
    ntiles = t // tm
    hb = tm // 8
    return pl.pallas_call(
        functools.partial(_ffn_kernel, nct=nct, ntiles=ntiles),
        grid=(b, ntiles),
        in_specs=[pl.BlockSpec((None, tm, d), lambda i, j: (i, j, 0)),
                  pl.BlockSpec((None, 8, d), lambda i, j: (i, jnp.maximum(j * hb - 1, 0), 0)),
                  pl.BlockSpec((None, 8, d), lambda i, j: (i, jnp.minimum((j + 1) * hb, t // 8 - 1), 0)),
                  pl.BlockSpec((None, None, 6, d), lambda i, j: (i, jnp.minimum(j // nct, 1), 0, 0)),
                  pl.BlockSpec((1, d), lambda i, j: (0, 0)),
                  _resident(wup.shape, lambda i, j: (0, 0, 0, 0)),
                  _resident(cw.shape, lambda i, j: (0, 0, 0, 0)),
                  _resident(wdn.shape, lambda i, j: (0, 0, 0))],
        out_specs=pl.BlockSpec((None, tm, d), lambda i, j: (i, j, 0)),
        out_shape=jax.ShapeDtypeStruct((b, t, d), F32),
        scratch_shapes=[pltpu.VMEM((tm, d), F32)],
        compiler_params=_params(("arbitrary", "arbitrary")),
        name="ffn",
    )(h, h, h, mod, nw, wup, cw, wdn)


def _rope_tables(tc, tl):
    rows = tl // GRID_W
    row = jnp.repeat(jnp.arange(rows, dtype=F32), GRID_W)
    col = jnp.tile(jnp.arange(GRID_W, dtype=F32), rows)
    n_freq = B_DH // 4
    inv = ROPE_BASE ** (-jnp.arange(n_freq, dtype=F32) / n_freq)
    ang_r = row[:, None] * inv
    ang_c = col[:, None] * inv
    cos = jnp.concatenate([jnp.cos(ang_r)] * 2 + [jnp.cos(ang_c)] * 2, axis=1)
    sin = jnp.concatenate([-jnp.sin(ang_r), jnp.sin(ang_r), -jnp.sin(ang_c), jnp.sin(ang_c)], axis=1)
    cos = jnp.concatenate([jnp.ones((tc, B_DH), F32), cos], axis=0)
    sin = jnp.concatenate([jnp.zeros((tc, B_DH), F32), sin], axis=0)
    return jnp.tile(cos, (1, 2)), jnp.tile(sin, (1, 2))


def _pack_w_in(w_in):
    depth, d, _ = w_in.shape
    o = 0
    parts = {}
    for name, width in (("aq", 256), ("ak", 256), ("av", 512), ("ao", 512), ("ag", 16),
                        ("bq", 512), ("bk", 128), ("bv", 128),
                        ("cq", 512), ("ck", 512), ("cv", 512), ("cz", 512), ("ca", 8), ("cb", 8),
                        ("gate", 3 * D_MODEL)):
        parts[name] = w_in[:, :, o:o + width]
        o += width
    aqk = jnp.concatenate([parts["aq"].reshape(depth, d, A_HEADS, A_DQK),
                           parts["ak"].reshape(depth, d, A_HEADS, A_DQK)], axis=3).reshape(depth, d, 512)
    small = jnp.concatenate([parts["ag"], parts["ca"], parts["cb"]], axis=2)
    pad = jnp.zeros((depth, d, P_COLS - P_SMALL - small.shape[2]), w_in.dtype)
    cols = [parts["gate"], aqk, parts["av"], parts["ao"], parts["bq"], parts["bk"], parts["bv"],
            parts["cq"], parts["ck"], parts["cv"], parts["cz"], small, pad]
    return jnp.concatenate(cols, axis=2).astype(BF16)


def _gate_layouts(x, heads, kinds):
    b, t, _ = x.shape
    g = x.reshape(b, t, kinds, heads)
    col = g.transpose(0, 3, 1, 2)
    row = g.reshape(b, t // CHUNK, CHUNK, kinds, heads).transpose(0, 4, 1, 3, 2)
    return col, row


def kernel(x, c, ctx, c_ctx, norm1_w, norm2_w, ada_w, ada_b, w_in, a_gate_b, a_norm_w, b_qnorm_w,
           b_knorm_w, c_conv_w, c_a_log, c_dt_bias, c_norm_w, w_branch, w_out, w_up, ffn_conv_w, w_down):
    b, tl, d = x.shape
    tc = ctx.shape[1]
    depth = w_in.shape[0]
    t = tc + tl
    tm = 256 if (tc % 256 == 0 and tl % 256 == 0) else 128
    nct = tc // tm
    ncc = tc // CHUNK
    nj = D_FF // FFN_TF

    cc = jnp.zeros((16, d), F32).at[:b].set(c).at[b].set(c_ctx)
    mods = _ada_call(cc, ada_w, ada_b).reshape(depth, 16, 6, d)
    mod_all = jnp.stack([jnp.broadcast_to(mods[:, b][:, None], (depth, b, 6, d)), mods[:, :b]], axis=2)

    w_in_p = _pack_w_in(w_in)
    wb = w_branch.astype(BF16)
    wo = w_out.astype(BF16)
    wup = w_up.astype(BF16).reshape(depth, d, 2, nj, FFN_TF).transpose(0, 2, 3, 1, 4)
    fcw = ffn_conv_w.reshape(depth, 3, 2, nj, FFN_TF).transpose(0, 2, 3, 1, 4)
    wdn = w_down.astype(BF16).reshape(depth, nj, FFN_TF, d)
    cos2, sin2 = _rope_tables(tc, tl)
    a_bcol = a_gate_b.transpose(0, 2, 1)[:, :, :, None]
    c_par = jnp.concatenate([c_a_log, c_dt_bias], axis=1).transpose(0, 2, 1)[:, :, None, :]

    h = jnp.concatenate([ctx, x], axis=1)
    for l in range(depth):
        mod = mod_all[l]
        p = _proj_call(h, mod, norm1_w[l][None], w_in_p[l], tm, nct)
        small = p[:, :, P_SMALL:P_SMALL + 32]
        _, a_grow = _gate_layouts(small[:, :, :16], A_HEADS, 4)
        qkt = (p[:, :, P_AQK:P_AQK + 2 * A_DQK * A_HEADS]
               .reshape(b, t // CHUNK, CHUNK, A_HEADS, 2 * A_DQK).transpose(0, 3, 1, 4, 2))
        c_gcol, c_grow = _gate_layouts(small[:, :, 16:32], C_HEADS, 4)
        ya = _mlstm_call(p, qkt, a_grow, a_bcol[l], a_norm_w[l][None], ncc)
        yb = _gqa_call(p, jnp.tile(b_qnorm_w[l], 2)[None], jnp.tile(b_knorm_w[l], 2)[None], cos2, sin2, tc)
        yc = _gdn_call(p, c_conv_w[l], c_gcol, c_grow, c_par[l], c_norm_w[l][None], ncc)
        h = _merge_call(ya, yb, yc, p, h, mod, wb[l], wo[l], tm, nct)
        h = _ffn_call(h, mod, norm2_w[l][None], wup[l], fcw[l], wdn[l], tm, nct)
    return h[:, tc:, :]
```

```python
import functools

import jax
import jax.numpy as jnp
from jax import lax
from jax.experimental import pallas as pl
from jax.experimental.pallas import tpu as pltpu

F32 = jnp.float32
BF16 = jnp.bfloat16

D_MODEL = 1024
GRID_W = 64
A_HEADS, A_DQK, A_DV = 4, 64, 128
B_HEADS, B_KV_HEADS, B_DH = 8, 2, 64
C_HEADS, C_DK, C_DV = 4, 128, 128
BRANCH_W = 512
D_FF = 2816
CHUNK = 64
ROPE_BASE = 10000.0
EPS = 1e-6
M_INIT = -1e30
LOG2_E = 1.4426950408889634
NEG = -1e30

LANES = 128
P_GATE = 0
P_AQK = 3072
P_AV = 3584
P_AO = 4096
P_BQ = 4608
P_BK = 5120
P_BV = 5248
P_CQ = 5376
P_CK = 5888
P_CV = 6400
P_CZ = 6912
P_SMALL = 7424
P_COLS = 7680
PROJ_TN = 512
FFN_TF = 256
Q_TILE = 128
SCAN_GROUP = 12
VMEM_LIMIT = 56 * 1024 * 1024


def _dot(a, b):
    return jnp.dot(a.astype(BF16), b.astype(BF16), preferred_element_type=F32)


def _dot_nt(a, b):
    return lax.dot_general(a.astype(BF16), b.astype(BF16), (((1,), (1,)), ((), ())),
                           preferred_element_type=F32)


def _dot_tn(a, b):
    return lax.dot_general(a.astype(BF16), b.astype(BF16), (((0,), (0,)), ((), ())),
                           preferred_element_type=F32)


def _split(a):
    hi = a.astype(BF16)
    lo = (a - hi.astype(F32)).astype(BF16)
    return hi, lo


def _dot3(a, b):
    ah, al = _split(a)
    bh, bl = _split(b)
    d = functools.partial(jnp.dot, preferred_element_type=F32)
    return d(ah, bh) + (d(al, bh) + d(ah, bl))


def _sigmoid(x):
    return 1.0 / (1.0 + jnp.exp(-x))


def _silu(x):
    return x * _sigmoid(x)


def _softplus(x):
    return jnp.maximum(x, 0.0) + jnp.log1p(jnp.exp(-jnp.abs(x)))


def _logsigmoid(x):
    return -_softplus(-x)


def _modnorm(x, w, shift, scale):
    y = x * lax.rsqrt(jnp.mean(x * x, axis=-1, keepdims=True) + EPS)
    return (y * w) * (1.0 + scale) + shift


def _params(sem, vmem=VMEM_LIMIT):
    return pltpu.CompilerParams(dimension_semantics=sem, vmem_limit_bytes=vmem)


def _resident(shape, index_map):
    return pl.BlockSpec(shape, index_map, pipeline_mode=pl.Buffered(1))


def _ada_kernel(c_ref, w_ref, b_ref, o_ref):
    o_ref[...] = _dot3(_silu(c_ref[...]), w_ref[...]) + b_ref[...]


def _ada_call(cc, ada_w, ada_b):
    depth, d, n = ada_w.shape
    tn = 1536
    return pl.pallas_call(
        _ada_kernel,
        grid=(depth, n // tn),
        in_specs=[pl.BlockSpec(cc.shape, lambda l, j: (0, 0)),
                  pl.BlockSpec((None, d, tn), lambda l, j: (l, 0, j)),
                  pl.BlockSpec((None, 1, tn), lambda l, j: (l, 0, j))],
        out_specs=pl.BlockSpec((None, cc.shape[0], tn), lambda l, j: (l, 0, j)),
        out_shape=jax.ShapeDtypeStruct((depth, cc.shape[0], n), F32),
        compiler_params=_params(("arbitrary", "arbitrary")),
        name="ada",
    )(cc, ada_w, ada_b.reshape(depth, 1, n))


def _proj_kernel(h_ref, mod_ref, nw_ref, w_ref, p_ref):
    mod = mod_ref[...]
    xn = _modnorm(h_ref[...], nw_ref[...], mod[0:1], mod[1:2]).astype(BF16)
    for j in range(P_COLS // PROJ_TN):
        sl = slice(j * PROJ_TN, (j + 1) * PROJ_TN)
        p_ref[:, sl] = jnp.dot(xn, w_ref[:, sl], preferred_element_type=F32)


def _proj_call(h, mod, nw, w, tm, nct):
    b, t, d = h.shape
    return pl.pallas_call(
        _proj_kernel,
        grid=(b, t // tm),
        in_specs=[pl.BlockSpec((None, tm, d), lambda i, j: (i, j, 0)),
                  pl.BlockSpec((None, None, 6, d), lambda i, j: (i, jnp.minimum(j // nct, 1), 0, 0)),
                  pl.BlockSpec((1, d), lambda i, j: (0, 0)),
                  _resident((d, P_COLS), lambda i, j: (0, 0))],
        out_specs=pl.BlockSpec((None, tm, P_COLS), lambda i, j: (i, j, 0)),
        out_shape=jax.ShapeDtypeStruct((b, t, P_COLS), F32),
        compiler_params=_params(("arbitrary", "arbitrary")),
        name="proj_in",
    )(h, mod, nw, w)


def _chunk_masks():
    row = lax.broadcasted_iota(jnp.int32, (CHUNK, CHUNK), 0)
    col = lax.broadcasted_iota(jnp.int32, (CHUNK, CHUNK), 1)
    return col <= row, col >= row, col < row, col > row


def _bwd_chunk(it, nc, ncc):
    return jnp.where(it < ncc, ncc - 1 - it, nc - 1 - (it - ncc))


def _rows3(x, n_rows=16):
    hi = x.astype(BF16).astype(F32)
    mid = (x - hi).astype(BF16).astype(F32)
    lo = ((x - hi) - mid).astype(BF16).astype(F32)
    r = lax.broadcasted_iota(jnp.int32, (n_rows, x.shape[1]), 0)
    return jnp.where(r == 0, hi, jnp.where(r == 1, mid, jnp.where(r == 2, lo, 0.0))).astype(BF16)


def _mlstm_kernel(qkt_ref, v_ref, o_ref, grow_ref, bcol_ref, nw_ref, y_ref, hf_ref, hb_ref, *, nc, ncc):
    L = CHUNK
    le, ge, _, _ = _chunk_masks()
    vis = (ge, le)
    row_l = lax.broadcasted_iota(jnp.int32, (L, LANES), 0)
    col_l = lax.broadcasted_iota(jnp.int32, (L, LANES), 1)
    ones_l = jnp.ones((L, LANES), BF16)
    zeros_l = jnp.zeros((L, LANES), BF16)
    stat_rhs = [jnp.concatenate(
        [ones_l, jnp.where((col_l < L) & ((col_l >= row_l) if d == 0 else (col_l <= row_l)), 1.0, 0.0).astype(BF16)],
        axis=1) for d in range(2)]
    ones3 = jnp.where(lax.broadcasted_iota(jnp.int32, (16, LANES), 0) < 3, 1.0, 0.0).astype(BF16)
    tail_rhs = jnp.concatenate([jnp.zeros((16, 2 * LANES), BF16), ones3], axis=1)
    bias_col = bcol_ref[...]
    grp = _group_size(nc)
    h_refs = (hf_ref, hb_ref)

    def sum3(x):
        return x[0:1] + x[1:2] + x[2:3]

    def body(gi, carry):
        C = [carry[0], carry[2]]
        m = [carry[1], carry[3]]
        chains = []
        for j in range(grp):
            it = gi * grp + j
            for d, c in ((0, it), (1, _bwd_chunk(it, nc, ncc))):
                gr = grow_ref[c] + bias_col
                x = qkt_ref[c]
                chains.append(dict(d=d, r0=pl.multiple_of(c * L, L), i_row=gr[2 * d:2 * d + 1, :],
                                   f3=_rows3(_logsigmoid(gr[2 * d + 1:2 * d + 2, :])),
                                   qt=x[:A_DQK] * (A_DQK ** -0.5), kt=x[A_DQK:]))
        for ch in chains:
            ch["v"] = v_ref[pl.ds(ch["r0"], L), :].astype(BF16)
            ch["sqt"] = _dot_tn(ch["kt"], ch["qt"])
        for ch in chains:
            st = jnp.dot(ch["f3"], stat_rhs[ch["d"]], preferred_element_type=F32)
            ch["btot"] = sum3(st[:, :LANES])
            ch["bcum"] = sum3(st[:, LANES:LANES + L])
            ch["a_row"] = ch["i_row"] - ch["bcum"]
        for ch in chains:
            ch["a_col"] = lax.dot_general(_rows3(ch["a_row"]), ones3, (((0,), (0,)), ((), ())),
                                          preferred_element_type=F32)
        for ch in chains:
            d = ch["d"]
            m_new = jnp.maximum(ch["btot"] + m[d], ch["btot"] + jnp.max(ch["a_col"], axis=0, keepdims=True))
            dlog = jnp.where(vis[d], ch["bcum"] + ch["a_col"][:, :L], NEG)
            inter = ch["bcum"] + m[d][:, :L]
            mt = jnp.maximum(inter, jnp.max(dlog, axis=0, keepdims=True))
            ch["st"] = (ch["sqt"] * jnp.exp(dlog - mt)).astype(BF16)
            ch["qe"] = (ch["qt"] * jnp.exp(inter - mt)).astype(BF16)
            ch["e3"] = _rows3(jnp.exp(-mt))
            ch["kw"] = (ch["kt"] * jnp.exp(ch["btot"][:, :L] + ch["a_row"] - m_new[:, :L])).astype(BF16)
            decay = jnp.exp(ch["btot"] + m[d] - m_new)
            ch["decay"] = jnp.concatenate([decay, decay], axis=1)
            m[d] = m_new
        for ch in chains:
            ch["kv"] = jnp.dot(ch["kw"], jnp.concatenate([ch["v"], ones_l], axis=1), preferred_element_type=F32)
            lhs = jnp.concatenate([ch["st"], ch["e3"]], axis=0)
            rhs = jnp.concatenate([jnp.concatenate([ch["v"], ones_l, zeros_l], axis=1), tail_rhs], axis=0)
            ch["intra"] = lax.dot_general(lhs, rhs, (((0,), (0,)), ((), ())), preferred_element_type=F32)
        for ch in chains:
            d = ch["d"]
            ch["inter"] = lax.dot_general(ch["qe"], C[d].astype(BF16), (((0,), (0,)), ((), ())),
                                          preferred_element_type=F32)
            C[d] = ch["decay"] * C[d] + ch["kv"]
        for ch in chains:
            num = ch["intra"][:, :A_DV] + ch["inter"][:, :A_DV]
            den = ch["intra"][:, A_DV:2 * A_DV] + ch["inter"][:, A_DV:]
            h_refs[ch["d"]][pl.ds(ch["r0"], L), :] = num / jnp.maximum(jnp.abs(den), ch["intra"][:, 2 * A_DV:])
        return C[0], m[0], C[1], m[1]

    c0 = jnp.zeros((A_DQK, 2 * LANES), F32)
    m0 = jnp.full((1, LANES), M_INIT, F32)
    lax.fori_loop(0, nc // grp, body, (c0, m0, c0, m0))

    hsum = hf_ref[...] + hb_ref[...]
    hn = hsum * lax.rsqrt(jnp.mean(hsum * hsum, axis=-1, keepdims=True) + EPS)
    y_ref[...] = hn * nw_ref[...] * _sigmoid(o_ref[...])


def _mlstm_call(p, qkt, grow, bcol, nw, ncc):
    b, t, _ = p.shape
    nc = t // CHUNK
    blk = lambda base: pl.BlockSpec((None, t, LANES), lambda i, h: (i, 0, base // LANES + h))
    return pl.pallas_call(
        functools.partial(_mlstm_kernel, nc=nc, ncc=ncc),
        grid=(b, A_HEADS),
        in_specs=[pl.BlockSpec((None, None, nc, 2 * A_DQK, CHUNK), lambda i, h: (i, h, 0, 0, 0)),
                  blk(P_AV), blk(P_AO),
                  pl.BlockSpec((None, None, nc, 4, CHUNK), lambda i, h: (i, h, 0, 0, 0)),
                  pl.BlockSpec((None, 4, 1), lambda i, h: (h, 0, 0)),
                  pl.BlockSpec((1, LANES), lambda i, h: (0, h))],
        out_specs=pl.BlockSpec((None, t, LANES), lambda i, h: (i, 0, h)),
        out_shape=jax.ShapeDtypeStruct((b, t, BRANCH_W), F32),
        scratch_shapes=[pltpu.VMEM((t, LANES), F32), pltpu.VMEM((t, LANES), F32)],
        compiler_params=_params(("arbitrary", "arbitrary")),
        name="mlstm",
    )(qkt, p, p, grow, bcol, nw)


def _headnorm_rope(x, bd, w, cos, sin_signed, lane_half):
    hi, lo = _split(x * x)
    ssum = jnp.dot(hi, bd, preferred_element_type=F32) + jnp.dot(lo, bd, preferred_element_type=F32)
    y = x * lax.rsqrt(ssum * (1.0 / B_DH) + EPS) * w
    swapped = jnp.where(lane_half, pltpu.roll(y, LANES - 16, axis=1), pltpu.roll(y, 16, axis=1))
    return y * cos + swapped * sin_signed


def _gqa_kernel(q_ref, k_ref, v_ref, qw_ref, kw_ref, cosq_ref, sinq_ref, cosk_ref, sink_ref, y_ref,
                kn_ref, vb_ref, *, tc, nqc):
    tq = q_ref.shape[0]
    t = k_ref.shape[0]
    qb = pl.program_id(1)
    r = lax.broadcasted_iota(jnp.int32, (LANES, LANES), 0)
    c = lax.broadcasted_iota(jnp.int32, (LANES, LANES), 1)
    bd = jnp.where(r // B_DH == c // B_DH, 1.0, 0.0).astype(BF16)

    def lane_half(n):
        return (lax.broadcasted_iota(jnp.int32, (n, LANES), 1) % 32) < 16

    @pl.when(qb == 0)
    def _():
        kn = _headnorm_rope(k_ref[...], bd, kw_ref[...], cosk_ref[...], sink_ref[...], lane_half(t))
        kn_ref[...] = kn.astype(BF16)
        v = v_ref[...]
        first = lax.broadcasted_iota(jnp.int32, (t, LANES), 1) < B_DH
        vb_ref[0] = jnp.where(first, v, 1.0).astype(BF16)
        vb_ref[1] = jnp.where(first, pltpu.roll(v, B_DH, axis=1), 1.0).astype(BF16)

    qn = []
    for j in range(B_HEADS * B_DH // LANES):
        sl = slice(j * LANES, (j + 1) * LANES)
        y = _headnorm_rope(q_ref[:, sl], bd, qw_ref[...], cosq_ref[...], sinq_ref[...], lane_half(tq))
        qn.append((y * (B_DH ** -0.5 * LOG2_E)).astype(BF16))

    group = B_HEADS // B_KV_HEADS

    def attend(nk):
        scores = []
        for kvh in range(B_KV_HEADS):
            kh = kn_ref[0:nk, kvh * B_DH:(kvh + 1) * B_DH]
            qs = jnp.concatenate(
                [qn[h // 2][:, (h % 2) * B_DH:(h % 2 + 1) * B_DH]
                 for h in range(kvh * group, (kvh + 1) * group)], axis=0)
            scores.append(lax.dot_general(qs, kh, (((1,), (1,)), ((), ())), preferred_element_type=F32))
        for kvh, s in enumerate(scores):
            p = jnp.exp2(s - jnp.max(s, axis=-1, keepdims=True))
            oe = jnp.dot(p.astype(BF16), vb_ref[kvh, 0:nk, :], preferred_element_type=F32)
            o = oe * pltpu.roll(1.0 / oe, B_DH, axis=1)
            for g in range(group):
                h = kvh * group + g
                y_ref[:, h * B_DH:(h + 1) * B_DH] = o[g * tq:(g + 1) * tq, :B_DH]

    @pl.when(qb < nqc)
    def _():
        attend(tc)

    @pl.when(qb >= nqc)
    def _():
        attend(t)


def _gqa_call(p, qw, kw, cos2, sin2, tc):
    b, t, _ = p.shape
    tq = Q_TILE
    row_blk = lambda: pl.BlockSpec((tq, LANES), lambda i, j: (j, 0))
    full = lambda: pl.BlockSpec((t, LANES), lambda i, j: (0, 0))
    return pl.pallas_call(
        functools.partial(_gqa_kernel, tc=tc, nqc=tc // tq),
        grid=(b, t // tq),
        in_specs=[pl.BlockSpec((None, tq, B_HEADS * B_DH), lambda i, j: (i, j, P_BQ // (B_HEADS * B_DH))),
                  pl.BlockSpec((None, t, LANES), lambda i, j: (i, 0, P_BK // LANES)),
                  pl.BlockSpec((None, t, LANES), lambda i, j: (i, 0, P_BV // LANES)),
                  pl.BlockSpec((1, LANES), lambda i, j: (0, 0)),
                  pl.BlockSpec((1, LANES), lambda i, j: (0, 0)),
                  row_blk(), row_blk(), full(), full()],
        out_specs=pl.BlockSpec((None, tq, BRANCH_W), lambda i, j: (i, j, 0)),
        out_shape=jax.ShapeDtypeStruct((b, t, BRANCH_W), F32),
        scratch_shapes=[pltpu.VMEM((t, LANES), BF16), pltpu.VMEM((B_KV_HEADS, t, LANES), BF16)],
        compiler_params=_params(("arbitrary", "arbitrary")),
        name="gqa",
    )(p, p, p, qw, kw, cos2, sin2, cos2, sin2)


def _group_size(nc):
    return max(g for g in range(1, SCAN_GROUP + 1) if nc % g == 0)


def _gdn_kernel(q_ref, k_ref, v_ref, z_ref, cwq_ref, cwk_ref, cwv_ref, gcol_ref, grow_ref, par_ref,
                nw_ref, y_ref, qn_ref, kn_ref, vn_ref, n_ref, kq_ref, egl_ref, o_ref, *, nc, ncc):
    L = CHUNK
    t = q_ref.shape[0]
    tc = ncc * L
    le, ge, lt, gt = _chunk_masks()
    par = par_ref[...]

    rows = lax.broadcasted_iota(jnp.int32, (t, 1), 0)
    has_prev = jnp.logical_and(rows != 0, rows != tc)
    has_next = jnp.logical_and(rows != tc - 1, rows != t - 1)

    def conv_silu(x_ref, cw_ref):
        x = x_ref[...]
        cw = cw_ref[...]
        prev = jnp.where(has_prev, pltpu.roll(x, 1, axis=0), 0.0)
        nxt = jnp.where(has_next, pltpu.roll(x, t - 1, axis=0), 0.0)
        return _silu(prev * cw[0:1] + x * cw[1:2] + nxt * cw[2:3])

    def l2n(x):
        return x * lax.rsqrt(jnp.sum(x * x, axis=-1, keepdims=True) + EPS)

    qn_ref[...] = l2n(conv_silu(q_ref, cwq_ref)) * (C_DK ** -0.5)
    kn_ref[...] = l2n(conv_silu(k_ref, cwk_ref))
    vn_ref[...] = conv_silu(v_ref, cwv_ref)

    grp = _group_size(nc)
    eye = jnp.where(le & ge, 1.0, 0.0).astype(F32)

    def prep(gi, carry):
        chunks = [gi * grp + j for j in range(grp)]
        rows0 = [pl.multiple_of(c * L, L) for c in chunks]
        qs = [qn_ref[pl.ds(r0, L), :] for r0 in rows0]
        ks = [kn_ref[pl.ds(r0, L), :] for r0 in rows0]
        vs = [vn_ref[pl.ds(r0, L), :] for r0 in rows0]
        kkqk = [_dot_nt(jnp.concatenate([k, q], axis=0), k) for k, q in zip(ks, qs)]
        chains = []
        for j, c in enumerate(chunks):
            gc = gcol_ref[pl.ds(rows0[j], L), :]
            gr = grow_ref[c]
            for d in range(2):
                mask_in, mask_t, strict = (le, ge, lt) if d == 0 else (ge, le, gt)
                neg_rate = -jnp.exp(par[:, d:d + 1])
                g_col = neg_rate * _softplus(gc[:, d:d + 1] + par[:, 2 + d:3 + d])
                g_row = neg_rate * _softplus(gr[d:d + 1, :] + par[:, 2 + d:3 + d])
                beta = _sigmoid(gc[:, 2 + d:3 + d])
                G_col = jnp.sum(jnp.where(mask_in, g_row, 0.0), axis=1, keepdims=True)
                G_row = jnp.sum(jnp.where(mask_t, g_col, 0.0), axis=0, keepdims=True)
                dec = jnp.exp(jnp.where(mask_in, G_col - G_row, NEG))
                g_last = jnp.sum(g_row, axis=1, keepdims=True)
                chains.append(dict(j=j, c=c, d=d, beta=beta, dec=dec, eG=jnp.exp(G_col),
                                   kscale=jnp.exp(g_last - G_col), egl=jnp.exp(g_last),
                                   x=-jnp.where(strict, beta * kkqk[j][:L] * dec, 0.0)))
        invs = [eye + ch["x"] for ch in chains]
        pws = [ch["x"] for ch in chains]
        n = 1
        while 2 * n < L:
            pws = [_dot(pw, pw) for pw in pws]
            invs = [inv + _dot(inv, pw) for inv, pw in zip(invs, pws)]
            n *= 2
        uws = []
        for ch, inv in zip(chains, invs):
            k, v = ks[ch["j"]], vs[ch["j"]]
            rhs = jnp.concatenate([v * ch["beta"], k * (ch["beta"] * ch["eG"])], axis=1)
            uws.append(_dot(inv, rhs))
        kns = [_dot_tn(ks[ch["j"]] * ch["kscale"], uw) for ch, uw in zip(chains, uws)]
        qos = [_dot(kkqk[ch["j"]][L:] * ch["dec"], uw) for ch, uw in zip(chains, uws)]
        for ch, kn, qo in zip(chains, kns, qos):
            d, c, r0 = ch["d"], ch["c"], rows0[ch["j"]]
            n_ref[d, c] = kn[:, :C_DV]
            kq_ref[d, c, 0:C_DK, :] = kn[:, C_DV:].astype(BF16)
            kq_ref[d, c, C_DK:C_DK + L, :] = (qs[ch["j"]] * ch["eG"] - qo[:, C_DV:]).astype(BF16)
            o_ref[d, pl.ds(r0, L), :] = qo[:, :C_DV]
            egl_ref[d, c] = jnp.broadcast_to(ch["egl"], (8, LANES))
        return carry

    lax.fori_loop(0, nc // grp, prep, 0)

    def step(c, d, S):
        r0 = pl.multiple_of(c * L, L)
        ks_qs = jnp.dot(kq_ref[d, c], S.astype(BF16), preferred_element_type=F32)
        o_ref[d, pl.ds(r0, L), :] += ks_qs[C_DK:]
        return S * egl_ref[d, c][0:1, 0:1] + (n_ref[d, c] - ks_qs[:C_DK])

    def body(it, carry):
        Sf, Sb = carry
        return step(it, 0, Sf), step(_bwd_chunk(it, nc, ncc), 1, Sb)

    s0 = jnp.zeros((C_DK, C_DV), F32)
    lax.fori_loop(0, nc, body, (s0, s0))

    osum = o_ref[0] + o_ref[1]
    on = osum * lax.rsqrt(jnp.mean(osum * osum, axis=-1, keepdims=True) + EPS) * nw_ref[...]
    y_ref[...] = on * _silu(z_ref[...])


def _gdn_call(p, cw, gcol, grow, par, nw, ncc):
    b, t, _ = p.shape
    nc = t // CHUNK
    blk = lambda base: pl.BlockSpec((None, t, LANES), lambda i, h: (i, 0, base // LANES + h))
    cwb = lambda off: pl.BlockSpec((3, LANES), lambda i, h: (0, off + h))
    big = lambda dt: pltpu.VMEM((2, t, LANES), dt)
    return pl.pallas_call(
        functools.partial(_gdn_kernel, nc=nc, ncc=ncc),
        grid=(b, C_HEADS),
        in_specs=[blk(P_CQ), blk(P_CK), blk(P_CV), blk(P_CZ), cwb(0), cwb(C_HEADS), cwb(2 * C_HEADS),
                  pl.BlockSpec((None, None, t, 4), lambda i, h: (i, h, 0, 0)),
                  pl.BlockSpec((None, None, nc, 4, CHUNK), lambda i, h: (i, h, 0, 0, 0)),
                  pl.BlockSpec((None, 1, 4), lambda i, h: (h, 0, 0)),
                  pl.BlockSpec((1, LANES), lambda i, h: (0, 0))],
        out_specs=pl.BlockSpec((None, t, LANES), lambda i, h: (i, 0, h)),
        out_shape=jax.ShapeDtypeStruct((b, t, BRANCH_W), F32),
        scratch_shapes=[pltpu.VMEM((t, LANES), F32), pltpu.VMEM((t, LANES), F32),
                        pltpu.VMEM((t, LANES), F32),
                        pltpu.VMEM((2, nc, C_DK, C_DV), F32),
                        pltpu.VMEM((2, nc, C_DK + CHUNK, C_DV), BF16),
                        pltpu.VMEM((2, nc, 8, LANES), F32),
                        big(F32)],
        compiler_params=_params(("arbitrary", "arbitrary")),
        name="gdn",
    )(p, p, p, p, cw, cw, cw, gcol, grow, par, nw)


def _merge_kernel(ya_ref, yb_ref, yc_ref, ga_ref, gb_ref, gc_ref, h_ref, mod_ref, wb_ref, wo_ref, o_ref):
    y = (_sigmoid(ga_ref[...]) * _dot(ya_ref[...], wb_ref[0])
         + _sigmoid(gb_ref[...]) * _dot(yb_ref[...], wb_ref[1])
         + _sigmoid(gc_ref[...]) * _dot(yc_ref[...], wb_ref[2]))
    o_ref[...] = h_ref[...] + mod_ref[2:3, :] * _dot(y, wo_ref[...])


def _mod_row(j, nct):
    return 1 if nct == 0 else jnp.minimum(j // nct, 1)


def _merge_call(ya, yb, yc, p, h, mod, wb, wo, tm, nct, latent_only):
    b, t, d = h.shape
    skip = nct if latent_only else 0
    yblk = lambda: pl.BlockSpec((None, tm, BRANCH_W), lambda i, j: (i, j + skip, 0))
    gblk = lambda g: pl.BlockSpec((None, tm, d), lambda i, j: (i, j + skip, g))
    return pl.pallas_call(
        _merge_kernel,
        grid=(b, t // tm - skip),
        in_specs=[yblk(), yblk(), yblk(), gblk(0), gblk(1), gblk(2),
                  pl.BlockSpec((None, tm, d), lambda i, j: (i, j + skip, 0)),
                  pl.BlockSpec((None, None, 6, d), lambda i, j: (i, _mod_row(j + skip, nct), 0, 0)),
                  _resident((3, BRANCH_W, d), lambda i, j: (0, 0, 0)),
                  _resident((d, d), lambda i, j: (0, 0))],
        out_specs=pl.BlockSpec((None, tm, d), lambda i, j: (i, j, 0)),
        out_shape=jax.ShapeDtypeStruct((b, t - skip * tm, d), F32),
        compiler_params=_params(("arbitrary", "arbitrary")),
        name="merge",
    )(ya, yb, yc, p, p, p, h, mod, wb, wo)


def _ffn_kernel(h_ref, hp_ref, hn_ref, mod_ref, nw_ref, wup_ref, cw_ref, wdn_ref, o_ref, acc_ref,
                *, nct, ntiles):
    tm = h_ref.shape[0]
    j = pl.program_id(1)
    mod = mod_ref[...]
    nw = nw_ref[...]
    h = h_ref[...]
    xn = _modnorm(h, nw, mod[3:4], mod[4:5]).astype(BF16)
    halo = jnp.concatenate([hp_ref[...], hn_ref[...]], axis=0)
    xh = _modnorm(halo, nw, mod[3:4], mod[4:5]).astype(BF16)
    has_prev = jnp.logical_and(j != 0, j != nct).astype(F32)
    has_next = jnp.logical_and(j != nct - 1, j != ntiles - 1).astype(F32)
    rows = lax.broadcasted_iota(jnp.int32, (tm, 1), 0)
    first = rows == 0
    last = rows == tm - 1

    def up_conv(kind, jf):
        u = jnp.dot(xn, wup_ref[kind, jf], preferred_element_type=F32)
        uh = jnp.dot(xh, wup_ref[kind, jf], preferred_element_type=F32)
        cw = cw_ref[kind, jf]
        prev = jnp.where(first, uh[7:8] * has_prev, pltpu.roll(u, 1, axis=0))
        nxt = jnp.where(last, uh[8:9] * has_next, pltpu.roll(u, tm - 1, axis=0))
        return prev * cw[0:1] + u * cw[1:2] + nxt * cw[2:3]

    for jf in range(wup_ref.shape[1]):
        act = (up_conv(0, jf) * _silu(up_conv(1, jf))).astype(BF16)
        part = jnp.dot(act, wdn_ref[jf], preferred_element_type=F32)
        if jf == 0:
            acc_ref[...] = part
        else:
            acc_ref[...] += part
    o_ref[...] = h + mod[5:6] * acc_ref[...]


def _ffn_call(h, mod, nw, wup, cw, wdn, tm, nct):
    b, t, d = h.shape
    ntiles = t // tm
    hb = tm // 8
    return pl.pallas_call(
        functools.partial(_ffn_kernel, nct=nct, ntiles=ntiles),
        grid=(b, ntiles),
        in_specs=[pl.BlockSpec((None, tm, d), lambda i, j: (i, j, 0)),
                  pl.BlockSpec((None, 8, d), lambda i, j: (i, jnp.maximum(j * hb - 1, 0), 0)),
                  pl.BlockSpec((None, 8, d), lambda i, j: (i, jnp.minimum((j + 1) * hb, t // 8 - 1), 0)),
                  pl.BlockSpec((None, None, 6, d), lambda i, j: (i, _mod_row(j, nct), 0, 0)),
                  pl.BlockSpec((1, d), lambda i, j: (0, 0)),
                  _resident(wup.shape, lambda i, j: (0, 0, 0, 0)),
                  _resident(cw.shape, lambda i, j: (0, 0, 0, 0)),
                  _resident(wdn.shape, lambda i, j: (0, 0, 0))],
        out_specs=pl.BlockSpec((None, tm, d), lambda i, j: (i, j, 0)),
        out_shape=jax.ShapeDtypeStruct((b, t, d), F32),
        scratch_shapes=[pltpu.VMEM((tm, d), F32)],
        compiler_params=_params(("arbitrary", "arbitrary")),
        name="ffn",
    )(h, h, h, mod, nw, wup, cw, wdn)


def _rope_tables(tc, tl):
    rows = tl // GRID_W
    row = jnp.repeat(jnp.arange(rows, dtype=F32), GRID_W)
    col = jnp.tile(jnp.arange(GRID_W, dtype=F32), rows)
    n_freq = B_DH // 4
    inv = ROPE_BASE ** (-jnp.arange(n_freq, dtype=F32) / n_freq)
    ang_r = row[:, None] * inv
    ang_c = col[:, None] * inv
    cos = jnp.concatenate([jnp.cos(ang_r)] * 2 + [jnp.cos(ang_c)] * 2, axis=1)
    sin = jnp.concatenate([-jnp.sin(ang_r), jnp.sin(ang_r), -jnp.sin(ang_c), jnp.sin(ang_c)], axis=1)
    cos = jnp.concatenate([jnp.ones((tc, B_DH), F32), cos], axis=0)
    sin = jnp.concatenate([jnp.zeros((tc, B_DH), F32), sin], axis=0)
    return jnp.tile(cos, (1, 2)), jnp.tile(sin, (1, 2))


def _pack_w_in(w_in):
    depth, d, _ = w_in.shape
    o = 0
    parts = {}
    for name, width in (("aq", 256), ("ak", 256), ("av", 512), ("ao", 512), ("ag", 16),
                        ("bq", 512), ("bk", 128), ("bv", 128),
                        ("cq", 512), ("ck", 512), ("cv", 512), ("cz", 512), ("ca", 8), ("cb", 8),
                        ("gate", 3 * D_MODEL)):
        parts[name] = w_in[:, :, o:o + width]
        o += width
    aqk = jnp.concatenate([parts["aq"].reshape(depth, d, A_HEADS, A_DQK),
                           parts["ak"].reshape(depth, d, A_HEADS, A_DQK)], axis=3).reshape(depth, d, 512)
    small = jnp.concatenate([parts["ag"], parts["ca"], parts["cb"]], axis=2)
    pad = jnp.zeros((depth, d, P_COLS - P_SMALL - small.shape[2]), w_in.dtype)
    cols = [parts["gate"], aqk, parts["av"], parts["ao"], parts["bq"], parts["bk"], parts["bv"],
            parts["cq"], parts["ck"], parts["cv"], parts["cz"], small, pad]
    return jnp.concatenate(cols, axis=2).astype(BF16)


def _gate_layouts(x, heads, kinds):
    b, t, _ = x.shape
    g = x.reshape(b, t, kinds, heads)
    col = g.transpose(0, 3, 1, 2)
    row = g.reshape(b, t // CHUNK, CHUNK, kinds, heads).transpose(0, 4, 1, 3, 2)
    return col, row


def kernel(x, c, ctx, c_ctx, norm1_w, norm2_w, ada_w, ada_b, w_in, a_gate_b, a_norm_w, b_qnorm_w,
           b_knorm_w, c_conv_w, c_a_log, c_dt_bias, c_norm_w, w_branch, w_out, w_up, ffn_conv_w, w_down):
    b, tl, d = x.shape
    tc = ctx.shape[1]
    depth = w_in.shape[0]
    t = tc + tl
    tm = 256 if (tc % 256 == 0 and tl % 256 == 0) else 128
    nct = tc // tm
    ncc = tc // CHUNK
    nj = D_FF // FFN_TF

    cc = jnp.zeros((16, d), F32).at[:b].set(c).at[b].set(c_ctx)
    mods = _ada_call(cc, ada_w, ada_b).reshape(depth, 16, 6, d)
    mod_all = jnp.stack([jnp.broadcast_to(mods[:, b][:, None], (depth, b, 6, d)), mods[:, :b]], axis=2)

    w_in_p = _pack_w_in(w_in)
    wb = w_branch.astype(BF16)
    wo = w_out.astype(BF16)
    wup = w_up.astype(BF16).reshape(depth, d, 2, nj, FFN_TF).transpose(0, 2, 3, 1, 4)
    fcw = ffn_conv_w.reshape(depth, 3, 2, nj, FFN_TF).transpose(0, 2, 3, 1, 4)
    wdn = w_down.astype(BF16).reshape(depth, nj, FFN_TF, d)
    cos2, sin2 = _rope_tables(tc, tl)
    a_bcol = a_gate_b.transpose(0, 2, 1)[:, :, :, None]
    c_par = jnp.concatenate([c_a_log, c_dt_bias], axis=1).transpose(0, 2, 1)[:, :, None, :]

    h = jnp.concatenate([ctx, x], axis=1)
    for l in range(depth):
        mod = mod_all[l]
        p = _proj_call(h, mod, norm1_w[l][None], w_in_p[l], tm, nct)
        small = p[:, :, P_SMALL:P_SMALL + 32]
        _, a_grow = _gate_layouts(small[:, :, :16], A_HEADS, 4)
        qkt = (p[:, :, P_AQK:P_AQK + 2 * A_DQK * A_HEADS]
               .reshape(b, t // CHUNK, CHUNK, A_HEADS, 2 * A_DQK).transpose(0, 3, 1, 4, 2))
        c_gcol, c_grow = _gate_layouts(small[:, :, 16:32], C_HEADS, 4)
        ya = _mlstm_call(p, qkt, a_grow, a_bcol[l], a_norm_w[l][None], ncc)
        yb = _gqa_call(p, jnp.tile(b_qnorm_w[l], 2)[None], jnp.tile(b_knorm_w[l], 2)[None], cos2, sin2, tc)
        yc = _gdn_call(p, c_conv_w[l], c_gcol, c_grow, c_par[l], c_norm_w[l][None], ncc)
        last = l == depth - 1
        h = _merge_call(ya, yb, yc, p, h, mod, wb[l], wo[l], tm, nct, latent_only=last)
        h = _ffn_call(h, mod, norm2_w[l][None], wup[l], fcw[l], wdn[l], tm, 0 if last else nct)
    return h
```

```python
import functools

import jax
import jax.numpy as jnp
from jax import lax
from jax.experimental import pallas as pl
from jax.experimental.pallas import tpu as pltpu

F32 = jnp.float32
BF16 = jnp.bfloat16

D_MODEL = 1024
GRID_W = 64
A_HEADS, A_DQK, A_DV = 4, 64, 128
B_HEADS, B_KV_HEADS, B_DH = 8, 2, 64
C_HEADS, C_DK, C_DV = 4, 128, 128
BRANCH_W = 512
D_FF = 2816
CHUNK = 64
ROPE_BASE = 10000.0
EPS = 1e-6
M_INIT = -1e30
LOG2_E = 1.4426950408889634
NEG = -1e30

LANES = 128
P_GATE = 0
P_AV = 3072
P_AO = 3584
P_BQ = 4096
P_BK = 4608
P_BV = 4736
P_CQ = 4864
P_CK = 5376
P_CV = 5888
P_CZ = 6400
P_COLS = 6912
W_AQK = 6912
W_SMALL = 7424
W_COLS = 7552
PROJ_TN = 512
FFN_TF = 256
Q_TILE = 128
SCAN_GROUP = 12
VMEM_LIMIT = 56 * 1024 * 1024


def _dot(a, b):
    return jnp.dot(a.astype(BF16), b.astype(BF16), preferred_element_type=F32)


def _dot_nt(a, b):
    return lax.dot_general(a.astype(BF16), b.astype(BF16), (((1,), (1,)), ((), ())),
                           preferred_element_type=F32)


def _dot_tn(a, b):
    return lax.dot_general(a.astype(BF16), b.astype(BF16), (((0,), (0,)), ((), ())),
                           preferred_element_type=F32)


def _split(a):
    hi = a.astype(BF16)
    lo = (a - hi.astype(F32)).astype(BF16)
    return hi, lo


def _dot3(a, b):
    ah, al = _split(a)
    bh, bl = _split(b)
    d = functools.partial(jnp.dot, preferred_element_type=F32)
    return d(ah, bh) + (d(al, bh) + d(ah, bl))


def _sigmoid(x):
    return 1.0 / (1.0 + jnp.exp(-x))


def _silu(x):
    return x * _sigmoid(x)


def _softplus(x):
    return jnp.maximum(x, 0.0) + jnp.log1p(jnp.exp(-jnp.abs(x)))


def _logsigmoid(x):
    return -_softplus(-x)


def _modnorm(x, w, shift, scale):
    y = x * lax.rsqrt(jnp.mean(x * x, axis=-1, keepdims=True) + EPS)
    return (y * w) * (1.0 + scale) + shift


def _params(sem, vmem=VMEM_LIMIT):
    return pltpu.CompilerParams(dimension_semantics=sem, vmem_limit_bytes=vmem)


def _resident(shape, index_map):
    return pl.BlockSpec(shape, index_map, pipeline_mode=pl.Buffered(1))


def _ada_kernel(c_ref, w_ref, b_ref, o_ref):
    o_ref[...] = _dot3(_silu(c_ref[...]), w_ref[...]) + b_ref[...]


def _ada_call(cc, ada_w, ada_b):
    depth, d, n = ada_w.shape
    tn = 1536
    return pl.pallas_call(
        _ada_kernel,
        grid=(depth, n // tn),
        in_specs=[pl.BlockSpec(cc.shape, lambda l, j: (0, 0)),
                  pl.BlockSpec((None, d, tn), lambda l, j: (l, 0, j)),
                  pl.BlockSpec((None, 1, tn), lambda l, j: (l, 0, j))],
        out_specs=pl.BlockSpec((None, cc.shape[0], tn), lambda l, j: (l, 0, j)),
        out_shape=jax.ShapeDtypeStruct((depth, cc.shape[0], n), F32),
        compiler_params=_params(("arbitrary", "arbitrary")),
        name="ada",
    )(cc, ada_w, ada_b.reshape(depth, 1, n))


def _proj_kernel(h_ref, mod_ref, nw_ref, w_ref, p_ref, qkt_ref, small_ref):
    tm = h_ref.shape[0]
    mod = mod_ref[...]
    xn = _modnorm(h_ref[...], nw_ref[...], mod[0:1], mod[1:2]).astype(BF16)
    for c0 in range(0, P_COLS, PROJ_TN):
        sl = slice(c0, min(c0 + PROJ_TN, P_COLS))
        p_ref[:, sl] = jnp.dot(xn, w_ref[:, sl], preferred_element_type=F32)
    qk = jnp.dot(xn, w_ref[:, W_AQK:W_SMALL], preferred_element_type=F32)
    for hd in range(A_HEADS):
        for c in range(tm // CHUNK):
            qkt_ref[hd, c] = qk[c * CHUNK:(c + 1) * CHUNK, hd * LANES:(hd + 1) * LANES].T
    small_ref[...] = jnp.dot(xn, w_ref[:, W_SMALL:W_COLS], preferred_element_type=F32)


def _proj_call(h, mod, nw, w, l, tm, nct):
    b, t, d = h.shape
    cpt = tm // CHUNK
    return pl.pallas_call(
        _proj_kernel,
        grid=(b, t // tm),
        in_specs=[pl.BlockSpec((None, tm, d), lambda i, j: (i, j, 0)),
                  pl.BlockSpec((None, None, None, 6, d), lambda i, j: (l, i, _mod_row(j, nct), 0, 0)),
                  pl.BlockSpec((None, 1, d), lambda i, j: (l, 0, 0)),
                  _resident((None, d, W_COLS), lambda i, j: (l, 0, 0))],
        out_specs=[pl.BlockSpec((None, tm, P_COLS), lambda i, j: (i, j, 0)),
                   pl.BlockSpec((None, A_HEADS, cpt, 2 * A_DQK, CHUNK), lambda i, j: (i, 0, j, 0, 0)),
                   pl.BlockSpec((None, tm, LANES), lambda i, j: (i, j, 0))],
        out_shape=[jax.ShapeDtypeStruct((b, t, P_COLS), F32),
                   jax.ShapeDtypeStruct((b, A_HEADS, t // CHUNK, 2 * A_DQK, CHUNK), F32),
                   jax.ShapeDtypeStruct((b, t, LANES), F32)],
        compiler_params=_params(("arbitrary", "arbitrary")),
        name="proj_in",
    )(h, mod, nw, w)


def _chunk_masks():
    row = lax.broadcasted_iota(jnp.int32, (CHUNK, CHUNK), 0)
    col = lax.broadcasted_iota(jnp.int32, (CHUNK, CHUNK), 1)
    return col <= row, col >= row, col < row, col > row


def _bwd_chunk(it, nc, ncc):
    return jnp.where(it < ncc, ncc - 1 - it, nc - 1 - (it - ncc))


def _rows3(x, n_rows=16):
    hi = x.astype(BF16).astype(F32)
    mid = (x - hi).astype(BF16).astype(F32)
    lo = ((x - hi) - mid).astype(BF16).astype(F32)
    r = lax.broadcasted_iota(jnp.int32, (n_rows, x.shape[1]), 0)
    return jnp.where(r == 0, hi, jnp.where(r == 1, mid, jnp.where(r == 2, lo, 0.0))).astype(BF16)


def _mlstm_kernel(qkt_ref, v_ref, o_ref, grow_ref, bcol_ref, nw_ref, y_ref, hf_ref, hb_ref, *, nc, ncc):
    L = CHUNK
    le, ge, _, _ = _chunk_masks()
    vis = (ge, le)
    row_l = lax.broadcasted_iota(jnp.int32, (L, LANES), 0)
    col_l = lax.broadcasted_iota(jnp.int32, (L, LANES), 1)
    ones_l = jnp.ones((L, LANES), BF16)
    zeros_l = jnp.zeros((L, LANES), BF16)
    stat_rhs = [jnp.concatenate(
        [ones_l, jnp.where((col_l < L) & ((col_l >= row_l) if d == 0 else (col_l <= row_l)), 1.0, 0.0).astype(BF16)],
        axis=1) for d in range(2)]
    ones3 = jnp.where(lax.broadcasted_iota(jnp.int32, (16, LANES), 0) < 3, 1.0, 0.0).astype(BF16)
    tail_rhs = jnp.concatenate([jnp.zeros((16, 2 * LANES), BF16), ones3], axis=1)
    bias_col = bcol_ref[...]
    grp = _group_size(nc)
    h_refs = (hf_ref, hb_ref)

    def sum3(x):
        return x[0:1] + x[1:2] + x[2:3]

    def body(gi, carry):
        C = [carry[0], carry[2]]
        m = [carry[1], carry[3]]
        chains = []
        for j in range(grp):
            it = gi * grp + j
            for d, c in ((0, it), (1, _bwd_chunk(it, nc, ncc))):
                gr = grow_ref[c] + bias_col
                x = qkt_ref[c]
                chains.append(dict(d=d, r0=pl.multiple_of(c * L, L), i_row=gr[2 * d:2 * d + 1, :],
                                   f3=_rows3(_logsigmoid(gr[2 * d + 1:2 * d + 2, :])),
                                   qt=x[:A_DQK] * (A_DQK ** -0.5), kt=x[A_DQK:]))
        for ch in chains:
            ch["v"] = v_ref[pl.ds(ch["r0"], L), :].astype(BF16)
            ch["sqt"] = _dot_tn(ch["kt"], ch["qt"])
        for ch in chains:
            st = jnp.dot(ch["f3"], stat_rhs[ch["d"]], preferred_element_type=F32)
            ch["btot"] = sum3(st[:, :LANES])
            ch["bcum"] = sum3(st[:, LANES:LANES + L])
            ch["a_row"] = ch["i_row"] - ch["bcum"]
        for ch in chains:
            ch["a_col"] = lax.dot_general(_rows3(ch["a_row"]), ones3, (((0,), (0,)), ((), ())),
                                          preferred_element_type=F32)
        for ch in chains:
            d = ch["d"]
            m_new = jnp.maximum(ch["btot"] + m[d], ch["btot"] + jnp.max(ch["a_col"], axis=0, keepdims=True))
            dlog = jnp.where(vis[d], ch["bcum"] + ch["a_col"][:, :L], NEG)
            inter = ch["bcum"] + m[d][:, :L]
            mt = jnp.maximum(inter, jnp.max(dlog, axis=0, keepdims=True))
            ch["st"] = (ch["sqt"] * jnp.exp(dlog - mt)).astype(BF16)
            ch["qe"] = (ch["qt"] * jnp.exp(inter - mt)).astype(BF16)
            ch["e3"] = _rows3(jnp.exp(-mt))
            ch["kw"] = (ch["kt"] * jnp.exp(ch["btot"][:, :L] + ch["a_row"] - m_new[:, :L])).astype(BF16)
            decay = jnp.exp(ch["btot"] + m[d] - m_new)
            ch["decay"] = jnp.concatenate([decay, decay], axis=1)
            m[d] = m_new
        for ch in chains:
            ch["kv"] = jnp.dot(ch["kw"], jnp.concatenate([ch["v"], ones_l], axis=1), preferred_element_type=F32)
            lhs = jnp.concatenate([ch["st"], ch["e3"]], axis=0)
            rhs = jnp.concatenate([jnp.concatenate([ch["v"], ones_l, zeros_l], axis=1), tail_rhs], axis=0)
            ch["intra"] = lax.dot_general(lhs, rhs, (((0,), (0,)), ((), ())), preferred_element_type=F32)
        for ch in chains:
            d = ch["d"]
            ch["inter"] = lax.dot_general(ch["qe"], C[d].astype(BF16), (((0,), (0,)), ((), ())),
                                          preferred_element_type=F32)
            C[d] = ch["decay"] * C[d] + ch["kv"]
        for ch in chains:
            num = ch["intra"][:, :A_DV] + ch["inter"][:, :A_DV]
            den = ch["intra"][:, A_DV:2 * A_DV] + ch["inter"][:, A_DV:]
            h_refs[ch["d"]][pl.ds(ch["r0"], L), :] = num / jnp.maximum(jnp.abs(den), ch["intra"][:, 2 * A_DV:])
        return C[0], m[0], C[1], m[1]

    c0 = jnp.zeros((A_DQK, 2 * LANES), F32)
    m0 = jnp.full((1, LANES), M_INIT, F32)
    lax.fori_loop(0, nc // grp, body, (c0, m0, c0, m0))

    hsum = hf_ref[...] + hb_ref[...]
    hn = hsum * lax.rsqrt(jnp.mean(hsum * hsum, axis=-1, keepdims=True) + EPS)
    y_ref[...] = hn * nw_ref[...] * _sigmoid(o_ref[...])


def _mlstm_call(p, qkt, grow, bcol, nw, l, ncc):
    b, t, _ = p.shape
    nc = t // CHUNK
    blk = lambda base: pl.BlockSpec((None, t, LANES), lambda i, h: (i, 0, base // LANES + h))
    return pl.pallas_call(
        functools.partial(_mlstm_kernel, nc=nc, ncc=ncc),
        grid=(b, A_HEADS),
        in_specs=[pl.BlockSpec((None, None, nc, 2 * A_DQK, CHUNK), lambda i, h: (i, h, 0, 0, 0)),
                  blk(P_AV), blk(P_AO),
                  pl.BlockSpec((None, None, nc, 4, CHUNK), lambda i, h: (i, h, 0, 0, 0)),
                  pl.BlockSpec((None, None, 4, 1), lambda i, h: (l, h, 0, 0)),
                  pl.BlockSpec((None, 1, LANES), lambda i, h: (l, 0, h))],
        out_specs=pl.BlockSpec((None, t, LANES), lambda i, h: (i, 0, h)),
        out_shape=jax.ShapeDtypeStruct((b, t, BRANCH_W), F32),
        scratch_shapes=[pltpu.VMEM((t, LANES), F32), pltpu.VMEM((t, LANES), F32)],
        compiler_params=_params(("arbitrary", "arbitrary")),
        name="mlstm",
    )(qkt, p, p, grow, bcol, nw)


def _headnorm_rope(x, bd, w, cos, sin_signed, lane_half):
    hi, lo = _split(x * x)
    ssum = jnp.dot(hi, bd, preferred_element_type=F32) + jnp.dot(lo, bd, preferred_element_type=F32)
    y = x * lax.rsqrt(ssum * (1.0 / B_DH) + EPS) * w
    swapped = jnp.where(lane_half, pltpu.roll(y, LANES - 16, axis=1), pltpu.roll(y, 16, axis=1))
    return y * cos + swapped * sin_signed


def _gqa_kernel(q_ref, k_ref, v_ref, qw_ref, kw_ref, cosq_ref, sinq_ref, cosk_ref, sink_ref, y_ref,
                kn_ref, vb_ref, *, tc, nqc):
    tq = q_ref.shape[0]
    t = k_ref.shape[0]
    qb = pl.program_id(1)
    r = lax.broadcasted_iota(jnp.int32, (LANES, LANES), 0)
    c = lax.broadcasted_iota(jnp.int32, (LANES, LANES), 1)
    bd = jnp.where(r // B_DH == c // B_DH, 1.0, 0.0).astype(BF16)

    def lane_half(n):
        return (lax.broadcasted_iota(jnp.int32, (n, LANES), 1) % 32) < 16

    @pl.when(qb == 0)
    def _():
        kn = _headnorm_rope(k_ref[...], bd, kw_ref[...], cosk_ref[...], sink_ref[...], lane_half(t))
        kn_ref[...] = kn.astype(BF16)
        v = v_ref[...]
        first = lax.broadcasted_iota(jnp.int32, (t, LANES), 1) < B_DH
        vb_ref[0] = jnp.where(first, v, 1.0).astype(BF16)
        vb_ref[1] = jnp.where(first, pltpu.roll(v, B_DH, axis=1), 1.0).astype(BF16)

    qn = []
    for j in range(B_HEADS * B_DH // LANES):
        sl = slice(j * LANES, (j + 1) * LANES)
        y = _headnorm_rope(q_ref[:, sl], bd, qw_ref[...], cosq_ref[...], sinq_ref[...], lane_half(tq))
        qn.append((y * (B_DH ** -0.5 * LOG2_E)).astype(BF16))

    group = B_HEADS // B_KV_HEADS

    def attend(nk):
        scores = []
        for kvh in range(B_KV_HEADS):
            kh = kn_ref[0:nk, kvh * B_DH:(kvh + 1) * B_DH]
            qs = jnp.concatenate(
                [qn[h // 2][:, (h % 2) * B_DH:(h % 2 + 1) * B_DH]
                 for h in range(kvh * group, (kvh + 1) * group)], axis=0)
            scores.append(lax.dot_general(qs, kh, (((1,), (1,)), ((), ())), preferred_element_type=F32))
        for kvh, s in enumerate(scores):
            p = jnp.exp2(s - jnp.max(s, axis=-1, keepdims=True))
            oe = jnp.dot(p.astype(BF16), vb_ref[kvh, 0:nk, :], preferred_element_type=F32)
            o = oe * pltpu.roll(1.0 / oe, B_DH, axis=1)
            for g in range(group):
                h = kvh * group + g
                y_ref[:, h * B_DH:(h + 1) * B_DH] = o[g * tq:(g + 1) * tq, :B_DH]

    @pl.when(qb < nqc)
    def _():
        attend(tc)

    @pl.when(qb >= nqc)
    def _():
        attend(t)


def _gqa_call(p, qw, kw, cos2, sin2, l, tc):
    b, t, _ = p.shape
    tq = Q_TILE
    row_blk = lambda: pl.BlockSpec((tq, LANES), lambda i, j: (j, 0))
    full = lambda: pl.BlockSpec((t, LANES), lambda i, j: (0, 0))
    return pl.pallas_call(
        functools.partial(_gqa_kernel, tc=tc, nqc=tc // tq),
        grid=(b, t // tq),
        in_specs=[pl.BlockSpec((None, tq, B_HEADS * B_DH), lambda i, j: (i, j, P_BQ // (B_HEADS * B_DH))),
                  pl.BlockSpec((None, t, LANES), lambda i, j: (i, 0, P_BK // LANES)),
                  pl.BlockSpec((None, t, LANES), lambda i, j: (i, 0, P_BV // LANES)),
                  pl.BlockSpec((None, 1, LANES), lambda i, j: (l, 0, 0)),
                  pl.BlockSpec((None, 1, LANES), lambda i, j: (l, 0, 0)),
                  row_blk(), row_blk(), full(), full()],
        out_specs=pl.BlockSpec((None, tq, BRANCH_W), lambda i, j: (i, j, 0)),
        out_shape=jax.ShapeDtypeStruct((b, t, BRANCH_W), F32),
        scratch_shapes=[pltpu.VMEM((t, LANES), BF16), pltpu.VMEM((B_KV_HEADS, t, LANES), BF16)],
        compiler_params=_params(("arbitrary", "arbitrary")),
        name="gqa",
    )(p, p, p, qw, kw, cos2, sin2, cos2, sin2)


def _group_size(nc):
    return max(g for g in range(1, SCAN_GROUP + 1) if nc % g == 0)


def _gdn_kernel(q_ref, k_ref, v_ref, z_ref, cwq_ref, cwk_ref, cwv_ref, gcol_ref, grow_ref, par_ref,
                nw_ref, y_ref, qn_ref, kn_ref, vn_ref, n_ref, kq_ref, egl_ref, o_ref, *, nc, ncc):
    L = CHUNK
    t = q_ref.shape[0]
    tc = ncc * L
    le, ge, lt, gt = _chunk_masks()
    par = par_ref[...]

    rows = lax.broadcasted_iota(jnp.int32, (t, 1), 0)
    has_prev = jnp.logical_and(rows != 0, rows != tc)
    has_next = jnp.logical_and(rows != tc - 1, rows != t - 1)

    def conv_silu(x_ref, cw_ref):
        x = x_ref[...]
        cw = cw_ref[...]
        prev = jnp.where(has_prev, pltpu.roll(x, 1, axis=0), 0.0)
        nxt = jnp.where(has_next, pltpu.roll(x, t - 1, axis=0), 0.0)
        return _silu(prev * cw[0:1] + x * cw[1:2] + nxt * cw[2:3])

    def l2n(x):
        return x * lax.rsqrt(jnp.sum(x * x, axis=-1, keepdims=True) + EPS)

    qn_ref[...] = l2n(conv_silu(q_ref, cwq_ref)) * (C_DK ** -0.5)
    kn_ref[...] = l2n(conv_silu(k_ref, cwk_ref))
    vn_ref[...] = conv_silu(v_ref, cwv_ref)

    grp = _group_size(nc)
    eye = jnp.where(le & ge, 1.0, 0.0).astype(F32)

    def prep(gi, carry):
        chunks = [gi * grp + j for j in range(grp)]
        rows0 = [pl.multiple_of(c * L, L) for c in chunks]
        qs = [qn_ref[pl.ds(r0, L), :] for r0 in rows0]
        ks = [kn_ref[pl.ds(r0, L), :] for r0 in rows0]
        vs = [vn_ref[pl.ds(r0, L), :] for r0 in rows0]
        kkqk = [_dot_nt(jnp.concatenate([k, q], axis=0), k) for k, q in zip(ks, qs)]
        chains = []
        for j, c in enumerate(chunks):
            gc = gcol_ref[pl.ds(rows0[j], L), :]
            gr = grow_ref[c]
            for d in range(2):
                mask_in, mask_t, strict = (le, ge, lt) if d == 0 else (ge, le, gt)
                neg_rate = -jnp.exp(par[:, d:d + 1])
                g_col = neg_rate * _softplus(gc[:, d:d + 1] + par[:, 2 + d:3 + d])
                g_row = neg_rate * _softplus(gr[d:d + 1, :] + par[:, 2 + d:3 + d])
                beta = _sigmoid(gc[:, 2 + d:3 + d])
                G_col = jnp.sum(jnp.where(mask_in, g_row, 0.0), axis=1, keepdims=True)
                G_row = jnp.sum(jnp.where(mask_t, g_col, 0.0), axis=0, keepdims=True)
                dec = jnp.exp(jnp.where(mask_in, G_col - G_row, NEG))
                g_last = jnp.sum(g_row, axis=1, keepdims=True)
                chains.append(dict(j=j, c=c, d=d, beta=beta, dec=dec, eG=jnp.exp(G_col),
                                   kscale=jnp.exp(g_last - G_col), egl=jnp.exp(g_last),
                                   x=-jnp.where(strict, beta * kkqk[j][:L] * dec, 0.0)))
        invs = [eye + ch["x"] for ch in chains]
        pws = [ch["x"] for ch in chains]
        n = 1
        while 2 * n < L:
            pws = [_dot(pw, pw) for pw in pws]
            invs = [inv + _dot(inv, pw) for inv, pw in zip(invs, pws)]
            n *= 2
        uws = []
        for ch, inv in zip(chains, invs):
            k, v = ks[ch["j"]], vs[ch["j"]]
            rhs = jnp.concatenate([v * ch["beta"], k * (ch["beta"] * ch["eG"])], axis=1)
            uws.append(_dot(inv, rhs))
        kns = [_dot_tn(ks[ch["j"]] * ch["kscale"], uw) for ch, uw in zip(chains, uws)]
        qos = [_dot(kkqk[ch["j"]][L:] * ch["dec"], uw) for ch, uw in zip(chains, uws)]
        for ch, kn, qo in zip(chains, kns, qos):
            d, c, r0 = ch["d"], ch["c"], rows0[ch["j"]]
            n_ref[d, c] = kn[:, :C_DV]
            kq_ref[d, c, 0:C_DK, :] = kn[:, C_DV:].astype(BF16)
            kq_ref[d, c, C_DK:C_DK + L, :] = (qs[ch["j"]] * ch["eG"] - qo[:, C_DV:]).astype(BF16)
            o_ref[d, pl.ds(r0, L), :] = qo[:, :C_DV]
            egl_ref[d, c] = jnp.broadcast_to(ch["egl"], (8, LANES))
        return carry

    lax.fori_loop(0, nc // grp, prep, 0)

    def step(c, d, S):
        r0 = pl.multiple_of(c * L, L)
        ks_qs = jnp.dot(kq_ref[d, c], S.astype(BF16), preferred_element_type=F32)
        o_ref[d, pl.ds(r0, L), :] += ks_qs[C_DK:]
        return S * egl_ref[d, c][0:1, 0:1] + (n_ref[d, c] - ks_qs[:C_DK])

    def body(it, carry):
        Sf, Sb = carry
        return step(it, 0, Sf), step(_bwd_chunk(it, nc, ncc), 1, Sb)

    s0 = jnp.zeros((C_DK, C_DV), F32)
    lax.fori_loop(0, nc, body, (s0, s0))

    osum = o_ref[0] + o_ref[1]
    on = osum * lax.rsqrt(jnp.mean(osum * osum, axis=-1, keepdims=True) + EPS) * nw_ref[...]
    y_ref[...] = on * _silu(z_ref[...])


def _gdn_call(p, cw, gcol, grow, par, nw, l, ncc):
    b, t, _ = p.shape
    nc = t // CHUNK
    blk = lambda base: pl.BlockSpec((None, t, LANES), lambda i, h: (i, 0, base // LANES + h))
    cwb = lambda off: pl.BlockSpec((None, 3, LANES), lambda i, h: (l, 0, off + h))
    big = lambda dt: pltpu.VMEM((2, t, LANES), dt)
    return pl.pallas_call(
        functools.partial(_gdn_kernel, nc=nc, ncc=ncc),
        grid=(b, C_HEADS),
        in_specs=[blk(P_CQ), blk(P_CK), blk(P_CV), blk(P_CZ), cwb(0), cwb(C_HEADS), cwb(2 * C_HEADS),
                  pl.BlockSpec((None, None, t, 4), lambda i, h: (i, h, 0, 0)),
                  pl.BlockSpec((None, None, nc, 4, CHUNK), lambda i, h: (i, h, 0, 0, 0)),
                  pl.BlockSpec((None, None, 1, 4), lambda i, h: (l, h, 0, 0)),
                  pl.BlockSpec((None, 1, LANES), lambda i, h: (l, 0, 0))],
        out_specs=pl.BlockSpec((None, t, LANES), lambda i, h: (i, 0, h)),
        out_shape=jax.ShapeDtypeStruct((b, t, BRANCH_W), F32),
        scratch_shapes=[pltpu.VMEM((t, LANES), F32), pltpu.VMEM((t, LANES), F32),
                        pltpu.VMEM((t, LANES), F32),
                        pltpu.VMEM((2, nc, C_DK, C_DV), F32),
                        pltpu.VMEM((2, nc, C_DK + CHUNK, C_DV), BF16),
                        pltpu.VMEM((2, nc, 8, LANES), F32),
                        big(F32)],
        compiler_params=_params(("arbitrary", "arbitrary")),
        name="gdn",
    )(p, p, p, p, cw, cw, cw, gcol, grow, par, nw)


def _merge_kernel(ya_ref, yb_ref, yc_ref, ga_ref, gb_ref, gc_ref, h_ref, mod_ref, wb_ref, wo_ref, o_ref):
    y = (_sigmoid(ga_ref[...]) * _dot(ya_ref[...], wb_ref[0])
         + _sigmoid(gb_ref[...]) * _dot(yb_ref[...], wb_ref[1])
         + _sigmoid(gc_ref[...]) * _dot(yc_ref[...], wb_ref[2]))
    o_ref[...] = h_ref[...] + mod_ref[2:3, :] * _dot(y, wo_ref[...])


def _mod_row(j, nct):
    return 1 if nct == 0 else jnp.minimum(j // nct, 1)


def _merge_call(ya, yb, yc, p, h, mod, wb, wo, l, tm, nct, latent_only):
    b, t, d = h.shape
    skip = nct if latent_only else 0
    yblk = lambda: pl.BlockSpec((None, tm, BRANCH_W), lambda i, j: (i, j + skip, 0))
    gblk = lambda g: pl.BlockSpec((None, tm, d), lambda i, j: (i, j + skip, g))
    return pl.pallas_call(
        _merge_kernel,
        grid=(b, t // tm - skip),
        in_specs=[yblk(), yblk(), yblk(), gblk(0), gblk(1), gblk(2),
                  pl.BlockSpec((None, tm, d), lambda i, j: (i, j + skip, 0)),
                  pl.BlockSpec((None, None, None, 6, d), lambda i, j: (l, i, _mod_row(j + skip, nct), 0, 0)),
                  _resident((None, 3, BRANCH_W, d), lambda i, j: (l, 0, 0, 0)),
                  _resident((None, d, d), lambda i, j: (l, 0, 0))],
        out_specs=pl.BlockSpec((None, tm, d), lambda i, j: (i, j, 0)),
        out_shape=jax.ShapeDtypeStruct((b, t - skip * tm, d), F32),
        compiler_params=_params(("arbitrary", "arbitrary")),
        name="merge",
    )(ya, yb, yc, p, p, p, h, mod, wb, wo)


def _ffn_kernel(h_ref, hp_ref, hn_ref, mod_ref, nw_ref, wup_ref, cw_ref, wdn_ref, o_ref, acc_ref,
                *, nct, ntiles):
    tm = h_ref.shape[0]
    j = pl.program_id(1)
    mod = mod_ref[...]
    nw = nw_ref[...]
    h = h_ref[...]
    xn = _modnorm(h, nw, mod[3:4], mod[4:5]).astype(BF16)
    halo = jnp.concatenate([hp_ref[...], hn_ref[...]], axis=0)
    xh = _modnorm(halo, nw, mod[3:4], mod[4:5]).astype(BF16)
    has_prev = jnp.logical_and(j != 0, j != nct).astype(F32)
    has_next = jnp.logical_and(j != nct - 1, j != ntiles - 1).astype(F32)
    rows = lax.broadcasted_iota(jnp.int32, (tm, 1), 0)
    first = rows == 0
    last = rows == tm - 1

    def up_conv(kind, jf):
        sl = slice(kind * D_FF + jf * FFN_TF, kind * D_FF + (jf + 1) * FFN_TF)
        u = jnp.dot(xn, wup_ref[:, sl], preferred_element_type=F32)
        uh = jnp.dot(xh, wup_ref[:, sl], preferred_element_type=F32)
        cw = cw_ref[:, sl]
        prev = jnp.where(first, uh[7:8] * has_prev, pltpu.roll(u, 1, axis=0))
        nxt = jnp.where(last, uh[8:9] * has_next, pltpu.roll(u, tm - 1, axis=0))
        return prev * cw[0:1] + u * cw[1:2] + nxt * cw[2:3]

    for jf in range(D_FF // FFN_TF):
        act = (up_conv(0, jf) * _silu(up_conv(1, jf))).astype(BF16)
        part = jnp.dot(act, wdn_ref[jf], preferred_element_type=F32)
        if jf == 0:
            acc_ref[...] = part
        else:
            acc_ref[...] += part
    o_ref[...] = h + mod[5:6] * acc_ref[...]


def _ffn_call(h, mod, nw, wup, cw, wdn, l, tm, nct):
    b, t, d = h.shape
    ntiles = t // tm
    hb = tm // 8
    return pl.pallas_call(
        functools.partial(_ffn_kernel, nct=nct, ntiles=ntiles),
        grid=(b, ntiles),
        in_specs=[pl.BlockSpec((None, tm, d), lambda i, j: (i, j, 0)),
                  pl.BlockSpec((None, 8, d), lambda i, j: (i, jnp.maximum(j * hb - 1, 0), 0)),
                  pl.BlockSpec((None, 8, d), lambda i, j: (i, jnp.minimum((j + 1) * hb, t // 8 - 1), 0)),
                  pl.BlockSpec((None, None, None, 6, d), lambda i, j: (l, i, _mod_row(j, nct), 0, 0)),
                  pl.BlockSpec((None, 1, d), lambda i, j: (l, 0, 0)),
                  _resident((None,) + wup.shape[1:], lambda i, j: (l, 0, 0)),
                  _resident((None,) + cw.shape[1:], lambda i, j: (l, 0, 0)),
                  _resident((None,) + wdn.shape[1:], lambda i, j: (l, 0, 0, 0))],
        out_specs=pl.BlockSpec((None, tm, d), lambda i, j: (i, j, 0)),
        out_shape=jax.ShapeDtypeStruct((b, t, d), F32),
        scratch_shapes=[pltpu.VMEM((tm, d), F32)],
        compiler_params=_params(("arbitrary", "arbitrary")),
        name="ffn",
    )(h, h, h, mod, nw, wup, cw, wdn)


def _rope_tables(tc, tl):
    rows = tl // GRID_W
    row = jnp.repeat(jnp.arange(rows, dtype=F32), GRID_W)
    col = jnp.tile(jnp.arange(GRID_W, dtype=F32), rows)
    n_freq = B_DH // 4
    inv = ROPE_BASE ** (-jnp.arange(n_freq, dtype=F32) / n_freq)
    ang_r = row[:, None] * inv
    ang_c = col[:, None] * inv
    cos = jnp.concatenate([jnp.cos(ang_r)] * 2 + [jnp.cos(ang_c)] * 2, axis=1)
    sin = jnp.concatenate([-jnp.sin(ang_r), jnp.sin(ang_r), -jnp.sin(ang_c), jnp.sin(ang_c)], axis=1)
    cos = jnp.concatenate([jnp.ones((tc, B_DH), F32), cos], axis=0)
    sin = jnp.concatenate([jnp.zeros((tc, B_DH), F32), sin], axis=0)
    return jnp.tile(cos, (1, 2)), jnp.tile(sin, (1, 2))


def _pack_w_in(w_in):
    depth, d, _ = w_in.shape
    w = w_in.astype(BF16)
    o = 0
    parts = {}
    for name, width in (("aq", 256), ("ak", 256), ("av", 512), ("ao", 512), ("ag", 16),
                        ("bq", 512), ("bk", 128), ("bv", 128),
                        ("cq", 512), ("ck", 512), ("cv", 512), ("cz", 512), ("ca", 8), ("cb", 8),
                        ("gate", 3 * D_MODEL)):
        parts[name] = w[:, :, o:o + width]
        o += width
    aqk = jnp.concatenate([parts["aq"].reshape(depth, d, A_HEADS, A_DQK),
                           parts["ak"].reshape(depth, d, A_HEADS, A_DQK)], axis=3).reshape(depth, d, 512)
    small = jnp.concatenate([parts["ag"], parts["ca"], parts["cb"]], axis=2)
    pad = jnp.zeros((depth, d, W_COLS - W_SMALL - small.shape[2]), BF16)
    cols = [parts["gate"], parts["av"], parts["ao"], parts["bq"], parts["bk"], parts["bv"],
            parts["cq"], parts["ck"], parts["cv"], parts["cz"], aqk, small, pad]
    return jnp.concatenate(cols, axis=2)


def _gate_layouts(x, heads, kinds):
    b, t, _ = x.shape
    g = x.reshape(b, t, kinds, heads)
    col = g.transpose(0, 3, 1, 2)
    row = g.reshape(b, t // CHUNK, CHUNK, kinds, heads).transpose(0, 4, 1, 3, 2)
    return col, row


def kernel(x, c, ctx, c_ctx, norm1_w, norm2_w, ada_w, ada_b, w_in, a_gate_b, a_norm_w, b_qnorm_w,
           b_knorm_w, c_conv_w, c_a_log, c_dt_bias, c_norm_w, w_branch, w_out, w_up, ffn_conv_w, w_down):
    b, tl, d = x.shape
    tc = ctx.shape[1]
    depth = w_in.shape[0]
    t = tc + tl
    tm = 256 if (tc % 256 == 0 and tl % 256 == 0) else 128
    nct = tc // tm
    ncc = tc // CHUNK
    nj = D_FF // FFN_TF

    cc = jnp.zeros((16, d), F32).at[:b].set(c).at[b].set(c_ctx)
    mods = _ada_call(cc, ada_w, ada_b).reshape(depth, 16, 6, d)
    mod = jnp.stack([jnp.broadcast_to(mods[:, b][:, None], (depth, b, 6, d)), mods[:, :b]], axis=2)

    w_in_p = _pack_w_in(w_in)
    wb = w_branch.astype(BF16)
    wo = w_out.astype(BF16)
    wup = w_up.astype(BF16)
    wdn = w_down.astype(BF16).reshape(depth, nj, FFN_TF, d)
    cos2, sin2 = _rope_tables(tc, tl)
    qw2 = jnp.tile(b_qnorm_w, (1, 2))[:, None]
    kw2 = jnp.tile(b_knorm_w, (1, 2))[:, None]
    a_bcol = a_gate_b.transpose(0, 2, 1)[:, :, :, None]
    c_par = jnp.concatenate([c_a_log, c_dt_bias], axis=1).transpose(0, 2, 1)[:, :, None, :]

    h = jnp.concatenate([ctx, x], axis=1)
    for l in range(depth):
        p, qkt, small = _proj_call(h, mod, norm1_w[:, None], w_in_p, l, tm, nct)
        _, a_grow = _gate_layouts(small[:, :, :16], A_HEADS, 4)
        c_gcol, c_grow = _gate_layouts(small[:, :, 16:32], C_HEADS, 4)
        ya = _mlstm_call(p, qkt, a_grow, a_bcol, a_norm_w[:, None], l, ncc)
        yb = _gqa_call(p, qw2, kw2, cos2, sin2, l, tc)
        yc = _gdn_call(p, c_conv_w, c_gcol, c_grow, c_par, c_norm_w[:, None], l, ncc)
        last = l == depth - 1
        h = _merge_call(ya, yb, yc, p, h, mod, wb, wo, l, tm, nct, latent_only=last)
        h = _ffn_call(h, mod, norm2_w[:, None], wup, ffn_conv_w, wdn, l, tm, 0 if last else nct)
    return h
```

```python
import functools

import jax
import jax.numpy as jnp
from jax import lax
from jax.experimental import pallas as pl
from jax.experimental.pallas import tpu as pltpu

F32 = jnp.float32
BF16 = jnp.bfloat16

D_MODEL = 1024
GRID_W = 64
A_HEADS, A_DQK, A_DV = 4, 64, 128
B_HEADS, B_KV_HEADS, B_DH = 8, 2, 64
C_HEADS, C_DK, C_DV = 4, 128, 128
BRANCH_W = 512
D_FF = 2816
CHUNK = 64
ROPE_BASE = 10000.0
EPS = 1e-6
M_INIT = -1e30
LOG2_E = 1.4426950408889634
NEG = -1e30

LANES = 128
P_GATE = 0
P_AV = 3072
P_AO = 3584
P_BQ = 4096
P_BK = 4608
P_BV = 4736
P_CQ = 4864
P_CK = 5376
P_CV = 5888
P_CZ = 6400
P_COLS = 6912
W_AQK = 6912
W_SMALL = 7424
W_COLS = 7552
PROJ_TN = 512
FFN_TF = 256
Q_TILE = 128
SCAN_GROUP = 12
VMEM_LIMIT = 56 * 1024 * 1024


def _dot(a, b):
    return jnp.dot(a.astype(BF16), b.astype(BF16), preferred_element_type=F32)


def _dot_nt(a, b):
    return lax.dot_general(a.astype(BF16), b.astype(BF16), (((1,), (1,)), ((), ())),
                           preferred_element_type=F32)


def _dot_tn(a, b):
    return lax.dot_general(a.astype(BF16), b.astype(BF16), (((0,), (0,)), ((), ())),
                           preferred_element_type=F32)


def _split(a):
    hi = a.astype(BF16)
    lo = (a - hi.astype(F32)).astype(BF16)
    return hi, lo


def _dot3(a, b):
    ah, al = _split(a)
    bh, bl = _split(b)
    d = functools.partial(jnp.dot, preferred_element_type=F32)
    return d(ah, bh) + (d(al, bh) + d(ah, bl))


def _sigmoid(x):
    return 1.0 / (1.0 + jnp.exp(-x))


def _silu(x):
    return x * _sigmoid(x)


def _softplus(x):
    return jnp.maximum(x, 0.0) + jnp.log1p(jnp.exp(-jnp.abs(x)))


def _logsigmoid(x):
    return -_softplus(-x)


def _modnorm(x, w, shift, scale):
    y = x * lax.rsqrt(jnp.mean(x * x, axis=-1, keepdims=True) + EPS)
    return (y * w) * (1.0 + scale) + shift


def _params(sem, vmem=VMEM_LIMIT):
    return pltpu.CompilerParams(dimension_semantics=sem, vmem_limit_bytes=vmem)


def _resident(shape, index_map):
    return pl.BlockSpec(shape, index_map, pipeline_mode=pl.Buffered(1))


def _ada_kernel(c_ref, w_ref, b_ref, o_ref):
    o_ref[...] = _dot3(_silu(c_ref[...]), w_ref[...]) + b_ref[...]


def _ada_call(cc, ada_w, ada_b):
    depth, d, n = ada_w.shape
    tn = 1536
    return pl.pallas_call(
        _ada_kernel,
        grid=(depth, n // tn),
        in_specs=[pl.BlockSpec(cc.shape, lambda l, j: (0, 0)),
                  pl.BlockSpec((None, d, tn), lambda l, j: (l, 0, j)),
                  pl.BlockSpec((None, 1, tn), lambda l, j: (l, 0, j))],
        out_specs=pl.BlockSpec((None, cc.shape[0], tn), lambda l, j: (l, 0, j)),
        out_shape=jax.ShapeDtypeStruct((depth, cc.shape[0], n), F32),
        compiler_params=_params(("arbitrary", "arbitrary")),
        name="ada",
    )(cc, ada_w, ada_b.reshape(depth, 1, n))


def _proj_kernel(h_ref, mod_ref, nw_ref, w_ref, p_ref, qkt_ref, small_ref):
    tm = h_ref.shape[0]
    mod = mod_ref[...]
    xn = _modnorm(h_ref[...], nw_ref[...], mod[0:1], mod[1:2]).astype(BF16)
    for c0 in range(0, P_COLS, PROJ_TN):
        sl = slice(c0, min(c0 + PROJ_TN, P_COLS))
        p_ref[:, sl] = jnp.dot(xn, w_ref[:, sl], preferred_element_type=F32)
    qk = jnp.dot(xn, w_ref[:, W_AQK:W_SMALL], preferred_element_type=F32)
    for hd in range(A_HEADS):
        for c in range(tm // CHUNK):
            qkt_ref[hd, c] = qk[c * CHUNK:(c + 1) * CHUNK, hd * LANES:(hd + 1) * LANES].T
    small_ref[...] = jnp.dot(xn, w_ref[:, W_SMALL:W_COLS], preferred_element_type=F32)


def _proj_call(h, mod, nw, w, l, tm, nct):
    b, t, d = h.shape
    cpt = tm // CHUNK
    return pl.pallas_call(
        _proj_kernel,
        grid=(b, t // tm),
        in_specs=[pl.BlockSpec((None, tm, d), lambda i, j: (i, j, 0)),
                  pl.BlockSpec((None, None, None, 6, d), lambda i, j: (l, i, _mod_row(j, nct), 0, 0)),
                  pl.BlockSpec((None, 1, d), lambda i, j: (l, 0, 0)),
                  _resident((None, d, W_COLS), lambda i, j: (l, 0, 0))],
        out_specs=[pl.BlockSpec((None, tm, P_COLS), lambda i, j: (i, j, 0)),
                   pl.BlockSpec((None, A_HEADS, cpt, 2 * A_DQK, CHUNK), lambda i, j: (i, 0, j, 0, 0)),
                   pl.BlockSpec((None, tm, LANES), lambda i, j: (i, j, 0))],
        out_shape=[jax.ShapeDtypeStruct((b, t, P_COLS), F32),
                   jax.ShapeDtypeStruct((b, A_HEADS, t // CHUNK, 2 * A_DQK, CHUNK), F32),
                   jax.ShapeDtypeStruct((b, t, LANES), F32)],
        compiler_params=_params(("arbitrary", "arbitrary")),
        name="proj_in",
    )(h, mod, nw, w)


def _chunk_masks():
    row = lax.broadcasted_iota(jnp.int32, (CHUNK, CHUNK), 0)
    col = lax.broadcasted_iota(jnp.int32, (CHUNK, CHUNK), 1)
    return col <= row, col >= row, col < row, col > row


def _bwd_chunk(it, nc, ncc):
    return jnp.where(it < ncc, ncc - 1 - it, nc - 1 - (it - ncc))


def _rows3(x, n_rows=16):
    hi = x.astype(BF16).astype(F32)
    mid = (x - hi).astype(BF16).astype(F32)
    lo = ((x - hi) - mid).astype(BF16).astype(F32)
    r = lax.broadcasted_iota(jnp.int32, (n_rows, x.shape[1]), 0)
    return jnp.where(r == 0, hi, jnp.where(r == 1, mid, jnp.where(r == 2, lo, 0.0))).astype(BF16)


def _mlstm_kernel(qkt_ref, v_ref, o_ref, grow_ref, bcol_ref, nw_ref, y_ref, hf_ref, hb_ref, *, nc, ncc):
    L = CHUNK
    le, ge, _, _ = _chunk_masks()
    vis = (ge, le)
    row_l = lax.broadcasted_iota(jnp.int32, (L, LANES), 0)
    col_l = lax.broadcasted_iota(jnp.int32, (L, LANES), 1)
    ones_l = jnp.ones((L, LANES), BF16)
    zeros_l = jnp.zeros((L, LANES), BF16)
    stat_rhs = [jnp.concatenate(
        [ones_l, jnp.where((col_l < L) & ((col_l >= row_l) if d == 0 else (col_l <= row_l)), 1.0, 0.0).astype(BF16)],
        axis=1) for d in range(2)]
    ones3 = jnp.where(lax.broadcasted_iota(jnp.int32, (16, LANES), 0) < 3, 1.0, 0.0).astype(BF16)
    tail_rhs = jnp.concatenate([jnp.zeros((16, 2 * LANES), BF16), ones3], axis=1)
    bias_col = bcol_ref[...]
    grp = _group_size(nc)
    h_refs = (hf_ref, hb_ref)

    def sum3(x):
        return x[0:1] + x[1:2] + x[2:3]

    def body(gi, carry):
        C = [carry[0], carry[2]]
        m = [carry[1], carry[3]]
        chains = []
        for j in range(grp):
            it = gi * grp + j
            for d, c in ((0, it), (1, _bwd_chunk(it, nc, ncc))):
                gr = grow_ref[c] + bias_col
                x = qkt_ref[c]
                chains.append(dict(d=d, r0=pl.multiple_of(c * L, L), i_row=gr[2 * d:2 * d + 1, :],
                                   f3=_rows3(_logsigmoid(gr[2 * d + 1:2 * d + 2, :])),
                                   qt=x[:A_DQK] * (A_DQK ** -0.5), kt=x[A_DQK:]))
        for ch in chains:
            ch["v"] = v_ref[pl.ds(ch["r0"], L), :].astype(BF16)
            ch["sqt"] = _dot_tn(ch["kt"], ch["qt"])
        for ch in chains:
            st = jnp.dot(ch["f3"], stat_rhs[ch["d"]], preferred_element_type=F32)
            ch["btot"] = sum3(st[:, :LANES])
            ch["bcum"] = sum3(st[:, LANES:LANES + L])
            ch["a_row"] = ch["i_row"] - ch["bcum"]
        for ch in chains:
            ch["a_col"] = lax.dot_general(_rows3(ch["a_row"]), ones3, (((0,), (0,)), ((), ())),
                                          preferred_element_type=F32)
        for ch in chains:
            d = ch["d"]
            m_new = jnp.maximum(ch["btot"] + m[d], ch["btot"] + jnp.max(ch["a_col"], axis=0, keepdims=True))
            dlog = jnp.where(vis[d], ch["bcum"] + ch["a_col"][:, :L], NEG)
            inter = ch["bcum"] + m[d][:, :L]
            mt = jnp.maximum(inter, jnp.max(dlog, axis=0, keepdims=True))
            ch["st"] = (ch["sqt"] * jnp.exp(dlog - mt)).astype(BF16)
            ch["qe"] = (ch["qt"] * jnp.exp(inter - mt)).astype(BF16)
            ch["e3"] = _rows3(jnp.exp(-mt))
            ch["kw"] = (ch["kt"] * jnp.exp(ch["btot"][:, :L] + ch["a_row"] - m_new[:, :L])).astype(BF16)
            decay = jnp.exp(ch["btot"] + m[d] - m_new)
            ch["decay"] = jnp.concatenate([decay, decay], axis=1)
            m[d] = m_new
        for ch in chains:
            ch["kv"] = jnp.dot(ch["kw"], jnp.concatenate([ch["v"], ones_l], axis=1), preferred_element_type=F32)
            lhs = jnp.concatenate([ch["st"], ch["e3"]], axis=0)
            rhs = jnp.concatenate([jnp.concatenate([ch["v"], ones_l, zeros_l], axis=1), tail_rhs], axis=0)
            ch["intra"] = lax.dot_general(lhs, rhs, (((0,), (0,)), ((), ())), preferred_element_type=F32)
        for ch in chains:
            d = ch["d"]
            ch["inter"] = lax.dot_general(ch["qe"], C[d].astype(BF16), (((0,), (0,)), ((), ())),
                                          preferred_element_type=F32)
            C[d] = ch["decay"] * C[d] + ch["kv"]
        for ch in chains:
            num = ch["intra"][:, :A_DV] + ch["inter"][:, :A_DV]
            den = ch["intra"][:, A_DV:2 * A_DV] + ch["inter"][:, A_DV:]
            h_refs[ch["d"]][pl.ds(ch["r0"], L), :] = num / jnp.maximum(jnp.abs(den), ch["intra"][:, 2 * A_DV:])
        return C[0], m[0], C[1], m[1]

    c0 = jnp.zeros((A_DQK, 2 * LANES), F32)
    m0 = jnp.full((1, LANES), M_INIT, F32)
    lax.fori_loop(0, nc // grp, body, (c0, m0, c0, m0))

    hsum = hf_ref[...] + hb_ref[...]
    hn = hsum * lax.rsqrt(jnp.mean(hsum * hsum, axis=-1, keepdims=True) + EPS)
    y_ref[...] = hn * nw_ref[...] * _sigmoid(o_ref[...])


def _mlstm_call(p, qkt, grow, bcol, nw, l, ncc):
    b, t, _ = p.shape
    nc = t // CHUNK
    blk = lambda base: pl.BlockSpec((None, t, LANES), lambda i, h: (i, 0, base // LANES + h))
    return pl.pallas_call(
        functools.partial(_mlstm_kernel, nc=nc, ncc=ncc),
        grid=(b, A_HEADS),
        in_specs=[pl.BlockSpec((None, None, nc, 2 * A_DQK, CHUNK), lambda i, h: (i, h, 0, 0, 0)),
                  blk(P_AV), blk(P_AO),
                  pl.BlockSpec((None, None, nc, 4, CHUNK), lambda i, h: (i, h, 0, 0, 0)),
                  pl.BlockSpec((None, None, 4, 1), lambda i, h: (l, h, 0, 0)),
                  pl.BlockSpec((None, 1, LANES), lambda i, h: (l, 0, h))],
        out_specs=pl.BlockSpec((None, t, LANES), lambda i, h: (i, 0, h)),
        out_shape=jax.ShapeDtypeStruct((b, t, BRANCH_W), F32),
        scratch_shapes=[pltpu.VMEM((t, LANES), F32), pltpu.VMEM((t, LANES), F32)],
        compiler_params=_params(("arbitrary", "arbitrary")),
        name="mlstm",
    )(qkt, p, p, grow, bcol, nw)


def _headnorm_rope(x, bd, w, cos, sin_signed, lane_half):
    hi, lo = _split(x * x)
    ssum = jnp.dot(hi, bd, preferred_element_type=F32) + jnp.dot(lo, bd, preferred_element_type=F32)
    y = x * lax.rsqrt(ssum * (1.0 / B_DH) + EPS) * w
    swapped = jnp.where(lane_half, pltpu.roll(y, LANES - 16, axis=1), pltpu.roll(y, 16, axis=1))
    return y * cos + swapped * sin_signed


def _gqa_kernel(q_ref, k_ref, v_ref, qw_ref, kw_ref, cosq_ref, sinq_ref, cosk_ref, sink_ref, y_ref,
                kn_ref, vb_ref, *, tc, nqc):
    tq = q_ref.shape[0]
    t = k_ref.shape[0]
    qb = pl.program_id(1)
    r = lax.broadcasted_iota(jnp.int32, (LANES, LANES), 0)
    c = lax.broadcasted_iota(jnp.int32, (LANES, LANES), 1)
    bd = jnp.where(r // B_DH == c // B_DH, 1.0, 0.0).astype(BF16)

    def lane_half(n):
        return (lax.broadcasted_iota(jnp.int32, (n, LANES), 1) % 32) < 16

    @pl.when(qb == 0)
    def _():
        kn = _headnorm_rope(k_ref[...], bd, kw_ref[...], cosk_ref[...], sink_ref[...], lane_half(t))
        kn_ref[...] = kn.astype(BF16)
        v = v_ref[...]
        first = lax.broadcasted_iota(jnp.int32, (t, LANES), 1) < B_DH
        vb_ref[0] = jnp.where(first, v, 1.0).astype(BF16)
        vb_ref[1] = jnp.where(first, pltpu.roll(v, B_DH, axis=1), 1.0).astype(BF16)

    qn = []
    for j in range(B_HEADS * B_DH // LANES):
        sl = slice(j * LANES, (j + 1) * LANES)
        y = _headnorm_rope(q_ref[:, sl], bd, qw_ref[...], cosq_ref[...], sinq_ref[...], lane_half(tq))
        qn.append((y * (B_DH ** -0.5 * LOG2_E)).astype(BF16))

    group = B_HEADS // B_KV_HEADS

    def attend(nk):
        scores = []
        for kvh in range(B_KV_HEADS):
            kh = kn_ref[0:nk, kvh * B_DH:(kvh + 1) * B_DH]
            qs = jnp.concatenate(
                [qn[h // 2][:, (h % 2) * B_DH:(h % 2 + 1) * B_DH]
                 for h in range(kvh * group, (kvh + 1) * group)], axis=0)
            scores.append(lax.dot_general(qs, kh, (((1,), (1,)), ((), ())), preferred_element_type=F32))
        for kvh, s in enumerate(scores):
            p = jnp.exp2(s - jnp.max(s, axis=-1, keepdims=True))
            oe = jnp.dot(p.astype(BF16), vb_ref[kvh, 0:nk, :], preferred_element_type=F32)
            o = oe * pltpu.roll(1.0 / oe, B_DH, axis=1)
            for g in range(group):
                h = kvh * group + g
                y_ref[:, h * B_DH:(h + 1) * B_DH] = o[g * tq:(g + 1) * tq, :B_DH]

    @pl.when(qb < nqc)
    def _():
        attend(tc)

    @pl.when(qb >= nqc)
    def _():
        attend(t)


def _gqa_call(p, qw, kw, cos2, sin2, l, tc):
    b, t, _ = p.shape
    tq = Q_TILE
    row_blk = lambda: pl.BlockSpec((tq, LANES), lambda i, j: (j, 0))
    full = lambda: pl.BlockSpec((t, LANES), lambda i, j: (0, 0))
    return pl.pallas_call(
        functools.partial(_gqa_kernel, tc=tc, nqc=tc // tq),
        grid=(b, t // tq),
        in_specs=[pl.BlockSpec((None, tq, B_HEADS * B_DH), lambda i, j: (i, j, P_BQ // (B_HEADS * B_DH))),
                  pl.BlockSpec((None, t, LANES), lambda i, j: (i, 0, P_BK // LANES)),
                  pl.BlockSpec((None, t, LANES), lambda i, j: (i, 0, P_BV // LANES)),
                  pl.BlockSpec((None, 1, LANES), lambda i, j: (l, 0, 0)),
                  pl.BlockSpec((None, 1, LANES), lambda i, j: (l, 0, 0)),
                  row_blk(), row_blk(), full(), full()],
        out_specs=pl.BlockSpec((None, tq, BRANCH_W), lambda i, j: (i, j, 0)),
        out_shape=jax.ShapeDtypeStruct((b, t, BRANCH_W), F32),
        scratch_shapes=[pltpu.VMEM((t, LANES), BF16), pltpu.VMEM((B_KV_HEADS, t, LANES), BF16)],
        compiler_params=_params(("arbitrary", "arbitrary")),
        name="gqa",
    )(p, p, p, qw, kw, cos2, sin2, cos2, sin2)


def _group_size(nc):
    return max(g for g in range(1, SCAN_GROUP + 1) if nc % g == 0)


def _gdn_kernel(q_ref, k_ref, v_ref, z_ref, cwq_ref, cwk_ref, cwv_ref, gcol_ref, grow_ref, par_ref,
                nw_ref, y_ref, qn_ref, kn_ref, vn_ref, n_ref, kq_ref, egl_ref, o_ref, *, nc, ncc):
    L = CHUNK
    t = q_ref.shape[0]
    tc = ncc * L
    le, ge, lt, gt = _chunk_masks()
    par = par_ref[...]

    rows = lax.broadcasted_iota(jnp.int32, (t, 1), 0)
    has_prev = jnp.logical_and(rows != 0, rows != tc)
    has_next = jnp.logical_and(rows != tc - 1, rows != t - 1)

    def conv_silu(x_ref, cw_ref):
        x = x_ref[...]
        cw = cw_ref[...]
        prev = jnp.where(has_prev, pltpu.roll(x, 1, axis=0), 0.0)
        nxt = jnp.where(has_next, pltpu.roll(x, t - 1, axis=0), 0.0)
        return _silu(prev * cw[0:1] + x * cw[1:2] + nxt * cw[2:3])

    def l2n(x):
        return x * lax.rsqrt(jnp.sum(x * x, axis=-1, keepdims=True) + EPS)

    qn_ref[...] = l2n(conv_silu(q_ref, cwq_ref)) * (C_DK ** -0.5)
    kn_ref[...] = l2n(conv_silu(k_ref, cwk_ref))
    vn_ref[...] = conv_silu(v_ref, cwv_ref)

    grp = _group_size(nc)
    ng = nc // grp
    eye = jnp.where(le & ge, 1.0, 0.0).astype(F32)

    def prep_stages(gi):
        chains = []
        for j in range(grp):
            it = gi * grp + j
            for d, c in ((0, it), (1, _bwd_chunk(it, nc, ncc))):
                r0 = pl.multiple_of(c * L, L)
                chains.append(dict(it=it, d=d, c=c, r0=r0, q=qn_ref[pl.ds(r0, L), :],
                                   k=kn_ref[pl.ds(r0, L), :], v=vn_ref[pl.ds(r0, L), :]))
        for ch in chains:
            ch["kkqk"] = _dot_nt(jnp.concatenate([ch["k"], ch["q"]], axis=0), ch["k"])
        yield
        for ch in chains:
            d = ch["d"]
            gc = gcol_ref[pl.ds(ch["r0"], L), :]
            gr = grow_ref[ch["c"]]
            mask_in, mask_t, strict = (le, ge, lt) if d == 0 else (ge, le, gt)
            neg_rate = -jnp.exp(par[:, d:d + 1])
            g_col = neg_rate * _softplus(gc[:, d:d + 1] + par[:, 2 + d:3 + d])
            g_row = neg_rate * _softplus(gr[d:d + 1, :] + par[:, 2 + d:3 + d])
            beta = _sigmoid(gc[:, 2 + d:3 + d])
            G_col = jnp.sum(jnp.where(mask_in, g_row, 0.0), axis=1, keepdims=True)
            G_row = jnp.sum(jnp.where(mask_t, g_col, 0.0), axis=0, keepdims=True)
            dec = jnp.exp(jnp.where(mask_in, G_col - G_row, NEG))
            g_last = jnp.sum(g_row, axis=1, keepdims=True)
            ch.update(beta=beta, dec=dec, eG=jnp.exp(G_col), kscale=jnp.exp(g_last - G_col),
                      egl=jnp.exp(g_last), x=-jnp.where(strict, beta * ch["kkqk"][:L] * dec, 0.0))
        invs = [eye + ch["x"] for ch in chains]
        pws = [ch["x"] for ch in chains]
        n = 1
        while 2 * n < L:
            pws = [_dot(pw, pw) for pw in pws]
            yield
            invs = [inv + _dot(inv, pw) for inv, pw in zip(invs, pws)]
            yield
            n *= 2
        uws = []
        for ch, inv in zip(chains, invs):
            rhs = jnp.concatenate([ch["v"] * ch["beta"], ch["k"] * (ch["beta"] * ch["eG"])], axis=1)
            uws.append(_dot(inv, rhs))
        yield
        kns = [_dot_tn(ch["k"] * ch["kscale"], uw) for ch, uw in zip(chains, uws)]
        yield
        qos = [_dot(ch["kkqk"][L:] * ch["dec"], uw) for ch, uw in zip(chains, uws)]
        yield
        for ch, kn, qo in zip(chains, kns, qos):
            d, it = ch["d"], ch["it"]
            n_ref[d, it] = kn[:, :C_DV]
            kq_ref[d, it, 0:C_DK, :] = kn[:, C_DV:].astype(BF16)
            kq_ref[d, it, C_DK:C_DK + L, :] = (ch["q"] * ch["eG"] - qo[:, C_DV:]).astype(BF16)
            o_ref[d, pl.ds(ch["r0"], L), :] = qo[:, :C_DV]
            egl_ref[d, it] = jnp.broadcast_to(ch["egl"], (8, LANES))

    def recur(it, S):
        out = []
        for d, c in ((0, it), (1, _bwd_chunk(it, nc, ncc))):
            r0 = pl.multiple_of(c * L, L)
            ks_qs = jnp.dot(kq_ref[d, it], S[d].astype(BF16), preferred_element_type=F32)
            o_ref[d, pl.ds(r0, L), :] += ks_qs[C_DK:]
            out.append(S[d] * egl_ref[d, it][0:1, 0:1] + (n_ref[d, it] - ks_qs[:C_DK]))
        return out

    for _ in prep_stages(0):
        pass

    def body(gi, carry):
        S = list(carry)
        stages = prep_stages(gi)
        for j in range(grp):
            next(stages, None)
            S = recur((gi - 1) * grp + j, S)
        for _ in stages:
            pass
        return tuple(S)

    s0 = jnp.zeros((C_DK, C_DV), F32)
    S = list(lax.fori_loop(1, ng, body, (s0, s0)))
    for j in range(grp):
        S = recur((ng - 1) * grp + j, S)

    osum = o_ref[0] + o_ref[1]
    on = osum * lax.rsqrt(jnp.mean(osum * osum, axis=-1, keepdims=True) + EPS) * nw_ref[...]
    y_ref[...] = on * _silu(z_ref[...])


def _gdn_call(p, cw, gcol, grow, par, nw, l, ncc):
    b, t, _ = p.shape
    nc = t // CHUNK
    blk = lambda base: pl.BlockSpec((None, t, LANES), lambda i, h: (i, 0, base // LANES + h))
    cwb = lambda off: pl.BlockSpec((None, 3, LANES), lambda i, h: (l, 0, off + h))
    big = lambda dt: pltpu.VMEM((2, t, LANES), dt)
    return pl.pallas_call(
        functools.partial(_gdn_kernel, nc=nc, ncc=ncc),
        grid=(b, C_HEADS),
        in_specs=[blk(P_CQ), blk(P_CK), blk(P_CV), blk(P_CZ), cwb(0), cwb(C_HEADS), cwb(2 * C_HEADS),
                  pl.BlockSpec((None, None, t, 4), lambda i, h: (i, h, 0, 0)),
                  pl.BlockSpec((None, None, nc, 4, CHUNK), lambda i, h: (i, h, 0, 0, 0)),
                  pl.BlockSpec((None, None, 1, 4), lambda i, h: (l, h, 0, 0)),
                  pl.BlockSpec((None, 1, LANES), lambda i, h: (l, 0, 0))],
        out_specs=pl.BlockSpec((None, t, LANES), lambda i, h: (i, 0, h)),
        out_shape=jax.ShapeDtypeStruct((b, t, BRANCH_W), F32),
        scratch_shapes=[pltpu.VMEM((t, LANES), F32), pltpu.VMEM((t, LANES), F32),
                        pltpu.VMEM((t, LANES), F32),
                        pltpu.VMEM((2, nc, C_DK, C_DV), F32),
                        pltpu.VMEM((2, nc, C_DK + CHUNK, C_DV), BF16),
                        pltpu.VMEM((2, nc, 8, LANES), F32),
                        big(F32)],
        compiler_params=_params(("arbitrary", "arbitrary")),
        name="gdn",
    )(p, p, p, p, cw, cw, cw, gcol, grow, par, nw)


def _merge_kernel(ya_ref, yb_ref, yc_ref, ga_ref, gb_ref, gc_ref, h_ref, mod_ref, wb_ref, wo_ref, o_ref):
    y = (_sigmoid(ga_ref[...]) * _dot(ya_ref[...], wb_ref[0])
         + _sigmoid(gb_ref[...]) * _dot(yb_ref[...], wb_ref[1])
         + _sigmoid(gc_ref[...]) * _dot(yc_ref[...], wb_ref[2]))
    o_ref[...] = h_ref[...] + mod_ref[2:3, :] * _dot(y, wo_ref[...])


def _mod_row(j, nct):
    return 1 if nct == 0 else jnp.minimum(j // nct, 1)


def _merge_call(ya, yb, yc, p, h, mod, wb, wo, l, tm, nct, latent_only):
    b, t, d = h.shape
    skip = nct if latent_only else 0
    yblk = lambda: pl.BlockSpec((None, tm, BRANCH_W), lambda i, j: (i, j + skip, 0))
    gblk = lambda g: pl.BlockSpec((None, tm, d), lambda i, j: (i, j + skip, g))
    return pl.pallas_call(
        _merge_kernel,
        grid=(b, t // tm - skip),
        in_specs=[yblk(), yblk(), yblk(), gblk(0), gblk(1), gblk(2),
                  pl.BlockSpec((None, tm, d), lambda i, j: (i, j + skip, 0)),
                  pl.BlockSpec((None, None, None, 6, d), lambda i, j: (l, i, _mod_row(j + skip, nct), 0, 0)),
                  _resident((None, 3, BRANCH_W, d), lambda i, j: (l, 0, 0, 0)),
                  _resident((None, d, d), lambda i, j: (l, 0, 0))],
        out_specs=pl.BlockSpec((None, tm, d), lambda i, j: (i, j, 0)),
        out_shape=jax.ShapeDtypeStruct((b, t - skip * tm, d), F32),
        compiler_params=_params(("arbitrary", "arbitrary")),
        name="merge",
    )(ya, yb, yc, p, p, p, h, mod, wb, wo)


def _ffn_kernel(h_ref, hp_ref, hn_ref, mod_ref, nw_ref, wup_ref, cw_ref, wdn_ref, o_ref, acc_ref,
                *, nct, ntiles):
    tm = h_ref.shape[0]
    j = pl.program_id(1)
    mod = mod_ref[...]
    nw = nw_ref[...]
    h = h_ref[...]
    xn = _modnorm(h, nw, mod[3:4], mod[4:5]).astype(BF16)
    halo = jnp.concatenate([hp_ref[...], hn_ref[...]], axis=0)
    xh = _modnorm(halo, nw, mod[3:4], mod[4:5]).astype(BF16)
    has_prev = jnp.logical_and(j != 0, j != nct).astype(F32)
    has_next = jnp.logical_and(j != nct - 1, j != ntiles - 1).astype(F32)
    rows = lax.broadcasted_iota(jnp.int32, (tm, 1), 0)
    first = rows == 0
    last = rows == tm - 1

    def up_conv(kind, jf):
        sl = slice(kind * D_FF + jf * FFN_TF, kind * D_FF + (jf + 1) * FFN_TF)
        u = jnp.dot(xn, wup_ref[:, sl], preferred_element_type=F32)
        uh = jnp.dot(xh, wup_ref[:, sl], preferred_element_type=F32)
        cw = cw_ref[:, sl]
        prev = jnp.where(first, uh[7:8] * has_prev, pltpu.roll(u, 1, axis=0))
        nxt = jnp.where(last, uh[8:9] * has_next, pltpu.roll(u, tm - 1, axis=0))
        return prev * cw[0:1] + u * cw[1:2] + nxt * cw[2:3]

    for jf in range(D_FF // FFN_TF):
        act = (up_conv(0, jf) * _silu(up_conv(1, jf))).astype(BF16)
        part = jnp.dot(act, wdn_ref[jf], preferred_element_type=F32)
        if jf == 0:
            acc_ref[...] = part
        else:
            acc_ref[...] += part
    o_ref[...] = h + mod[5:6] * acc_ref[...]


def _ffn_call(h, mod, nw, wup, cw, wdn, l, tm, nct):
    b, t, d = h.shape
    ntiles = t // tm
    hb = tm // 8
    return pl.pallas_call(
        functools.partial(_ffn_kernel, nct=nct, ntiles=ntiles),
        grid=(b, ntiles),
        in_specs=[pl.BlockSpec((None, tm, d), lambda i, j: (i, j, 0)),
                  pl.BlockSpec((None, 8, d), lambda i, j: (i, jnp.maximum(j * hb - 1, 0), 0)),
                  pl.BlockSpec((None, 8, d), lambda i, j: (i, jnp.minimum((j + 1) * hb, t // 8 - 1), 0)),
                  pl.BlockSpec((None, None, None, 6, d), lambda i, j: (l, i, _mod_row(j, nct), 0, 0)),
                  pl.BlockSpec((None, 1, d), lambda i, j: (l, 0, 0)),
                  _resident((None,) + wup.shape[1:], lambda i, j: (l, 0, 0)),
                  _resident((None,) + cw.shape[1:], lambda i, j: (l, 0, 0)),
                  _resident((None,) + wdn.shape[1:], lambda i, j: (l, 0, 0, 0))],
        out_specs=pl.BlockSpec((None, tm, d), lambda i, j: (i, j, 0)),
        out_shape=jax.ShapeDtypeStruct((b, t, d), F32),
        scratch_shapes=[pltpu.VMEM((tm, d), F32)],
        compiler_params=_params(("arbitrary", "arbitrary")),
        name="ffn",
    )(h, h, h, mod, nw, wup, cw, wdn)


def _rope_tables(tc, tl):
    rows = tl // GRID_W
    row = jnp.repeat(jnp.arange(rows, dtype=F32), GRID_W)
    col = jnp.tile(jnp.arange(GRID_W, dtype=F32), rows)
    n_freq = B_DH // 4
    inv = ROPE_BASE ** (-jnp.arange(n_freq, dtype=F32) / n_freq)
    ang_r = row[:, None] * inv
    ang_c = col[:, None] * inv
    cos = jnp.concatenate([jnp.cos(ang_r)] * 2 + [jnp.cos(ang_c)] * 2, axis=1)
    sin = jnp.concatenate([-jnp.sin(ang_r), jnp.sin(ang_r), -jnp.sin(ang_c), jnp.sin(ang_c)], axis=1)
    cos = jnp.concatenate([jnp.ones((tc, B_DH), F32), cos], axis=0)
    sin = jnp.concatenate([jnp.zeros((tc, B_DH), F32), sin], axis=0)
    return jnp.tile(cos, (1, 2)), jnp.tile(sin, (1, 2))


def _pack_w_in(w_in):
    depth, d, _ = w_in.shape
    w = w_in.astype(BF16)
    o = 0
    parts = {}
    for name, width in (("aq", 256), ("ak", 256), ("av", 512), ("ao", 512), ("ag", 16),
                        ("bq", 512), ("bk", 128), ("bv", 128),
                        ("cq", 512), ("ck", 512), ("cv", 512), ("cz", 512), ("ca", 8), ("cb", 8),
                        ("gate", 3 * D_MODEL)):
        parts[name] = w[:, :, o:o + width]
        o += width
    aqk = jnp.concatenate([parts["aq"].reshape(depth, d, A_HEADS, A_DQK),
                           parts["ak"].reshape(depth, d, A_HEADS, A_DQK)], axis=3).reshape(depth, d, 512)
    small = jnp.concatenate([parts["ag"], parts["ca"], parts["cb"]], axis=2)
    pad = jnp.zeros((depth, d, W_COLS - W_SMALL - small.shape[2]), BF16)
    cols = [parts["gate"], parts["av"], parts["ao"], parts["bq"], parts["bk"], parts["bv"],
            parts["cq"], parts["ck"], parts["cv"], parts["cz"], aqk, small, pad]
    return jnp.concatenate(cols, axis=2)


def _gate_layouts(x, heads, kinds):
    b, t, _ = x.shape
    g = x.reshape(b, t, kinds, heads)
    col = g.transpose(0, 3, 1, 2)
    row = g.reshape(b, t // CHUNK, CHUNK, kinds, heads).transpose(0, 4, 1, 3, 2)
    return col, row


def kernel(x, c, ctx, c_ctx, norm1_w, norm2_w, ada_w, ada_b, w_in, a_gate_b, a_norm_w, b_qnorm_w,
           b_knorm_w, c_conv_w, c_a_log, c_dt_bias, c_norm_w, w_branch, w_out, w_up, ffn_conv_w, w_down):
    b, tl, d = x.shape
    tc = ctx.shape[1]
    depth = w_in.shape[0]
    t = tc + tl
    tm = 256 if (tc % 256 == 0 and tl % 256 == 0) else 128
    nct = tc // tm
    ncc = tc // CHUNK
    nj = D_FF // FFN_TF

    cc = jnp.zeros((16, d), F32).at[:b].set(c).at[b].set(c_ctx)
    mods = _ada_call(cc, ada_w, ada_b).reshape(depth, 16, 6, d)
    mod = jnp.stack([jnp.broadcast_to(mods[:, b][:, None], (depth, b, 6, d)), mods[:, :b]], axis=2)

    w_in_p = _pack_w_in(w_in)
    wb = w_branch.astype(BF16)
    wo = w_out.astype(BF16)
    wup = w_up.astype(BF16)
    wdn = w_down.astype(BF16).reshape(depth, nj, FFN_TF, d)
    cos2, sin2 = _rope_tables(tc, tl)
    qw2 = jnp.tile(b_qnorm_w, (1, 2))[:, None]
    kw2 = jnp.tile(b_knorm_w, (1, 2))[:, None]
    a_bcol = a_gate_b.transpose(0, 2, 1)[:, :, :, None]
    c_par = jnp.concatenate([c_a_log, c_dt_bias], axis=1).transpose(0, 2, 1)[:, :, None, :]

    h = jnp.concatenate([ctx, x], axis=1)
    for l in range(depth):
        p, qkt, small = _proj_call(h, mod, norm1_w[:, None], w_in_p, l, tm, nct)
        _, a_grow = _gate_layouts(small[:, :, :16], A_HEADS, 4)
        c_gcol, c_grow = _gate_layouts(small[:, :, 16:32], C_HEADS, 4)
        ya = _mlstm_call(p, qkt, a_grow, a_bcol, a_norm_w[:, None], l, ncc)
        yb = _gqa_call(p, qw2, kw2, cos2, sin2, l, tc)
        yc = _gdn_call(p, c_conv_w, c_gcol, c_grow, c_par, c_norm_w[:, None], l, ncc)
        last = l == depth - 1
        h = _merge_call(ya, yb, yc, p, h, mod, wb, wo, l, tm, nct, latent_only=last)
        h = _ffn_call(h, mod, norm2_w[:, None], wup, ffn_conv_w, wdn, l, tm, 0 if last else nct)
    return h
```

```python
import functools

import jax
import jax.numpy as jnp
from jax import lax
from jax.experimental import pallas as pl
from jax.experimental.pallas import tpu as pltpu

F32 = jnp.float32
BF16 = jnp.bfloat16

D_MODEL = 1024
GRID_W = 64
A_HEADS, A_DQK, A_DV = 4, 64, 128
B_HEADS, B_KV_HEADS, B_DH = 8, 2, 64
C_HEADS, C_DK, C_DV = 4, 128, 128
BRANCH_W = 512
D_FF = 2816
CHUNK = 64
ROPE_BASE = 10000.0
EPS = 1e-6
M_INIT = -1e30
LOG2_E = 1.4426950408889634
NEG = -1e30

LANES = 128
P_GATE = 0
P_AV = 3072
P_AO = 3584
P_BQ = 4096
P_BK = 4608
P_BV = 4736
P_CQ = 4864
P_CK = 5376
P_CV = 5888
P_CZ = 6400
P_COLS = 6912
W_AQK = 6912
W_SMALL = 7424
W_COLS = 7552
PROJ_TN = 512
FFN_TF = 256
FFN_AHEAD = 2
Q_TILE = 128
SCAN_GROUP = 12
VMEM_LIMIT = 56 * 1024 * 1024


def _dot(a, b):
    return jnp.dot(a.astype(BF16), b.astype(BF16), preferred_element_type=F32)


def _dot_nt(a, b):
    return lax.dot_general(a.astype(BF16), b.astype(BF16), (((1,), (1,)), ((), ())),
                           preferred_element_type=F32)


def _dot_tn(a, b):
    return lax.dot_general(a.astype(BF16), b.astype(BF16), (((0,), (0,)), ((), ())),
                           preferred_element_type=F32)


def _split(a):
    hi = a.astype(BF16)
    lo = (a - hi.astype(F32)).astype(BF16)
    return hi, lo


def _dot3(a, b):
    ah, al = _split(a)
    bh, bl = _split(b)
    d = functools.partial(jnp.dot, preferred_element_type=F32)
    return d(ah, bh) + (d(al, bh) + d(ah, bl))


def _sigmoid(x):
    return 1.0 / (1.0 + jnp.exp(-x))


def _silu(x):
    return x * _sigmoid(x)


def _softplus(x):
    return jnp.maximum(x, 0.0) + jnp.log1p(jnp.exp(-jnp.abs(x)))


def _logsigmoid(x):
    return -_softplus(-x)


def _modnorm(x, w, shift, scale):
    y = x * lax.rsqrt(jnp.mean(x * x, axis=-1, keepdims=True) + EPS)
    return (y * w) * (1.0 + scale) + shift


def _params(sem, vmem=VMEM_LIMIT):
    return pltpu.CompilerParams(dimension_semantics=sem, vmem_limit_bytes=vmem)


def _resident(shape, index_map):
    return pl.BlockSpec(shape, index_map, pipeline_mode=pl.Buffered(1))


def _ada_kernel(c_ref, w_ref, b_ref, o_ref):
    o_ref[...] = _dot3(_silu(c_ref[...]), w_ref[...]) + b_ref[...]


def _ada_call(cc, ada_w, ada_b):
    depth, d, n = ada_w.shape
    tn = 1536
    return pl.pallas_call(
        _ada_kernel,
        grid=(depth, n // tn),
        in_specs=[pl.BlockSpec(cc.shape, lambda l, j: (0, 0)),
                  pl.BlockSpec((None, d, tn), lambda l, j: (l, 0, j)),
                  pl.BlockSpec((None, 1, tn), lambda l, j: (l, 0, j))],
        out_specs=pl.BlockSpec((None, cc.shape[0], tn), lambda l, j: (l, 0, j)),
        out_shape=jax.ShapeDtypeStruct((depth, cc.shape[0], n), F32),
        compiler_params=_params(("arbitrary", "arbitrary")),
        name="ada",
    )(cc, ada_w, ada_b.reshape(depth, 1, n))


def _proj_kernel(h_ref, mod_ref, nw_ref, w_ref, p_ref, qkt_ref, small_ref):
    tm = h_ref.shape[0]
    mod = mod_ref[...]
    xn = _modnorm(h_ref[...], nw_ref[...], mod[0:1], mod[1:2]).astype(BF16)
    for c0 in range(0, P_COLS, PROJ_TN):
        sl = slice(c0, min(c0 + PROJ_TN, P_COLS))
        p_ref[:, sl] = jnp.dot(xn, w_ref[:, sl], preferred_element_type=F32)
    qk = jnp.dot(xn, w_ref[:, W_AQK:W_SMALL], preferred_element_type=F32)
    for hd in range(A_HEADS):
        for c in range(tm // CHUNK):
            qkt_ref[hd, c] = qk[c * CHUNK:(c + 1) * CHUNK, hd * LANES:(hd + 1) * LANES].T
    small_ref[...] = jnp.dot(xn, w_ref[:, W_SMALL:W_COLS], preferred_element_type=F32)


def _proj_call(h, mod, nw, w, l, tm, nct):
    b, t, d = h.shape
    cpt = tm // CHUNK
    return pl.pallas_call(
        _proj_kernel,
        grid=(b, t // tm),
        in_specs=[pl.BlockSpec((None, tm, d), lambda i, j: (i, j, 0)),
                  pl.BlockSpec((None, None, None, 6, d), lambda i, j: (l, i, _mod_row(j, nct), 0, 0)),
                  pl.BlockSpec((None, 1, d), lambda i, j: (l, 0, 0)),
                  _resident((None, d, W_COLS), lambda i, j: (l, 0, 0))],
        out_specs=[pl.BlockSpec((None, tm, P_COLS), lambda i, j: (i, j, 0)),
                   pl.BlockSpec((None, A_HEADS, cpt, 2 * A_DQK, CHUNK), lambda i, j: (i, 0, j, 0, 0)),
                   pl.BlockSpec((None, tm, LANES), lambda i, j: (i, j, 0))],
        out_shape=[jax.ShapeDtypeStruct((b, t, P_COLS), F32),
                   jax.ShapeDtypeStruct((b, A_HEADS, t // CHUNK, 2 * A_DQK, CHUNK), F32),
                   jax.ShapeDtypeStruct((b, t, LANES), F32)],
        compiler_params=_params(("arbitrary", "arbitrary")),
        name="proj_in",
    )(h, mod, nw, w)


def _chunk_masks():
    row = lax.broadcasted_iota(jnp.int32, (CHUNK, CHUNK), 0)
    col = lax.broadcasted_iota(jnp.int32, (CHUNK, CHUNK), 1)
    return col <= row, col >= row, col < row, col > row


def _bwd_chunk(it, nc, ncc):
    return jnp.where(it < ncc, ncc - 1 - it, nc - 1 - (it - ncc))


def _rows3(x, n_rows=16):
    hi = x.astype(BF16).astype(F32)
    mid = (x - hi).astype(BF16).astype(F32)
    lo = ((x - hi) - mid).astype(BF16).astype(F32)
    r = lax.broadcasted_iota(jnp.int32, (n_rows, x.shape[1]), 0)
    return jnp.where(r == 0, hi, jnp.where(r == 1, mid, jnp.where(r == 2, lo, 0.0))).astype(BF16)


def _mlstm_kernel(qkt_ref, v_ref, o_ref, grow_ref, bcol_ref, nw_ref, y_ref, hf_ref, hb_ref, *, nc, ncc):
    L = CHUNK
    le, ge, _, _ = _chunk_masks()
    vis = (ge, le)
    row_l = lax.broadcasted_iota(jnp.int32, (L, LANES), 0)
    col_l = lax.broadcasted_iota(jnp.int32, (L, LANES), 1)
    ones_l = jnp.ones((L, LANES), BF16)
    zeros_l = jnp.zeros((L, LANES), BF16)
    stat_rhs = [jnp.concatenate(
        [ones_l, jnp.where((col_l < L) & ((col_l >= row_l) if d == 0 else (col_l <= row_l)), 1.0, 0.0).astype(BF16)],
        axis=1) for d in range(2)]
    ones3 = jnp.where(lax.broadcasted_iota(jnp.int32, (16, LANES), 0) < 3, 1.0, 0.0).astype(BF16)
    tail_rhs = jnp.concatenate([jnp.zeros((16, 2 * LANES), BF16), ones3], axis=1)
    bias_col = bcol_ref[...]
    grp = _group_size(nc)
    h_refs = (hf_ref, hb_ref)

    def sum3(x):
        return x[0:1] + x[1:2] + x[2:3]

    def body(gi, carry):
        C = [carry[0], carry[2]]
        m = [carry[1], carry[3]]
        chains = []
        for j in range(grp):
            it = gi * grp + j
            for d, c in ((0, it), (1, _bwd_chunk(it, nc, ncc))):
                gr = grow_ref[c] + bias_col
                x = qkt_ref[c]
                chains.append(dict(d=d, r0=pl.multiple_of(c * L, L), i_row=gr[2 * d:2 * d + 1, :],
                                   f3=_rows3(_logsigmoid(gr[2 * d + 1:2 * d + 2, :])),
                                   qt=x[:A_DQK] * (A_DQK ** -0.5), kt=x[A_DQK:]))
        for ch in chains:
            ch["v"] = v_ref[pl.ds(ch["r0"], L), :].astype(BF16)
            ch["sqt"] = _dot_tn(ch["kt"], ch["qt"])
        for ch in chains:
            st = jnp.dot(ch["f3"], stat_rhs[ch["d"]], preferred_element_type=F32)
            ch["btot"] = sum3(st[:, :LANES])
            ch["bcum"] = sum3(st[:, LANES:LANES + L])
            ch["a_row"] = ch["i_row"] - ch["bcum"]
        for ch in chains:
            ch["a_col"] = lax.dot_general(_rows3(ch["a_row"]), ones3, (((0,), (0,)), ((), ())),
                                          preferred_element_type=F32)
        for ch in chains:
            d = ch["d"]
            m_new = jnp.maximum(ch["btot"] + m[d], ch["btot"] + jnp.max(ch["a_col"], axis=0, keepdims=True))
            dlog = jnp.where(vis[d], ch["bcum"] + ch["a_col"][:, :L], NEG)
            inter = ch["bcum"] + m[d][:, :L]
            mt = jnp.maximum(inter, jnp.max(dlog, axis=0, keepdims=True))
            ch["st"] = (ch["sqt"] * jnp.exp(dlog - mt)).astype(BF16)
            ch["qe"] = (ch["qt"] * jnp.exp(inter - mt)).astype(BF16)
            ch["e3"] = _rows3(jnp.exp(-mt))
            ch["kw"] = (ch["kt"] * jnp.exp(ch["btot"][:, :L] + ch["a_row"] - m_new[:, :L])).astype(BF16)
            decay = jnp.exp(ch["btot"] + m[d] - m_new)
            ch["decay"] = jnp.concatenate([decay, decay], axis=1)
            m[d] = m_new
        for ch in chains:
            ch["kv"] = jnp.dot(ch["kw"], jnp.concatenate([ch["v"], ones_l], axis=1), preferred_element_type=F32)
            lhs = jnp.concatenate([ch["st"], ch["e3"]], axis=0)
            rhs = jnp.concatenate([jnp.concatenate([ch["v"], ones_l, zeros_l], axis=1), tail_rhs], axis=0)
            ch["intra"] = lax.dot_general(lhs, rhs, (((0,), (0,)), ((), ())), preferred_element_type=F32)
        for ch in chains:
            d = ch["d"]
            ch["inter"] = lax.dot_general(ch["qe"], C[d].astype(BF16), (((0,), (0,)), ((), ())),
                                          preferred_element_type=F32)
            C[d] = ch["decay"] * C[d] + ch["kv"]
        for ch in chains:
            num = ch["intra"][:, :A_DV] + ch["inter"][:, :A_DV]
            den = ch["intra"][:, A_DV:2 * A_DV] + ch["inter"][:, A_DV:]
            h_refs[ch["d"]][pl.ds(ch["r0"], L), :] = num / jnp.maximum(jnp.abs(den), ch["intra"][:, 2 * A_DV:])
        return C[0], m[0], C[1], m[1]

    c0 = jnp.zeros((A_DQK, 2 * LANES), F32)
    m0 = jnp.full((1, LANES), M_INIT, F32)
    lax.fori_loop(0, nc // grp, body, (c0, m0, c0, m0))

    hsum = hf_ref[...] + hb_ref[...]
    hn = hsum * lax.rsqrt(jnp.mean(hsum * hsum, axis=-1, keepdims=True) + EPS)
    y_ref[...] = hn * nw_ref[...] * _sigmoid(o_ref[...])


def _mlstm_call(p, qkt, grow, bcol, nw, l, ncc):
    b, t, _ = p.shape
    nc = t // CHUNK
    blk = lambda base: pl.BlockSpec((None, t, LANES), lambda i, h: (i, 0, base // LANES + h))
    return pl.pallas_call(
        functools.partial(_mlstm_kernel, nc=nc, ncc=ncc),
        grid=(b, A_HEADS),
        in_specs=[pl.BlockSpec((None, None, nc, 2 * A_DQK, CHUNK), lambda i, h: (i, h, 0, 0, 0)),
                  blk(P_AV), blk(P_AO),
                  pl.BlockSpec((None, None, nc, 4, CHUNK), lambda i, h: (i, h, 0, 0, 0)),
                  pl.BlockSpec((None, None, 4, 1), lambda i, h: (l, h, 0, 0)),
                  pl.BlockSpec((None, 1, LANES), lambda i, h: (l, 0, h))],
        out_specs=pl.BlockSpec((None, t, LANES), lambda i, h: (i, 0, h)),
        out_shape=jax.ShapeDtypeStruct((b, t, BRANCH_W), F32),
        scratch_shapes=[pltpu.VMEM((t, LANES), F32), pltpu.VMEM((t, LANES), F32)],
        compiler_params=_params(("arbitrary", "arbitrary")),
        name="mlstm",
    )(qkt, p, p, grow, bcol, nw)


def _headnorm_rope(x, bd, w, cos, sin_signed, lane_half):
    hi, lo = _split(x * x)
    ssum = jnp.dot(hi, bd, preferred_element_type=F32) + jnp.dot(lo, bd, preferred_element_type=F32)
    y = x * lax.rsqrt(ssum * (1.0 / B_DH) + EPS) * w
    swapped = jnp.where(lane_half, pltpu.roll(y, LANES - 16, axis=1), pltpu.roll(y, 16, axis=1))
    return y * cos + swapped * sin_signed


def _gqa_kernel(q_ref, k_ref, v_ref, qw_ref, kw_ref, cosq_ref, sinq_ref, cosk_ref, sink_ref, y_ref,
                kn_ref, vb_ref, *, tc, nqc):
    tq = q_ref.shape[0]
    t = k_ref.shape[0]
    qb = pl.program_id(1)
    r = lax.broadcasted_iota(jnp.int32, (LANES, LANES), 0)
    c = lax.broadcasted_iota(jnp.int32, (LANES, LANES), 1)
    bd = jnp.where(r // B_DH == c // B_DH, 1.0, 0.0).astype(BF16)

    def lane_half(n):
        return (lax.broadcasted_iota(jnp.int32, (n, LANES), 1) % 32) < 16

    @pl.when(qb == 0)
    def _():
        kn = _headnorm_rope(k_ref[...], bd, kw_ref[...], cosk_ref[...], sink_ref[...], lane_half(t))
        kn_ref[...] = kn.astype(BF16)
        v = v_ref[...]
        first = lax.broadcasted_iota(jnp.int32, (t, LANES), 1) < B_DH
        vb_ref[0] = jnp.where(first, v, 1.0).astype(BF16)
        vb_ref[1] = jnp.where(first, pltpu.roll(v, B_DH, axis=1), 1.0).astype(BF16)

    qn = []
    for j in range(B_HEADS * B_DH // LANES):
        sl = slice(j * LANES, (j + 1) * LANES)
        y = _headnorm_rope(q_ref[:, sl], bd, qw_ref[...], cosq_ref[...], sinq_ref[...], lane_half(tq))
        qn.append((y * (B_DH ** -0.5 * LOG2_E)).astype(BF16))

    group = B_HEADS // B_KV_HEADS

    def attend(nk):
        scores = []
        for kvh in range(B_KV_HEADS):
            kh = kn_ref[0:nk, kvh * B_DH:(kvh + 1) * B_DH]
            qs = jnp.concatenate(
                [qn[h // 2][:, (h % 2) * B_DH:(h % 2 + 1) * B_DH]
                 for h in range(kvh * group, (kvh + 1) * group)], axis=0)
            scores.append(lax.dot_general(qs, kh, (((1,), (1,)), ((), ())), preferred_element_type=F32))
        for kvh, s in enumerate(scores):
            p = jnp.exp2(s - jnp.max(s, axis=-1, keepdims=True))
            oe = jnp.dot(p.astype(BF16), vb_ref[kvh, 0:nk, :], preferred_element_type=F32)
            o = oe * pltpu.roll(1.0 / oe, B_DH, axis=1)
            for g in range(group):
                h = kvh * group + g
                y_ref[:, h * B_DH:(h + 1) * B_DH] = o[g * tq:(g + 1) * tq, :B_DH]

    @pl.when(qb < nqc)
    def _():
        attend(tc)

    @pl.when(qb >= nqc)
    def _():
        attend(t)


def _gqa_call(p, qw, kw, cos2, sin2, l, tc):
    b, t, _ = p.shape
    tq = Q_TILE
    row_blk = lambda: pl.BlockSpec((tq, LANES), lambda i, j: (j, 0))
    full = lambda: pl.BlockSpec((t, LANES), lambda i, j: (0, 0))
    return pl.pallas_call(
        functools.partial(_gqa_kernel, tc=tc, nqc=tc // tq),
        grid=(b, t // tq),
        in_specs=[pl.BlockSpec((None, tq, B_HEADS * B_DH), lambda i, j: (i, j, P_BQ // (B_HEADS * B_DH))),
                  pl.BlockSpec((None, t, LANES), lambda i, j: (i, 0, P_BK // LANES)),
                  pl.BlockSpec((None, t, LANES), lambda i, j: (i, 0, P_BV // LANES)),
                  pl.BlockSpec((None, 1, LANES), lambda i, j: (l, 0, 0)),
                  pl.BlockSpec((None, 1, LANES), lambda i, j: (l, 0, 0)),
                  row_blk(), row_blk(), full(), full()],
        out_specs=pl.BlockSpec((None, tq, BRANCH_W), lambda i, j: (i, j, 0)),
        out_shape=jax.ShapeDtypeStruct((b, t, BRANCH_W), F32),
        scratch_shapes=[pltpu.VMEM((t, LANES), BF16), pltpu.VMEM((B_KV_HEADS, t, LANES), BF16)],
        compiler_params=_params(("arbitrary", "arbitrary")),
        name="gqa",
    )(p, p, p, qw, kw, cos2, sin2, cos2, sin2)


def _group_size(nc):
    return max(g for g in range(1, SCAN_GROUP + 1) if nc % g == 0)


def _gdn_kernel(q_ref, k_ref, v_ref, z_ref, cwq_ref, cwk_ref, cwv_ref, gcol_ref, grow_ref, par_ref,
                nw_ref, y_ref, qn_ref, kn_ref, vn_ref, n_ref, kq_ref, egl_ref, o_ref, *, nc, ncc):
    L = CHUNK
    t = q_ref.shape[0]
    tc = ncc * L
    le, ge, lt, gt = _chunk_masks()
    par = par_ref[...]

    rows = lax.broadcasted_iota(jnp.int32, (t, 1), 0)
    has_prev = jnp.logical_and(rows != 0, rows != tc)
    has_next = jnp.logical_and(rows != tc - 1, rows != t - 1)

    def conv_silu(x_ref, cw_ref):
        x = x_ref[...]
        cw = cw_ref[...]
        prev = jnp.where(has_prev, pltpu.roll(x, 1, axis=0), 0.0)
        nxt = jnp.where(has_next, pltpu.roll(x, t - 1, axis=0), 0.0)
        return _silu(prev * cw[0:1] + x * cw[1:2] + nxt * cw[2:3])

    def l2n(x):
        return x * lax.rsqrt(jnp.sum(x * x, axis=-1, keepdims=True) + EPS)

    qn_ref[...] = l2n(conv_silu(q_ref, cwq_ref)) * (C_DK ** -0.5)
    kn_ref[...] = l2n(conv_silu(k_ref, cwk_ref))
    vn_ref[...] = conv_silu(v_ref, cwv_ref)

    grp = _group_size(nc)
    ng = nc // grp
    eye = jnp.where(le & ge, 1.0, 0.0).astype(F32)
    rowi = lax.broadcasted_iota(jnp.int32, (L, L), 0)
    coli = lax.broadcasted_iota(jnp.int32, (L, L), 1)
    blk = {b_: (rowi // (2 * b_) == coli // (2 * b_)) & (rowi // b_ != coli // b_) for b_ in (1, 2, 4, 8, 16, 32)}

    def prep_stages(gi):
        chains = []
        for j in range(grp):
            it = gi * grp + j
            for d, c in ((0, it), (1, _bwd_chunk(it, nc, ncc))):
                r0 = pl.multiple_of(c * L, L)
                chains.append(dict(it=it, d=d, c=c, r0=r0, q=qn_ref[pl.ds(r0, L), :],
                                   k=kn_ref[pl.ds(r0, L), :], v=vn_ref[pl.ds(r0, L), :]))
        for ch in chains:
            ch["kkqk"] = _dot_nt(jnp.concatenate([ch["k"], ch["q"]], axis=0), ch["k"])
        yield
        for ch in chains:
            d = ch["d"]
            gc = gcol_ref[pl.ds(ch["r0"], L), :]
            gr = grow_ref[ch["c"]]
            mask_in, mask_t, strict = (le, ge, lt) if d == 0 else (ge, le, gt)
            neg_rate = -jnp.exp(par[:, d:d + 1])
            g_col = neg_rate * _softplus(gc[:, d:d + 1] + par[:, 2 + d:3 + d])
            g_row = neg_rate * _softplus(gr[d:d + 1, :] + par[:, 2 + d:3 + d])
            beta = _sigmoid(gc[:, 2 + d:3 + d])
            G_col = jnp.sum(jnp.where(mask_in, g_row, 0.0), axis=1, keepdims=True)
            G_row = jnp.sum(jnp.where(mask_t, g_col, 0.0), axis=0, keepdims=True)
            dec = jnp.exp(jnp.where(mask_in, G_col - G_row, NEG))
            g_last = jnp.sum(g_row, axis=1, keepdims=True)
            ch.update(beta=beta, dec=dec, eG=jnp.exp(G_col), kscale=jnp.exp(g_last - G_col),
                      egl=jnp.exp(g_last), x=-jnp.where(strict, beta * ch["kkqk"][:L] * dec, 0.0))
        invs = [eye + jnp.where(blk[1], ch["x"], 0.0) for ch in chains]
        bsz = 2
        while bsz < L:
            offs = [jnp.where(blk[bsz], ch["x"], 0.0) for ch in chains]
            tmps = [_dot(off, inv) for off, inv in zip(offs, invs)]
            yield
            invs = [inv + _dot(inv, tmp) for inv, tmp in zip(invs, tmps)]
            yield
            bsz *= 2
        uws = []
        for ch, inv in zip(chains, invs):
            rhs = jnp.concatenate([ch["v"] * ch["beta"], ch["k"] * (ch["beta"] * ch["eG"])], axis=1)
            uws.append(_dot(inv, rhs))
        yield
        kns = [_dot_tn(ch["k"] * ch["kscale"], uw) for ch, uw in zip(chains, uws)]
        yield
        qos = [_dot(ch["kkqk"][L:] * ch["dec"], uw) for ch, uw in zip(chains, uws)]
        yield
        for ch, kn, qo in zip(chains, kns, qos):
            d, it = ch["d"], ch["it"]
            n_ref[d, it] = kn[:, :C_DV]
            kq_ref[d, it, 0:C_DK, :] = kn[:, C_DV:].astype(BF16)
            kq_ref[d, it, C_DK:C_DK + L, :] = (ch["q"] * ch["eG"] - qo[:, C_DV:]).astype(BF16)
            o_ref[d, pl.ds(ch["r0"], L), :] = qo[:, :C_DV]
            egl_ref[d, it] = jnp.broadcast_to(ch["egl"], (8, LANES))

    def recur(it, S):
        out = []
        for d, c in ((0, it), (1, _bwd_chunk(it, nc, ncc))):
            r0 = pl.multiple_of(c * L, L)
            ks_qs = jnp.dot(kq_ref[d, it], S[d].astype(BF16), preferred_element_type=F32)
            o_ref[d, pl.ds(r0, L), :] += ks_qs[C_DK:]
            out.append(S[d] * egl_ref[d, it][0:1, 0:1] + (n_ref[d, it] - ks_qs[:C_DK]))
        return out

    for _ in prep_stages(0):
        pass

    def body(gi, carry):
        S = list(carry)
        stages = prep_stages(gi)
        for j in range(grp):
            next(stages, None)
            S = recur((gi - 1) * grp + j, S)
        for _ in stages:
            pass
        return tuple(S)

    s0 = jnp.zeros((C_DK, C_DV), F32)
    S = list(lax.fori_loop(1, ng, body, (s0, s0)))
    for j in range(grp):
        S = recur((ng - 1) * grp + j, S)

    osum = o_ref[0] + o_ref[1]
    on = osum * lax.rsqrt(jnp.mean(osum * osum, axis=-1, keepdims=True) + EPS) * nw_ref[...]
    y_ref[...] = on * _silu(z_ref[...])


def _gdn_call(p, cw, gcol, grow, par, nw, l, ncc):
    b, t, _ = p.shape
    nc = t // CHUNK
    blk = lambda base: pl.BlockSpec((None, t, LANES), lambda i, h: (i, 0, base // LANES + h))
    cwb = lambda off: pl.BlockSpec((None, 3, LANES), lambda i, h: (l, 0, off + h))
    big = lambda dt: pltpu.VMEM((2, t, LANES), dt)
    return pl.pallas_call(
        functools.partial(_gdn_kernel, nc=nc, ncc=ncc),
        grid=(b, C_HEADS),
        in_specs=[blk(P_CQ), blk(P_CK), blk(P_CV), blk(P_CZ), cwb(0), cwb(C_HEADS), cwb(2 * C_HEADS),
                  pl.BlockSpec((None, None, t, 4), lambda i, h: (i, h, 0, 0)),
                  pl.BlockSpec((None, None, nc, 4, CHUNK), lambda i, h: (i, h, 0, 0, 0)),
                  pl.BlockSpec((None, None, 1, 4), lambda i, h: (l, h, 0, 0)),
                  pl.BlockSpec((None, 1, LANES), lambda i, h: (l, 0, 0))],
        out_specs=pl.BlockSpec((None, t, LANES), lambda i, h: (i, 0, h)),
        out_shape=jax.ShapeDtypeStruct((b, t, BRANCH_W), F32),
        scratch_shapes=[pltpu.VMEM((t, LANES), F32), pltpu.VMEM((t, LANES), F32),
                        pltpu.VMEM((t, LANES), F32),
                        pltpu.VMEM((2, nc, C_DK, C_DV), F32),
                        pltpu.VMEM((2, nc, C_DK + CHUNK, C_DV), BF16),
                        pltpu.VMEM((2, nc, 8, LANES), F32),
                        big(F32)],
        compiler_params=_params(("arbitrary", "arbitrary")),
        name="gdn",
    )(p, p, p, p, cw, cw, cw, gcol, grow, par, nw)


def _merge_kernel(ya_ref, yb_ref, yc_ref, ga_ref, gb_ref, gc_ref, h_ref, mod_ref, wb_ref, wo_ref, o_ref):
    y = (_sigmoid(ga_ref[...]) * _dot(ya_ref[...], wb_ref[0])
         + _sigmoid(gb_ref[...]) * _dot(yb_ref[...], wb_ref[1])
         + _sigmoid(gc_ref[...]) * _dot(yc_ref[...], wb_ref[2]))
    o_ref[...] = h_ref[...] + mod_ref[2:3, :] * _dot(y, wo_ref[...])


def _mod_row(j, nct):
    return 1 if nct == 0 else jnp.minimum(j // nct, 1)


def _merge_call(ya, yb, yc, p, h, mod, wb, wo, l, tm, nct, latent_only):
    b, t, d = h.shape
    skip = nct if latent_only else 0
    yblk = lambda: pl.BlockSpec((None, tm, BRANCH_W), lambda i, j: (i, j + skip, 0))
    gblk = lambda g: pl.BlockSpec((None, tm, d), lambda i, j: (i, j + skip, g))
    return pl.pallas_call(
        _merge_kernel,
        grid=(b, t // tm - skip),
        in_specs=[yblk(), yblk(), yblk(), gblk(0), gblk(1), gblk(2),
                  pl.BlockSpec((None, tm, d), lambda i, j: (i, j + skip, 0)),
                  pl.BlockSpec((None, None, None, 6, d), lambda i, j: (l, i, _mod_row(j + skip, nct), 0, 0)),
                  _resident((None, 3, BRANCH_W, d), lambda i, j: (l, 0, 0, 0)),
                  _resident((None, d, d), lambda i, j: (l, 0, 0))],
        out_specs=pl.BlockSpec((None, tm, d), lambda i, j: (i, j, 0)),
        out_shape=jax.ShapeDtypeStruct((b, t - skip * tm, d), F32),
        compiler_params=_params(("arbitrary", "arbitrary")),
        name="merge",
    )(ya, yb, yc, p, p, p, h, mod, wb, wo)


def _ffn_kernel(h_ref, hp_ref, hn_ref, mod_ref, nw_ref, wup_ref, cw_ref, wdn_ref, o_ref, acc_ref,
                *, nct, ntiles):
    tm = h_ref.shape[0]
    j = pl.program_id(1)
    mod = mod_ref[...]
    nw = nw_ref[...]
    h = h_ref[...]
    xn = _modnorm(h, nw, mod[3:4], mod[4:5]).astype(BF16)
    halo = jnp.concatenate([hp_ref[...], hn_ref[...]], axis=0)
    xh = _modnorm(halo, nw, mod[3:4], mod[4:5]).astype(BF16)
    has_prev = jnp.logical_and(j != 0, j != nct).astype(F32)
    has_next = jnp.logical_and(j != nct - 1, j != ntiles - 1).astype(F32)
    rows = lax.broadcasted_iota(jnp.int32, (tm, 1), 0)
    first = rows == 0
    last = rows == tm - 1

    def cols(kind, jf):
        return slice(kind * D_FF + jf * FFN_TF, kind * D_FF + (jf + 1) * FFN_TF)

    def up(jf):
        return [(jnp.dot(xn, wup_ref[:, cols(kind, jf)], preferred_element_type=F32),
                 jnp.dot(xh, wup_ref[:, cols(kind, jf)], preferred_element_type=F32))
                for kind in range(2)]

    def conv(kind, jf, u, uh):
        cw = cw_ref[:, cols(kind, jf)]
        prev = jnp.where(first, uh[7:8] * has_prev, pltpu.roll(u, 1, axis=0))
        nxt = jnp.where(last, uh[8:9] * has_next, pltpu.roll(u, tm - 1, axis=0))
        return prev * cw[0:1] + u * cw[1:2] + nxt * cw[2:3]

    nj = D_FF // FFN_TF
    ahead = [up(jf) for jf in range(min(FFN_AHEAD, nj))]
    for jf in range(nj):
        if jf + FFN_AHEAD < nj:
            ahead.append(up(jf + FFN_AHEAD))
        cur = ahead.pop(0)
        act = (conv(0, jf, *cur[0]) * _silu(conv(1, jf, *cur[1]))).astype(BF16)
        part = jnp.dot(act, wdn_ref[jf], preferred_element_type=F32)
        if jf == 0:
            acc_ref[...] = part
        else:
            acc_ref[...] += part
    o_ref[...] = h + mod[5:6] * acc_ref[...]


def _ffn_call(h, mod, nw, wup, cw, wdn, l, tm, nct):
    b, t, d = h.shape
    ntiles = t // tm
    hb = tm // 8
    return pl.pallas_call(
        functools.partial(_ffn_kernel, nct=nct, ntiles=ntiles),
        grid=(b, ntiles),
        in_specs=[pl.BlockSpec((None, tm, d), lambda i, j: (i, j, 0)),
                  pl.BlockSpec((None, 8, d), lambda i, j: (i, jnp.maximum(j * hb - 1, 0), 0)),
                  pl.BlockSpec((None, 8, d), lambda i, j: (i, jnp.minimum((j + 1) * hb, t // 8 - 1), 0)),
                  pl.BlockSpec((None, None, None, 6, d), lambda i, j: (l, i, _mod_row(j, nct), 0, 0)),
                  pl.BlockSpec((None, 1, d), lambda i, j: (l, 0, 0)),
                  _resident((None,) + wup.shape[1:], lambda i, j: (l, 0, 0)),
                  _resident((None,) + cw.shape[1:], lambda i, j: (l, 0, 0)),
                  _resident((None,) + wdn.shape[1:], lambda i, j: (l, 0, 0, 0))],
        out_specs=pl.BlockSpec((None, tm, d), lambda i, j: (i, j, 0)),
        out_shape=jax.ShapeDtypeStruct((b, t, d), F32),
        scratch_shapes=[pltpu.VMEM((tm, d), F32)],
        compiler_params=_params(("arbitrary", "arbitrary")),
        name="ffn",
    )(h, h, h, mod, nw, wup, cw, wdn)


def _rope_tables(tc, tl):
    rows = tl // GRID_W
    row = jnp.repeat(jnp.arange(rows, dtype=F32), GRID_W)
    col = jnp.tile(jnp.arange(GRID_W, dtype=F32), rows)
    n_freq = B_DH // 4
    inv = ROPE_BASE ** (-jnp.arange(n_freq, dtype=F32) / n_freq)
    ang_r = row[:, None] * inv
    ang_c = col[:, None] * inv
    cos = jnp.concatenate([jnp.cos(ang_r)] * 2 + [jnp.cos(ang_c)] * 2, axis=1)
    sin = jnp.concatenate([-jnp.sin(ang_r), jnp.sin(ang_r), -jnp.sin(ang_c), jnp.sin(ang_c)], axis=1)
    cos = jnp.concatenate([jnp.ones((tc, B_DH), F32), cos], axis=0)
    sin = jnp.concatenate([jnp.zeros((tc, B_DH), F32), sin], axis=0)
    return jnp.tile(cos, (1, 2)), jnp.tile(sin, (1, 2))


def _pack_w_in(w_in):
    depth, d, _ = w_in.shape
    w = w_in.astype(BF16)
    o = 0
    parts = {}
    for name, width in (("aq", 256), ("ak", 256), ("av", 512), ("ao", 512), ("ag", 16),
                        ("bq", 512), ("bk", 128), ("bv", 128),
                        ("cq", 512), ("ck", 512), ("cv", 512), ("cz", 512), ("ca", 8), ("cb", 8),
                        ("gate", 3 * D_MODEL)):
        parts[name] = w[:, :, o:o + width]
        o += width
    aqk = jnp.concatenate([parts["aq"].reshape(depth, d, A_HEADS, A_DQK),
                           parts["ak"].reshape(depth, d, A_HEADS, A_DQK)], axis=3).reshape(depth, d, 512)
    small = jnp.concatenate([parts["ag"], parts["ca"], parts["cb"]], axis=2)
    pad = jnp.zeros((depth, d, W_COLS - W_SMALL - small.shape[2]), BF16)
    cols = [parts["gate"], parts["av"], parts["ao"], parts["bq"], parts["bk"], parts["bv"],
            parts["cq"], parts["ck"], parts["cv"], parts["cz"], aqk, small, pad]
    return jnp.concatenate(cols, axis=2)


def _gate_layouts(x, heads, kinds):
    b, t, _ = x.shape
    g = x.reshape(b, t, kinds, heads)
    col = g.transpose(0, 3, 1, 2)
    row = g.reshape(b, t // CHUNK, CHUNK, kinds, heads).transpose(0, 4, 1, 3, 2)
    return col, row


def kernel(x, c, ctx, c_ctx, norm1_w, norm2_w, ada_w, ada_b, w_in, a_gate_b, a_norm_w, b_qnorm_w,
           b_knorm_w, c_conv_w, c_a_log, c_dt_bias, c_norm_w, w_branch, w_out, w_up, ffn_conv_w, w_down):
    b, tl, d = x.shape
    tc = ctx.shape[1]
    depth = w_in.shape[0]
    t = tc + tl
    tm = 256 if (tc % 256 == 0 and tl % 256 == 0) else 128
    nct = tc // tm
    ncc = tc // CHUNK
    nj = D_FF // FFN_TF

    cc = jnp.zeros((16, d), F32).at[:b].set(c).at[b].set(c_ctx)
    mods = _ada_call(cc, ada_w, ada_b).reshape(depth, 16, 6, d)
    mod = jnp.stack([jnp.broadcast_to(mods[:, b][:, None], (depth, b, 6, d)), mods[:, :b]], axis=2)

    w_in_p = _pack_w_in(w_in)
    wb = w_branch.astype(BF16)
    wo = w_out.astype(BF16)
    wup = w_up.astype(BF16)
    wdn = w_down.astype(BF16).reshape(depth, nj, FFN_TF, d)
    cos2, sin2 = _rope_tables(tc, tl)
    qw2 = jnp.tile(b_qnorm_w, (1, 2))[:, None]
    kw2 = jnp.tile(b_knorm_w, (1, 2))[:, None]
    a_bcol = a_gate_b.transpose(0, 2, 1)[:, :, :, None]
    c_par = jnp.concatenate([c_a_log, c_dt_bias], axis=1).transpose(0, 2, 1)[:, :, None, :]

    h = jnp.concatenate([ctx, x], axis=1)
    for l in range(depth):
        p, qkt, small = _proj_call(h, mod, norm1_w[:, None], w_in_p, l, tm, nct)
        _, a_grow = _gate_layouts(small[:, :, :16], A_HEADS, 4)
        c_gcol, c_grow = _gate_layouts(small[:, :, 16:32], C_HEADS, 4)
        ya = _mlstm_call(p, qkt, a_grow, a_bcol, a_norm_w[:, None], l, ncc)
        yb = _gqa_call(p, qw2, kw2, cos2, sin2, l, tc)
        yc = _gdn_call(p, c_conv_w, c_gcol, c_grow, c_par, c_norm_w[:, None], l, ncc)
        last = l == depth - 1
        h = _merge_call(ya, yb, yc, p, h, mod, wb, wo, l, tm, nct, latent_only=last)
        h = _ffn_call(h, mod, norm2_w[:, None], wup, ffn_conv_w, wdn, l, tm, 0 if last else nct)
    return h
```

```python
import functools

import jax
import jax.numpy as jnp
from jax import lax
from jax.experimental import pallas as pl
from jax.experimental.pallas import tpu as pltpu

F32 = jnp.float32
BF16 = jnp.bfloat16

D_MODEL = 1024
GRID_W = 64
A_HEADS, A_DQK, A_DV = 4, 64, 128
B_HEADS, B_KV_HEADS, B_DH = 8, 2, 64
C_HEADS, C_DK, C_DV = 4, 128, 128
BRANCH_W = 512
D_FF = 2816
CHUNK = 64
ROPE_BASE = 10000.0
EPS = 1e-6
M_INIT = -1e30
LOG2_E = 1.4426950408889634
NEG = -1e30

LANES = 128
W_GATES = 3 * D_MODEL
P_AV = 0
P_AO = 512
P_BQ = 1024
P_BK = 1536
P_BV = 1664
P_CQ = 1792
P_CK = 2304
P_CV = 2816
P_CZ = 3328
P_COLS = 3840
W_AQK = 6912
W_SMALL = 7424
W_COLS = 7552
PROJ_TN = 512
FFN_TF = 256
FFN_AHEAD = 3
FFN_DOWN_GROUP = 4
Q_TILE = 128
SCAN_GROUP = 12
VMEM_LIMIT = 56 * 1024 * 1024


def _dot(a, b):
    return jnp.dot(a.astype(BF16), b.astype(BF16), preferred_element_type=F32)


def _dot_nt(a, b):
    return lax.dot_general(a.astype(BF16), b.astype(BF16), (((1,), (1,)), ((), ())),
                           preferred_element_type=F32)


def _dot_tn(a, b):
    return lax.dot_general(a.astype(BF16), b.astype(BF16), (((0,), (0,)), ((), ())),
                           preferred_element_type=F32)


def _split(a):
    hi = a.astype(BF16)
    lo = (a - hi.astype(F32)).astype(BF16)
    return hi, lo


def _dot3(a, b):
    ah, al = _split(a)
    bh, bl = _split(b)
    d = functools.partial(jnp.dot, preferred_element_type=F32)
    return d(ah, bh) + (d(al, bh) + d(ah, bl))


def _sigmoid(x):
    return 1.0 / (1.0 + jnp.exp(-x))


def _silu(x):
    return x * _sigmoid(x)


def _softplus(x):
    return jnp.maximum(x, 0.0) + jnp.log1p(jnp.exp(-jnp.abs(x)))


def _logsigmoid(x):
    return -_softplus(-x)


def _modnorm(x, w, shift, scale):
    y = x * lax.rsqrt(jnp.mean(x * x, axis=-1, keepdims=True) + EPS)
    return (y * w) * (1.0 + scale) + shift


def _params(sem, vmem=VMEM_LIMIT):
    return pltpu.CompilerParams(dimension_semantics=sem, vmem_limit_bytes=vmem)


def _resident(shape, index_map):
    return pl.BlockSpec(shape, index_map, pipeline_mode=pl.Buffered(1))


def _ada_kernel(c_ref, w_ref, b_ref, o_ref):
    o_ref[...] = _dot3(_silu(c_ref[...]), w_ref[...]) + b_ref[...]


def _ada_call(cc, ada_w, ada_b):
    depth, d, n = ada_w.shape
    tn = 1536
    return pl.pallas_call(
        _ada_kernel,
        grid=(depth, n // tn),
        in_specs=[pl.BlockSpec(cc.shape, lambda l, j: (0, 0)),
                  pl.BlockSpec((None, d, tn), lambda l, j: (l, 0, j)),
                  pl.BlockSpec((None, 1, tn), lambda l, j: (l, 0, j))],
        out_specs=pl.BlockSpec((None, cc.shape[0], tn), lambda l, j: (l, 0, j)),
        out_shape=jax.ShapeDtypeStruct((depth, cc.shape[0], n), F32),
        compiler_params=_params(("arbitrary", "arbitrary")),
        name="ada",
    )(cc, ada_w, ada_b.reshape(depth, 1, n))


def _proj_kernel(h_ref, mod_ref, nw_ref, w_ref, g_ref, p_ref, qkt_ref, small_ref):
    tm = h_ref.shape[0]
    mod = mod_ref[...]
    xn = _modnorm(h_ref[...], nw_ref[...], mod[0:1], mod[1:2]).astype(BF16)
    for c0 in range(0, W_GATES, PROJ_TN):
        sl = slice(c0, c0 + PROJ_TN)
        g_ref[:, sl] = jnp.dot(xn, w_ref[:, sl], preferred_element_type=F32).astype(BF16)
    for c0 in range(0, P_COLS, PROJ_TN):
        c1 = min(c0 + PROJ_TN, P_COLS)
        p_ref[:, c0:c1] = jnp.dot(xn, w_ref[:, W_GATES + c0:W_GATES + c1], preferred_element_type=F32)
    qk = jnp.dot(xn, w_ref[:, W_AQK:W_SMALL], preferred_element_type=F32)
    for hd in range(A_HEADS):
        for c in range(tm // CHUNK):
            qkt_ref[hd, c] = qk[c * CHUNK:(c + 1) * CHUNK, hd * LANES:(hd + 1) * LANES].T
    small_ref[...] = jnp.dot(xn, w_ref[:, W_SMALL:W_COLS], preferred_element_type=F32)


def _proj_call(h, mod, nw, w, l, tm, nct):
    b, t, d = h.shape
    cpt = tm // CHUNK
    return pl.pallas_call(
        _proj_kernel,
        grid=(b, t // tm),
        in_specs=[pl.BlockSpec((None, tm, d), lambda i, j: (i, j, 0)),
                  pl.BlockSpec((None, None, None, 6, d), lambda i, j: (l, i, _mod_row(j, nct), 0, 0)),
                  pl.BlockSpec((None, 1, d), lambda i, j: (l, 0, 0)),
                  _resident((None, d, W_COLS), lambda i, j: (l, 0, 0))],
        out_specs=[pl.BlockSpec((None, tm, W_GATES), lambda i, j: (i, j, 0)),
                   pl.BlockSpec((None, tm, P_COLS), lambda i, j: (i, j, 0)),
                   pl.BlockSpec((None, A_HEADS, cpt, 2 * A_DQK, CHUNK), lambda i, j: (i, 0, j, 0, 0)),
                   pl.BlockSpec((None, tm, LANES), lambda i, j: (i, j, 0))],
        out_shape=[jax.ShapeDtypeStruct((b, t, W_GATES), BF16),
                   jax.ShapeDtypeStruct((b, t, P_COLS), F32),
                   jax.ShapeDtypeStruct((b, A_HEADS, t // CHUNK, 2 * A_DQK, CHUNK), F32),
                   jax.ShapeDtypeStruct((b, t, LANES), F32)],
        compiler_params=_params(("arbitrary", "arbitrary")),
        name="proj_in",
    )(h, mod, nw, w)


def _chunk_masks():
    row = lax.broadcasted_iota(jnp.int32, (CHUNK, CHUNK), 0)
    col = lax.broadcasted_iota(jnp.int32, (CHUNK, CHUNK), 1)
    return col <= row, col >= row, col < row, col > row


def _bwd_chunk(it, nc, ncc):
    return jnp.where(it < ncc, ncc - 1 - it, nc - 1 - (it - ncc))


def _rows3(x, n_rows=16):
    hi = x.astype(BF16).astype(F32)
    mid = (x - hi).astype(BF16).astype(F32)
    lo = ((x - hi) - mid).astype(BF16).astype(F32)
    r = lax.broadcasted_iota(jnp.int32, (n_rows, x.shape[1]), 0)
    return jnp.where(r == 0, hi, jnp.where(r == 1, mid, jnp.where(r == 2, lo, 0.0))).astype(BF16)


def _mlstm_kernel(qkt_ref, v_ref, o_ref, grow_ref, bcol_ref, nw_ref, y_ref, hf_ref, hb_ref, *, nc, ncc):
    L = CHUNK
    le, ge, _, _ = _chunk_masks()
    vis = (ge, le)
    row_l = lax.broadcasted_iota(jnp.int32, (L, LANES), 0)
    col_l = lax.broadcasted_iota(jnp.int32, (L, LANES), 1)
    ones_l = jnp.ones((L, LANES), BF16)
    zeros_l = jnp.zeros((L, LANES), BF16)
    stat_rhs = [jnp.concatenate(
        [ones_l, jnp.where((col_l < L) & ((col_l >= row_l) if d == 0 else (col_l <= row_l)), 1.0, 0.0).astype(BF16)],
        axis=1) for d in range(2)]
    ones3 = jnp.where(lax.broadcasted_iota(jnp.int32, (16, LANES), 0) < 3, 1.0, 0.0).astype(BF16)
    tail_rhs = jnp.concatenate([jnp.zeros((16, 2 * LANES), BF16), ones3], axis=1)
    bias_col = bcol_ref[...]
    grp = _group_size(nc)
    h_refs = (hf_ref, hb_ref)

    def sum3(x):
        return x[0:1] + x[1:2] + x[2:3]

    def body(gi, carry):
        C = [carry[0], carry[2]]
        m = [carry[1], carry[3]]
        chains = []
        for j in range(grp):
            it = gi * grp + j
            for d, c in ((0, it), (1, _bwd_chunk(it, nc, ncc))):
                gr = grow_ref[c] + bias_col
                x = qkt_ref[c]
                chains.append(dict(d=d, r0=pl.multiple_of(c * L, L), i_row=gr[2 * d:2 * d + 1, :],
                                   f3=_rows3(_logsigmoid(gr[2 * d + 1:2 * d + 2, :])),
                                   qt=x[:A_DQK] * (A_DQK ** -0.5), kt=x[A_DQK:]))
        for ch in chains:
            ch["v"] = v_ref[pl.ds(ch["r0"], L), :].astype(BF16)
            ch["sqt"] = _dot_tn(ch["kt"], ch["qt"])
        for ch in chains:
            st = jnp.dot(ch["f3"], stat_rhs[ch["d"]], preferred_element_type=F32)
            ch["btot"] = sum3(st[:, :LANES])
            ch["bcum"] = sum3(st[:, LANES:LANES + L])
            ch["a_row"] = ch["i_row"] - ch["bcum"]
        for ch in chains:
            ch["a_col"] = lax.dot_general(_rows3(ch["a_row"]), ones3, (((0,), (0,)), ((), ())),
                                          preferred_element_type=F32)
        for ch in chains:
            d = ch["d"]
            m_new = jnp.maximum(ch["btot"] + m[d], ch["btot"] + jnp.max(ch["a_col"], axis=0, keepdims=True))
            dlog = jnp.where(vis[d], ch["bcum"] + ch["a_col"][:, :L], NEG)
            inter = ch["bcum"] + m[d][:, :L]
            mt = jnp.maximum(inter, jnp.max(dlog, axis=0, keepdims=True))
            ch["st"] = (ch["sqt"] * jnp.exp(dlog - mt)).astype(BF16)
            ch["qe"] = (ch["qt"] * jnp.exp(inter - mt)).astype(BF16)
            ch["e3"] = _rows3(jnp.exp(-mt))
            ch["kw"] = (ch["kt"] * jnp.exp(ch["btot"][:, :L] + ch["a_row"] - m_new[:, :L])).astype(BF16)
            decay = jnp.exp(ch["btot"] + m[d] - m_new)
            ch["decay"] = jnp.concatenate([decay, decay], axis=1)
            m[d] = m_new
        for ch in chains:
            ch["kv"] = jnp.dot(ch["kw"], jnp.concatenate([ch["v"], ones_l], axis=1), preferred_element_type=F32)
            lhs = jnp.concatenate([ch["st"], ch["e3"]], axis=0)
            rhs = jnp.concatenate([jnp.concatenate([ch["v"], ones_l, zeros_l], axis=1), tail_rhs], axis=0)
            ch["intra"] = lax.dot_general(lhs, rhs, (((0,), (0,)), ((), ())), preferred_element_type=F32)
        for ch in chains:
            d = ch["d"]
            ch["inter"] = lax.dot_general(ch["qe"], C[d].astype(BF16), (((0,), (0,)), ((), ())),
                                          preferred_element_type=F32)
            C[d] = ch["decay"] * C[d] + ch["kv"]
        for ch in chains:
            num = ch["intra"][:, :A_DV] + ch["inter"][:, :A_DV]
            den = ch["intra"][:, A_DV:2 * A_DV] + ch["inter"][:, A_DV:]
            h_refs[ch["d"]][pl.ds(ch["r0"], L), :] = num / jnp.maximum(jnp.abs(den), ch["intra"][:, 2 * A_DV:])
        return C[0], m[0], C[1], m[1]

    c0 = jnp.zeros((A_DQK, 2 * LANES), F32)
    m0 = jnp.full((1, LANES), M_INIT, F32)
    lax.fori_loop(0, nc // grp, body, (c0, m0, c0, m0))

    hsum = hf_ref[...] + hb_ref[...]
    hn = hsum * lax.rsqrt(jnp.mean(hsum * hsum, axis=-1, keepdims=True) + EPS)
    y_ref[...] = (hn * nw_ref[...] * _sigmoid(o_ref[...])).astype(y_ref.dtype)


def _mlstm_call(p, qkt, grow, bcol, nw, l, ncc):
    b, t, _ = p.shape
    nc = t // CHUNK
    blk = lambda base: pl.BlockSpec((None, t, LANES), lambda i, h: (i, 0, base // LANES + h))
    return pl.pallas_call(
        functools.partial(_mlstm_kernel, nc=nc, ncc=ncc),
        grid=(b, A_HEADS),
        in_specs=[pl.BlockSpec((None, None, nc, 2 * A_DQK, CHUNK), lambda i, h: (i, h, 0, 0, 0)),
                  blk(P_AV), blk(P_AO),
                  pl.BlockSpec((None, None, nc, 4, CHUNK), lambda i, h: (i, h, 0, 0, 0)),
                  pl.BlockSpec((None, None, 4, 1), lambda i, h: (l, h, 0, 0)),
                  pl.BlockSpec((None, 1, LANES), lambda i, h: (l, 0, h))],
        out_specs=pl.BlockSpec((None, t, LANES), lambda i, h: (i, 0, h)),
        out_shape=jax.ShapeDtypeStruct((b, t, BRANCH_W), BF16),
        scratch_shapes=[pltpu.VMEM((t, LANES), F32), pltpu.VMEM((t, LANES), F32)],
        compiler_params=_params(("arbitrary", "arbitrary")),
        name="mlstm",
    )(qkt, p, p, grow, bcol, nw)


def _headnorm_rope(x, bd, w, cos, sin_signed, lane_half):
    hi, lo = _split(x * x)
    ssum = jnp.dot(hi, bd, preferred_element_type=F32) + jnp.dot(lo, bd, preferred_element_type=F32)
    y = x * lax.rsqrt(ssum * (1.0 / B_DH) + EPS) * w
    swapped = jnp.where(lane_half, pltpu.roll(y, LANES - 16, axis=1), pltpu.roll(y, 16, axis=1))
    return y * cos + swapped * sin_signed


def _gqa_kernel(q_ref, k_ref, v_ref, qw_ref, kw_ref, cosq_ref, sinq_ref, cosk_ref, sink_ref, y_ref,
                kn_ref, vb_ref, *, tc, nqc):
    tq = q_ref.shape[0]
    t = k_ref.shape[0]
    qb = pl.program_id(1)
    r = lax.broadcasted_iota(jnp.int32, (LANES, LANES), 0)
    c = lax.broadcasted_iota(jnp.int32, (LANES, LANES), 1)
    bd = jnp.where(r // B_DH == c // B_DH, 1.0, 0.0).astype(BF16)

    def lane_half(n):
        return (lax.broadcasted_iota(jnp.int32, (n, LANES), 1) % 32) < 16

    @pl.when(qb == 0)
    def _():
        kn = _headnorm_rope(k_ref[...], bd, kw_ref[...], cosk_ref[...], sink_ref[...], lane_half(t))
        kn_ref[...] = kn.astype(BF16)
        v = v_ref[...]
        first = lax.broadcasted_iota(jnp.int32, (t, LANES), 1) < B_DH
        vb_ref[0] = jnp.where(first, v, 1.0).astype(BF16)
        vb_ref[1] = jnp.where(first, pltpu.roll(v, B_DH, axis=1), 1.0).astype(BF16)

    qn = []
    for j in range(B_HEADS * B_DH // LANES):
        sl = slice(j * LANES, (j + 1) * LANES)
        y = _headnorm_rope(q_ref[:, sl], bd, qw_ref[...], cosq_ref[...], sinq_ref[...], lane_half(tq))
        qn.append((y * (B_DH ** -0.5 * LOG2_E)).astype(BF16))

    group = B_HEADS // B_KV_HEADS

    def attend(nk):
        scores = []
        for kvh in range(B_KV_HEADS):
            kh = kn_ref[0:nk, kvh * B_DH:(kvh + 1) * B_DH]
            qs = jnp.concatenate(
                [qn[h // 2][:, (h % 2) * B_DH:(h % 2 + 1) * B_DH]
                 for h in range(kvh * group, (kvh + 1) * group)], axis=0)
            scores.append(lax.dot_general(qs, kh, (((1,), (1,)), ((), ())), preferred_element_type=F32))
        for kvh, s in enumerate(scores):
            p = jnp.exp2(s - jnp.max(s, axis=-1, keepdims=True))
            oe = jnp.dot(p.astype(BF16), vb_ref[kvh, 0:nk, :], preferred_element_type=F32)
            o = oe * pltpu.roll(1.0 / oe, B_DH, axis=1)
            for g in range(group):
                h = kvh * group + g
                y_ref[:, h * B_DH:(h + 1) * B_DH] = o[g * tq:(g + 1) * tq, :B_DH].astype(y_ref.dtype)

    @pl.when(qb < nqc)
    def _():
        attend(tc)

    @pl.when(qb >= nqc)
    def _():
        attend(t)


def _gqa_call(p, qw, kw, cos2, sin2, l, tc):
    b, t, _ = p.shape
    tq = Q_TILE
    row_blk = lambda: pl.BlockSpec((tq, LANES), lambda i, j: (j, 0))
    full = lambda: pl.BlockSpec((t, LANES), lambda i, j: (0, 0))
    return pl.pallas_call(
        functools.partial(_gqa_kernel, tc=tc, nqc=tc // tq),
        grid=(b, t // tq),
        in_specs=[pl.BlockSpec((None, tq, B_HEADS * B_DH), lambda i, j: (i, j, P_BQ // (B_HEADS * B_DH))),
                  pl.BlockSpec((None, t, LANES), lambda i, j: (i, 0, P_BK // LANES)),
                  pl.BlockSpec((None, t, LANES), lambda i, j: (i, 0, P_BV // LANES)),
                  pl.BlockSpec((None, 1, LANES), lambda i, j: (l, 0, 0)),
                  pl.BlockSpec((None, 1, LANES), lambda i, j: (l, 0, 0)),
                  row_blk(), row_blk(), full(), full()],
        out_specs=pl.BlockSpec((None, tq, BRANCH_W), lambda i, j: (i, j, 0)),
        out_shape=jax.ShapeDtypeStruct((b, t, BRANCH_W), BF16),
        scratch_shapes=[pltpu.VMEM((t, LANES), BF16), pltpu.VMEM((B_KV_HEADS, t, LANES), BF16)],
        compiler_params=_params(("arbitrary", "arbitrary")),
        name="gqa",
    )(p, p, p, qw, kw, cos2, sin2, cos2, sin2)


def _group_size(nc):
    return max(g for g in range(1, SCAN_GROUP + 1) if nc % g == 0)


def _gdn_kernel(q_ref, k_ref, v_ref, z_ref, cwq_ref, cwk_ref, cwv_ref, gcol_ref, grow_ref, par_ref,
                nw_ref, y_ref, qn_ref, kn_ref, vn_ref, n_ref, kq_ref, egl_ref, o_ref, *, nc, ncc):
    L = CHUNK
    t = q_ref.shape[0]
    tc = ncc * L
    le, ge, lt, gt = _chunk_masks()
    par = par_ref[...]

    rows = lax.broadcasted_iota(jnp.int32, (t, 1), 0)
    has_prev = jnp.logical_and(rows != 0, rows != tc)
    has_next = jnp.logical_and(rows != tc - 1, rows != t - 1)

    def conv_silu(x_ref, cw_ref):
        x = x_ref[...]
        cw = cw_ref[...]
        prev = jnp.where(has_prev, pltpu.roll(x, 1, axis=0), 0.0)
        nxt = jnp.where(has_next, pltpu.roll(x, t - 1, axis=0), 0.0)
        return _silu(prev * cw[0:1] + x * cw[1:2] + nxt * cw[2:3])

    def l2n(x):
        return x * lax.rsqrt(jnp.sum(x * x, axis=-1, keepdims=True) + EPS)

    qn_ref[...] = l2n(conv_silu(q_ref, cwq_ref)) * (C_DK ** -0.5)
    kn_ref[...] = l2n(conv_silu(k_ref, cwk_ref))
    vn_ref[...] = conv_silu(v_ref, cwv_ref)

    grp = _group_size(nc)
    ng = nc // grp
    eye = jnp.where(le & ge, 1.0, 0.0).astype(F32)
    rowi = lax.broadcasted_iota(jnp.int32, (L, L), 0)
    coli = lax.broadcasted_iota(jnp.int32, (L, L), 1)
    blk = {b_: (rowi // (2 * b_) == coli // (2 * b_)) & (rowi // b_ != coli // b_) for b_ in (1, 2, 4, 8, 16, 32)}

    def prep_stages(gi):
        chains = []
        for j in range(grp):
            it = gi * grp + j
            for d, c in ((0, it), (1, _bwd_chunk(it, nc, ncc))):
                r0 = pl.multiple_of(c * L, L)
                chains.append(dict(it=it, d=d, c=c, r0=r0, q=qn_ref[pl.ds(r0, L), :],
                                   k=kn_ref[pl.ds(r0, L), :], v=vn_ref[pl.ds(r0, L), :]))
        for ch in chains:
            ch["kkqk"] = _dot_nt(jnp.concatenate([ch["k"], ch["q"]], axis=0), ch["k"])
        yield
        for ch in chains:
            d = ch["d"]
            gc = gcol_ref[pl.ds(ch["r0"], L), :]
            gr = grow_ref[ch["c"]]
            mask_in, mask_t, strict = (le, ge, lt) if d == 0 else (ge, le, gt)
            neg_rate = -jnp.exp(par[:, d:d + 1])
            g_col = neg_rate * _softplus(gc[:, d:d + 1] + par[:, 2 + d:3 + d])
            g_row = neg_rate * _softplus(gr[d:d + 1, :] + par[:, 2 + d:3 + d])
            beta = _sigmoid(gc[:, 2 + d:3 + d])
            G_col = jnp.sum(jnp.where(mask_in, g_row, 0.0), axis=1, keepdims=True)
            G_row = jnp.sum(jnp.where(mask_t, g_col, 0.0), axis=0, keepdims=True)
            dec = jnp.exp(jnp.where(mask_in, G_col - G_row, NEG))
            g_last = jnp.sum(g_row, axis=1, keepdims=True)
            ch.update(beta=beta, dec=dec, eG=jnp.exp(G_col), kscale=jnp.exp(g_last - G_col),
                      egl=jnp.exp(g_last), x=-jnp.where(strict, beta * ch["kkqk"][:L] * dec, 0.0))
        invs = [eye + jnp.where(blk[1], ch["x"], 0.0) for ch in chains]
        bsz = 2
        while bsz < L:
            offs = [jnp.where(blk[bsz], ch["x"], 0.0) for ch in chains]
            tmps = [_dot(off, inv) for off, inv in zip(offs, invs)]
            yield
            invs = [inv + _dot(inv, tmp) for inv, tmp in zip(invs, tmps)]
            yield
            bsz *= 2
        uws = []
        for ch, inv in zip(chains, invs):
            rhs = jnp.concatenate([ch["v"] * ch["beta"], ch["k"] * (ch["beta"] * ch["eG"])], axis=1)
            uws.append(_dot(inv, rhs))
        yield
        kns = [_dot_tn(ch["k"] * ch["kscale"], uw) for ch, uw in zip(chains, uws)]
        yield
        qos = [_dot(ch["kkqk"][L:] * ch["dec"], uw) for ch, uw in zip(chains, uws)]
        yield
        for ch, kn, qo in zip(chains, kns, qos):
            d, it = ch["d"], ch["it"]
            n_ref[d, it] = kn[:, :C_DV]
            kq_ref[d, it, 0:C_DK, :] = kn[:, C_DV:].astype(BF16)
            kq_ref[d, it, C_DK:C_DK + L, :] = (ch["q"] * ch["eG"] - qo[:, C_DV:]).astype(BF16)
            o_ref[d, pl.ds(ch["r0"], L), :] = qo[:, :C_DV]
            egl_ref[d, it] = jnp.broadcast_to(ch["egl"], (8, LANES))

    def recur(it, S):
        out = []
        for d, c in ((0, it), (1, _bwd_chunk(it, nc, ncc))):
            r0 = pl.multiple_of(c * L, L)
            ks_qs = jnp.dot(kq_ref[d, it], S[d].astype(BF16), preferred_element_type=F32)
            o_ref[d, pl.ds(r0, L), :] += ks_qs[C_DK:]
            out.append(S[d] * egl_ref[d, it][0:1, 0:1] + (n_ref[d, it] - ks_qs[:C_DK]))
        return out

    for _ in prep_stages(0):
        pass

    def body(gi, carry):
        S = list(carry)
        stages = prep_stages(gi)
        for j in range(grp):
            next(stages, None)
            S = recur((gi - 1) * grp + j, S)
        for _ in stages:
            pass
        return tuple(S)

    s0 = jnp.zeros((C_DK, C_DV), F32)
    S = list(lax.fori_loop(1, ng, body, (s0, s0)))
    for j in range(grp):
        S = recur((ng - 1) * grp + j, S)

    osum = o_ref[0] + o_ref[1]
    on = osum * lax.rsqrt(jnp.mean(osum * osum, axis=-1, keepdims=True) + EPS) * nw_ref[...]
    y_ref[...] = (on * _silu(z_ref[...])).astype(y_ref.dtype)


def _gdn_call(p, cw, gcol, grow, par, nw, l, ncc):
    b, t, _ = p.shape
    nc = t // CHUNK
    blk = lambda base: pl.BlockSpec((None, t, LANES), lambda i, h: (i, 0, base // LANES + h))
    cwb = lambda off: pl.BlockSpec((None, 3, LANES), lambda i, h: (l, 0, off + h))
    big = lambda dt: pltpu.VMEM((2, t, LANES), dt)
    return pl.pallas_call(
        functools.partial(_gdn_kernel, nc=nc, ncc=ncc),
        grid=(b, C_HEADS),
        in_specs=[blk(P_CQ), blk(P_CK), blk(P_CV), blk(P_CZ), cwb(0), cwb(C_HEADS), cwb(2 * C_HEADS),
                  pl.BlockSpec((None, None, t, 4), lambda i, h: (i, h, 0, 0)),
                  pl.BlockSpec((None, None, nc, 4, CHUNK), lambda i, h: (i, h, 0, 0, 0)),
                  pl.BlockSpec((None, None, 1, 4), lambda i, h: (l, h, 0, 0)),
                  pl.BlockSpec((None, 1, LANES), lambda i, h: (l, 0, 0))],
        out_specs=pl.BlockSpec((None, t, LANES), lambda i, h: (i, 0, h)),
        out_shape=jax.ShapeDtypeStruct((b, t, BRANCH_W), BF16),
        scratch_shapes=[pltpu.VMEM((t, LANES), F32), pltpu.VMEM((t, LANES), F32),
                        pltpu.VMEM((t, LANES), F32),
                        pltpu.VMEM((2, nc, C_DK, C_DV), F32),
                        pltpu.VMEM((2, nc, C_DK + CHUNK, C_DV), BF16),
                        pltpu.VMEM((2, nc, 8, LANES), F32),
                        big(F32)],
        compiler_params=_params(("arbitrary", "arbitrary")),
        name="gdn",
    )(p, p, p, p, cw, cw, cw, gcol, grow, par, nw)


def _merge_kernel(ya_ref, yb_ref, yc_ref, ga_ref, gb_ref, gc_ref, h_ref, mod_ref, wb_ref, wo_ref, o_ref):
    y = (_sigmoid(ga_ref[...].astype(F32)) * _dot(ya_ref[...], wb_ref[0])
         + _sigmoid(gb_ref[...].astype(F32)) * _dot(yb_ref[...], wb_ref[1])
         + _sigmoid(gc_ref[...].astype(F32)) * _dot(yc_ref[...], wb_ref[2]))
    o_ref[...] = h_ref[...] + mod_ref[2:3, :] * _dot(y, wo_ref[...])


def _mod_row(j, nct):
    return 1 if nct == 0 else jnp.minimum(j // nct, 1)


def _merge_call(ya, yb, yc, gates, h, mod, wb, wo, l, tm, nct, latent_only):
    b, t, d = h.shape
    skip = nct if latent_only else 0
    yblk = lambda: pl.BlockSpec((None, tm, BRANCH_W), lambda i, j: (i, j + skip, 0))
    gblk = lambda g: pl.BlockSpec((None, tm, d), lambda i, j: (i, j + skip, g))
    return pl.pallas_call(
        _merge_kernel,
        grid=(b, t // tm - skip),
        in_specs=[yblk(), yblk(), yblk(), gblk(0), gblk(1), gblk(2),
                  pl.BlockSpec((None, tm, d), lambda i, j: (i, j + skip, 0)),
                  pl.BlockSpec((None, None, None, 6, d), lambda i, j: (l, i, _mod_row(j + skip, nct), 0, 0)),
                  _resident((None, 3, BRANCH_W, d), lambda i, j: (l, 0, 0, 0)),
                  _resident((None, d, d), lambda i, j: (l, 0, 0))],
        out_specs=pl.BlockSpec((None, tm, d), lambda i, j: (i, j, 0)),
        out_shape=jax.ShapeDtypeStruct((b, t - skip * tm, d), F32),
        compiler_params=_params(("arbitrary", "arbitrary")),
        name="merge",
    )(ya, yb, yc, gates, gates, gates, h, mod, wb, wo)


def _ffn_kernel(h_ref, hp_ref, hn_ref, mod_ref, nw_ref, wup_ref, cw_ref, wdn_ref, o_ref, acc_ref,
                *, nct, ntiles):
    tm = h_ref.shape[0]
    j = pl.program_id(1)
    mod = mod_ref[...]
    nw = nw_ref[...]
    h = h_ref[...]
    xn = _modnorm(h, nw, mod[3:4], mod[4:5]).astype(BF16)
    halo = jnp.concatenate([hp_ref[...], hn_ref[...]], axis=0)
    xh = _modnorm(halo, nw, mod[3:4], mod[4:5]).astype(BF16)
    has_prev = jnp.logical_and(j != 0, j != nct).astype(F32)
    has_next = jnp.logical_and(j != nct - 1, j != ntiles - 1).astype(F32)
    rows = lax.broadcasted_iota(jnp.int32, (tm, 1), 0)
    first = rows == 0
    last = rows == tm - 1

    def cols(kind, jf):
        return slice(kind * D_FF + jf * FFN_TF, kind * D_FF + (jf + 1) * FFN_TF)

    x_all = jnp.concatenate([xn, xh], axis=0)

    def up(jf):
        us = [jnp.dot(x_all, wup_ref[:, cols(kind, jf)], preferred_element_type=F32) for kind in range(2)]
        return [(u[:tm], u[tm:]) for u in us]

    def conv(kind, jf, u, uh):
        cw = cw_ref[:, cols(kind, jf)]
        prev = jnp.where(first, uh[7:8] * has_prev, pltpu.roll(u, 1, axis=0))
        nxt = jnp.where(last, uh[8:9] * has_next, pltpu.roll(u, tm - 1, axis=0))
        return prev * cw[0:1] + u * cw[1:2] + nxt * cw[2:3]

    nj = D_FF // FFN_TF
    ahead = [up(jf) for jf in range(min(FFN_AHEAD, nj))]
    acts = []
    for jf in range(nj):
        if jf + FFN_AHEAD < nj:
            ahead.append(up(jf + FFN_AHEAD))
        cur = ahead.pop(0)
        acts.append((conv(0, jf, *cur[0]) * _silu(conv(1, jf, *cur[1]))).astype(BF16))
        if len(acts) == FFN_DOWN_GROUP or jf == nj - 1:
            r1 = (jf + 1) * FFN_TF
            part = jnp.dot(jnp.concatenate(acts, axis=1), wdn_ref[r1 - len(acts) * FFN_TF:r1, :],
                           preferred_element_type=F32)
            if r1 == len(acts) * FFN_TF:
                acc_ref[...] = part
            else:
                acc_ref[...] += part
            acts = []
    o_ref[...] = h + mod[5:6] * acc_ref[...]


def _ffn_call(h, mod, nw, wup, cw, wdn, l, tm, nct):
    b, t, d = h.shape
    ntiles = t // tm
    hb = tm // 8
    return pl.pallas_call(
        functools.partial(_ffn_kernel, nct=nct, ntiles=ntiles),
        grid=(b, ntiles),
        in_specs=[pl.BlockSpec((None, tm, d), lambda i, j: (i, j, 0)),
                  pl.BlockSpec((None, 8, d), lambda i, j: (i, jnp.maximum(j * hb - 1, 0), 0)),
                  pl.BlockSpec((None, 8, d), lambda i, j: (i, jnp.minimum((j + 1) * hb, t // 8 - 1), 0)),
                  pl.BlockSpec((None, None, None, 6, d), lambda i, j: (l, i, _mod_row(j, nct), 0, 0)),
                  pl.BlockSpec((None, 1, d), lambda i, j: (l, 0, 0)),
                  _resident((None,) + wup.shape[1:], lambda i, j: (l, 0, 0)),
                  _resident((None,) + cw.shape[1:], lambda i, j: (l, 0, 0)),
                  _resident((None,) + wdn.shape[1:], lambda i, j: (l, 0, 0))],
        out_specs=pl.BlockSpec((None, tm, d), lambda i, j: (i, j, 0)),
        out_shape=jax.ShapeDtypeStruct((b, t, d), F32),
        scratch_shapes=[pltpu.VMEM((tm, d), F32)],
        compiler_params=_params(("arbitrary", "arbitrary")),
        name="ffn",
    )(h, h, h, mod, nw, wup, cw, wdn)


def _rope_tables(tc, tl):
    rows = tl // GRID_W
    row = jnp.repeat(jnp.arange(rows, dtype=F32), GRID_W)
    col = jnp.tile(jnp.arange(GRID_W, dtype=F32), rows)
    n_freq = B_DH // 4
    inv = ROPE_BASE ** (-jnp.arange(n_freq, dtype=F32) / n_freq)
    ang_r = row[:, None] * inv
    ang_c = col[:, None] * inv
    cos = jnp.concatenate([jnp.cos(ang_r)] * 2 + [jnp.cos(ang_c)] * 2, axis=1)
    sin = jnp.concatenate([-jnp.sin(ang_r), jnp.sin(ang_r), -jnp.sin(ang_c), jnp.sin(ang_c)], axis=1)
    cos = jnp.concatenate([jnp.ones((tc, B_DH), F32), cos], axis=0)
    sin = jnp.concatenate([jnp.zeros((tc, B_DH), F32), sin], axis=0)
    return jnp.tile(cos, (1, 2)), jnp.tile(sin, (1, 2))


def _pack_w_in(w_in):
    depth, d, _ = w_in.shape
    w = w_in.astype(BF16)
    o = 0
    parts = {}
    for name, width in (("aq", 256), ("ak", 256), ("av", 512), ("ao", 512), ("ag", 16),
                        ("bq", 512), ("bk", 128), ("bv", 128),
                        ("cq", 512), ("ck", 512), ("cv", 512), ("cz", 512), ("ca", 8), ("cb", 8),
                        ("gate", 3 * D_MODEL)):
        parts[name] = w[:, :, o:o + width]
        o += width
    aqk = jnp.concatenate([parts["aq"].reshape(depth, d, A_HEADS, A_DQK),
                           parts["ak"].reshape(depth, d, A_HEADS, A_DQK)], axis=3).reshape(depth, d, 512)
    small = jnp.concatenate([parts["ag"], parts["ca"], parts["cb"]], axis=2)
    pad = jnp.zeros((depth, d, W_COLS - W_SMALL - small.shape[2]), BF16)
    cols = [parts["gate"], parts["av"], parts["ao"], parts["bq"], parts["bk"], parts["bv"],
            parts["cq"], parts["ck"], parts["cv"], parts["cz"], aqk, small, pad]
    return jnp.concatenate(cols, axis=2)


def _gate_layouts(x, heads, kinds):
    b, t, _ = x.shape
    g = x.reshape(b, t, kinds, heads)
    col = g.transpose(0, 3, 1, 2)
    row = g.reshape(b, t // CHUNK, CHUNK, kinds, heads).transpose(0, 4, 1, 3, 2)
    return col, row


def kernel(x, c, ctx, c_ctx, norm1_w, norm2_w, ada_w, ada_b, w_in, a_gate_b, a_norm_w, b_qnorm_w,
           b_knorm_w, c_conv_w, c_a_log, c_dt_bias, c_norm_w, w_branch, w_out, w_up, ffn_conv_w, w_down):
    b, tl, d = x.shape
    tc = ctx.shape[1]
    depth = w_in.shape[0]
    t = tc + tl
    tm = 256 if (tc % 256 == 0 and tl % 256 == 0) else 128
    nct = tc // tm
    ncc = tc // CHUNK
    nj = D_FF // FFN_TF

    cc = jnp.zeros((16, d), F32).at[:b].set(c).at[b].set(c_ctx)
    mods = _ada_call(cc, ada_w, ada_b).reshape(depth, 16, 6, d)
    mod = jnp.stack([jnp.broadcast_to(mods[:, b][:, None], (depth, b, 6, d)), mods[:, :b]], axis=2)

    w_in_p = _pack_w_in(w_in)
    wb = w_branch.astype(BF16)
    wo = w_out.astype(BF16)
    wup = w_up.astype(BF16)
    wdn = w_down.astype(BF16)
    cos2, sin2 = _rope_tables(tc, tl)
    qw2 = jnp.tile(b_qnorm_w, (1, 2))[:, None]
    kw2 = jnp.tile(b_knorm_w, (1, 2))[:, None]
    a_bcol = a_gate_b.transpose(0, 2, 1)[:, :, :, None]
    c_par = jnp.concatenate([c_a_log, c_dt_bias], axis=1).transpose(0, 2, 1)[:, :, None, :]

    h = jnp.concatenate([ctx, x], axis=1)
    for l in range(depth):
        gates, p, qkt, small = _proj_call(h, mod, norm1_w[:, None], w_in_p, l, tm, nct)
        _, a_grow = _gate_layouts(small[:, :, :16], A_HEADS, 4)
        c_gcol, c_grow = _gate_layouts(small[:, :, 16:32], C_HEADS, 4)
        ya = _mlstm_call(p, qkt, a_grow, a_bcol, a_norm_w[:, None], l, ncc)
        yb = _gqa_call(p, qw2, kw2, cos2, sin2, l, tc)
        yc = _gdn_call(p, c_conv_w, c_gcol, c_grow, c_par, c_norm_w[:, None], l, ncc)
        last = l == depth - 1
        h = _merge_call(ya, yb, yc, gates, h, mod, wb, wo, l, tm, nct, latent_only=last)
        h = _ffn_call(h, mod, norm2_w[:, None], wup, ffn_conv_w, wdn, l, tm, 0 if last else nct)
    return h
```

```python
import functools

import jax
import jax.numpy as jnp
from jax import lax
from jax.experimental import pallas as pl
from jax.experimental.pallas import tpu as pltpu

F32 = jnp.float32
BF16 = jnp.bfloat16

D_MODEL = 1024
GRID_W = 64
A_HEADS, A_DQK, A_DV = 4, 64, 128
B_HEADS, B_KV_HEADS, B_DH = 8, 2, 64
C_HEADS, C_DK, C_DV = 4, 128, 128
BRANCH_W = 512
D_FF = 2816
CHUNK = 64
ROPE_BASE = 10000.0
EPS = 1e-6
M_INIT = -1e30
LOG2_E = 1.4426950408889634
NEG = -1e30

LANES = 128
W_GATES = 3 * D_MODEL
W_AV = 3072
W_BQ = 4096
W_BK = 4608
W_CQ = 4864
W_CZ = 6400
W_AQK = 6912
W_SMALL = 7424
W_COLS = 7552
P_AV = 0
P_AO = 512
P_BV = 1024
P_CQ = 1152
P_CK = 1664
P_CV = 2176
P_CZ = 2688
P_COLS = 3200
PROJ_TN = 512
FFN_TF = 256
FFN_AHEAD = 3
FFN_DOWN_GROUP = 4
Q_TILE = 128
SCAN_GROUP = 12
VMEM_LIMIT = 56 * 1024 * 1024


def _dot(a, b):
    return jnp.dot(a.astype(BF16), b.astype(BF16), preferred_element_type=F32)


def _dot_nt(a, b):
    return lax.dot_general(a.astype(BF16), b.astype(BF16), (((1,), (1,)), ((), ())),
                           preferred_element_type=F32)


def _dot_tn(a, b):
    return lax.dot_general(a.astype(BF16), b.astype(BF16), (((0,), (0,)), ((), ())),
                           preferred_element_type=F32)


def _split(a):
    hi = a.astype(BF16)
    lo = (a - hi.astype(F32)).astype(BF16)
    return hi, lo


def _dot3(a, b):
    ah, al = _split(a)
    bh, bl = _split(b)
    d = functools.partial(jnp.dot, preferred_element_type=F32)
    return d(ah, bh) + (d(al, bh) + d(ah, bl))


def _sigmoid(x):
    return 1.0 / (1.0 + jnp.exp(-x))


def _silu(x):
    return x * _sigmoid(x)


def _softplus(x):
    return jnp.maximum(x, 0.0) + jnp.log1p(jnp.exp(-jnp.abs(x)))


def _logsigmoid(x):
    return -_softplus(-x)


def _modnorm(x, w, shift, scale):
    y = x * lax.rsqrt(jnp.mean(x * x, axis=-1, keepdims=True) + EPS)
    return (y * w) * (1.0 + scale) + shift


def _params(sem, vmem=VMEM_LIMIT):
    return pltpu.CompilerParams(dimension_semantics=sem, vmem_limit_bytes=vmem)


def _resident(shape, index_map):
    return pl.BlockSpec(shape, index_map, pipeline_mode=pl.Buffered(1))


def _ada_kernel(c_ref, w_ref, b_ref, o_ref):
    o_ref[...] = _dot3(_silu(c_ref[...]), w_ref[...]) + b_ref[...]


def _ada_call(cc, ada_w, ada_b):
    depth, d, n = ada_w.shape
    tn = 1536
    return pl.pallas_call(
        _ada_kernel,
        grid=(depth, n // tn),
        in_specs=[pl.BlockSpec(cc.shape, lambda l, j: (0, 0)),
                  pl.BlockSpec((None, d, tn), lambda l, j: (l, 0, j)),
                  pl.BlockSpec((None, 1, tn), lambda l, j: (l, 0, j))],
        out_specs=pl.BlockSpec((None, cc.shape[0], tn), lambda l, j: (l, 0, j)),
        out_shape=jax.ShapeDtypeStruct((depth, cc.shape[0], n), F32),
        compiler_params=_params(("arbitrary", "arbitrary")),
        name="ada",
    )(cc, ada_w, ada_b.reshape(depth, 1, n))


def _proj_kernel(h_ref, hp_ref, hn_ref, mod_ref, nw_ref, w_ref, cw_ref, qw_ref, kw_ref, cos_ref, sin_ref,
                 g_ref, p_ref, bq_ref, bk_ref, qkt_ref, small_ref, *, nct, ntiles):
    tm = h_ref.shape[0]
    j = pl.program_id(1)
    mod = mod_ref[...]
    nw = nw_ref[...]
    xn = _modnorm(h_ref[...], nw, mod[0:1], mod[1:2]).astype(BF16)

    def proj(c0, width):
        return jnp.dot(xn, w_ref[:, c0:c0 + width], preferred_element_type=F32)

    for c0 in range(0, W_GATES, PROJ_TN):
        g_ref[:, c0:c0 + PROJ_TN] = proj(c0, PROJ_TN).astype(BF16)
    for c0 in range(0, P_BV, PROJ_TN):
        p_ref[:, c0:c0 + PROJ_TN] = proj(W_AV + c0, PROJ_TN)
    p_ref[:, P_CZ:P_COLS] = proj(W_CZ, P_COLS - P_CZ)

    r = lax.broadcasted_iota(jnp.int32, (LANES, LANES), 0)
    c = lax.broadcasted_iota(jnp.int32, (LANES, LANES), 1)
    bd = jnp.where(r // B_DH == c // B_DH, 1.0, 0.0).astype(BF16)
    lane_half = (lax.broadcasted_iota(jnp.int32, (tm, LANES), 1) % 32) < 16
    cos, sin = cos_ref[...], sin_ref[...]
    bq = proj(W_BQ, B_HEADS * B_DH)
    for s0 in range(0, B_HEADS * B_DH, LANES):
        y = _headnorm_rope(bq[:, s0:s0 + LANES], bd, qw_ref[...], cos, sin, lane_half)
        bq_ref[:, s0:s0 + LANES] = (y * (B_DH ** -0.5 * LOG2_E)).astype(BF16)
    kv = proj(W_BK, 2 * LANES)
    bk_ref[...] = _headnorm_rope(kv[:, :LANES], bd, kw_ref[...], cos, sin, lane_half).astype(BF16)
    p_ref[:, P_BV:P_BV + LANES] = kv[:, LANES:]

    halo = jnp.concatenate([hp_ref[...], hn_ref[...]], axis=0)
    x_all = jnp.concatenate([xn, _modnorm(halo, nw, mod[0:1], mod[1:2]).astype(BF16)], axis=0)
    has_prev = jnp.logical_and(j != 0, j != nct).astype(F32)
    has_next = jnp.logical_and(j != nct - 1, j != ntiles - 1).astype(F32)
    rows = lax.broadcasted_iota(jnp.int32, (tm, 1), 0)
    width = C_HEADS * C_DK
    for part in range(3):
        c0 = part * width
        u_all = jnp.dot(x_all, w_ref[:, W_CQ + c0:W_CQ + c0 + width], preferred_element_type=F32)
        u = u_all[:tm]
        cw = cw_ref[:, c0:c0 + width]
        prev = jnp.where(rows == 0, u_all[tm + 7:tm + 8] * has_prev, pltpu.roll(u, 1, axis=0))
        nxt = jnp.where(rows == tm - 1, u_all[tm + 8:tm + 9] * has_next, pltpu.roll(u, tm - 1, axis=0))
        y = _silu(prev * cw[0:1] + u * cw[1:2] + nxt * cw[2:3])
        for hd in range(C_HEADS):
            yh = y[:, hd * C_DK:(hd + 1) * C_DK]
            if part < 2:
                yh = yh * lax.rsqrt(jnp.sum(yh * yh, axis=-1, keepdims=True) + EPS)
            if part == 0:
                yh = yh * (C_DK ** -0.5)
            p_ref[:, P_CQ + c0 + hd * C_DK:P_CQ + c0 + (hd + 1) * C_DK] = yh
    qk = jnp.dot(xn, w_ref[:, W_AQK:W_SMALL], preferred_element_type=F32)
    for hd in range(A_HEADS):
        for c in range(tm // CHUNK):
            qkt_ref[hd, c] = qk[c * CHUNK:(c + 1) * CHUNK, hd * LANES:(hd + 1) * LANES].T
    small_ref[...] = jnp.dot(xn, w_ref[:, W_SMALL:W_COLS], preferred_element_type=F32)


def _proj_call(h, mod, nw, w, cw, qw, kw, cos2, sin2, l, tm, nct):
    b, t, d = h.shape
    cpt = tm // CHUNK
    ntiles = t // tm
    hb = tm // 8
    vec = lambda: pl.BlockSpec((None, 1, LANES), lambda i, j: (l, 0, 0))
    table = lambda: pl.BlockSpec((tm, LANES), lambda i, j: (j, 0))
    return pl.pallas_call(
        functools.partial(_proj_kernel, nct=nct, ntiles=ntiles),
        grid=(b, ntiles),
        in_specs=[pl.BlockSpec((None, tm, d), lambda i, j: (i, j, 0)),
                  pl.BlockSpec((None, 8, d), lambda i, j: (i, jnp.maximum(j * hb - 1, 0), 0)),
                  pl.BlockSpec((None, 8, d), lambda i, j: (i, jnp.minimum((j + 1) * hb, t // 8 - 1), 0)),
                  pl.BlockSpec((None, None, None, 6, d), lambda i, j: (l, i, _mod_row(j, nct), 0, 0)),
                  pl.BlockSpec((None, 1, d), lambda i, j: (l, 0, 0)),
                  _resident((None, d, W_COLS), lambda i, j: (l, 0, 0)),
                  _resident((None,) + cw.shape[1:], lambda i, j: (l, 0, 0)),
                  vec(), vec(), table(), table()],
        out_specs=[pl.BlockSpec((None, tm, W_GATES), lambda i, j: (i, j, 0)),
                   pl.BlockSpec((None, tm, P_COLS), lambda i, j: (i, j, 0)),
                   pl.BlockSpec((None, tm, B_HEADS * B_DH), lambda i, j: (i, j, 0)),
                   pl.BlockSpec((None, tm, LANES), lambda i, j: (i, j, 0)),
                   pl.BlockSpec((None, A_HEADS, cpt, 2 * A_DQK, CHUNK), lambda i, j: (i, 0, j, 0, 0)),
                   pl.BlockSpec((None, tm, LANES), lambda i, j: (i, j, 0))],
        out_shape=[jax.ShapeDtypeStruct((b, t, W_GATES), BF16),
                   jax.ShapeDtypeStruct((b, t, P_COLS), F32),
                   jax.ShapeDtypeStruct((b, t, B_HEADS * B_DH), BF16),
                   jax.ShapeDtypeStruct((b, t, LANES), BF16),
                   jax.ShapeDtypeStruct((b, A_HEADS, t // CHUNK, 2 * A_DQK, CHUNK), F32),
                   jax.ShapeDtypeStruct((b, t, LANES), F32)],
        compiler_params=_params(("arbitrary", "arbitrary")),
        name="proj_in",
    )(h, h, h, mod, nw, w, cw, qw, kw, cos2, sin2)


def _chunk_masks():
    row = lax.broadcasted_iota(jnp.int32, (CHUNK, CHUNK), 0)
    col = lax.broadcasted_iota(jnp.int32, (CHUNK, CHUNK), 1)
    return col <= row, col >= row, col < row, col > row


def _bwd_chunk(it, nc, ncc):
    return jnp.where(it < ncc, ncc - 1 - it, nc - 1 - (it - ncc))


def _rows3(x, n_rows=16):
    hi = x.astype(BF16).astype(F32)
    mid = (x - hi).astype(BF16).astype(F32)
    lo = ((x - hi) - mid).astype(BF16).astype(F32)
    r = lax.broadcasted_iota(jnp.int32, (n_rows, x.shape[1]), 0)
    return jnp.where(r == 0, hi, jnp.where(r == 1, mid, jnp.where(r == 2, lo, 0.0))).astype(BF16)


def _mlstm_kernel(qkt_ref, v_ref, o_ref, grow_ref, bcol_ref, nw_ref, y_ref, hf_ref, hb_ref, *, nc, ncc):
    L = CHUNK
    le, ge, _, _ = _chunk_masks()
    vis = (ge, le)
    row_l = lax.broadcasted_iota(jnp.int32, (L, LANES), 0)
    col_l = lax.broadcasted_iota(jnp.int32, (L, LANES), 1)
    ones_l = jnp.ones((L, LANES), BF16)
    zeros_l = jnp.zeros((L, LANES), BF16)
    stat_rhs = [jnp.concatenate(
        [ones_l, jnp.where((col_l < L) & ((col_l >= row_l) if d == 0 else (col_l <= row_l)), 1.0, 0.0).astype(BF16)],
        axis=1) for d in range(2)]
    ones3 = jnp.where(lax.broadcasted_iota(jnp.int32, (16, LANES), 0) < 3, 1.0, 0.0).astype(BF16)
    tail_rhs = jnp.concatenate([jnp.zeros((16, 2 * LANES), BF16), ones3], axis=1)
    bias_col = bcol_ref[...]
    grp = _group_size(nc)
    h_refs = (hf_ref, hb_ref)

    def sum3(x):
        return x[0:1] + x[1:2] + x[2:3]

    def body(gi, carry):
        C = [carry[0], carry[2]]
        m = [carry[1], carry[3]]
        chains = []
        for j in range(grp):
            it = gi * grp + j
            for d, c in ((0, it), (1, _bwd_chunk(it, nc, ncc))):
                gr = grow_ref[c] + bias_col
                x = qkt_ref[c]
                chains.append(dict(d=d, r0=pl.multiple_of(c * L, L), i_row=gr[2 * d:2 * d + 1, :],
                                   f3=_rows3(_logsigmoid(gr[2 * d + 1:2 * d + 2, :])),
                                   qt=x[:A_DQK] * (A_DQK ** -0.5), kt=x[A_DQK:]))
        for ch in chains:
            ch["v"] = v_ref[pl.ds(ch["r0"], L), :].astype(BF16)
            ch["sqt"] = _dot_tn(ch["kt"], ch["qt"])
        for ch in chains:
            st = jnp.dot(ch["f3"], stat_rhs[ch["d"]], preferred_element_type=F32)
            ch["btot"] = sum3(st[:, :LANES])
            ch["bcum"] = sum3(st[:, LANES:LANES + L])
            ch["a_row"] = ch["i_row"] - ch["bcum"]
        for ch in chains:
            ch["a_col"] = lax.dot_general(_rows3(ch["a_row"]), ones3, (((0,), (0,)), ((), ())),
                                          preferred_element_type=F32)
        for ch in chains:
            d = ch["d"]
            m_new = jnp.maximum(ch["btot"] + m[d], ch["btot"] + jnp.max(ch["a_col"], axis=0, keepdims=True))
            dlog = jnp.where(vis[d], ch["bcum"] + ch["a_col"][:, :L], NEG)
            inter = ch["bcum"] + m[d][:, :L]
            mt = jnp.maximum(inter, jnp.max(dlog, axis=0, keepdims=True))
            ch["st"] = (ch["sqt"] * jnp.exp(dlog - mt)).astype(BF16)
            ch["qe"] = (ch["qt"] * jnp.exp(inter - mt)).astype(BF16)
            ch["e3"] = _rows3(jnp.exp(-mt))
            ch["kw"] = (ch["kt"] * jnp.exp(ch["btot"][:, :L] + ch["a_row"] - m_new[:, :L])).astype(BF16)
            decay = jnp.exp(ch["btot"] + m[d] - m_new)
            ch["decay"] = jnp.concatenate([decay, decay], axis=1)
            m[d] = m_new
        for ch in chains:
            ch["kv"] = jnp.dot(ch["kw"], jnp.concatenate([ch["v"], ones_l], axis=1), preferred_element_type=F32)
            lhs = jnp.concatenate([ch["st"], ch["e3"]], axis=0)
            rhs = jnp.concatenate([jnp.concatenate([ch["v"], ones_l, zeros_l], axis=1), tail_rhs], axis=0)
            ch["intra"] = lax.dot_general(lhs, rhs, (((0,), (0,)), ((), ())), preferred_element_type=F32)
        for ch in chains:
            d = ch["d"]
            ch["inter"] = lax.dot_general(ch["qe"], C[d].astype(BF16), (((0,), (0,)), ((), ())),
                                          preferred_element_type=F32)
            C[d] = ch["decay"] * C[d] + ch["kv"]
        for ch in chains:
            num = ch["intra"][:, :A_DV] + ch["inter"][:, :A_DV]
            den = ch["intra"][:, A_DV:2 * A_DV] + ch["inter"][:, A_DV:]
            h_refs[ch["d"]][pl.ds(ch["r0"], L), :] = num / jnp.maximum(jnp.abs(den), ch["intra"][:, 2 * A_DV:])
        return C[0], m[0], C[1], m[1]

    c0 = jnp.zeros((A_DQK, 2 * LANES), F32)
    m0 = jnp.full((1, LANES), M_INIT, F32)
    lax.fori_loop(0, nc // grp, body, (c0, m0, c0, m0))

    hsum = hf_ref[...] + hb_ref[...]
    hn = hsum * lax.rsqrt(jnp.mean(hsum * hsum, axis=-1, keepdims=True) + EPS)
    y_ref[...] = (hn * nw_ref[...] * _sigmoid(o_ref[...])).astype(y_ref.dtype)


def _mlstm_call(p, qkt, grow, bcol, nw, l, ncc):
    b, t, _ = p.shape
    nc = t // CHUNK
    blk = lambda base: pl.BlockSpec((None, t, LANES), lambda i, h: (i, 0, base // LANES + h))
    return pl.pallas_call(
        functools.partial(_mlstm_kernel, nc=nc, ncc=ncc),
        grid=(b, A_HEADS),
        in_specs=[pl.BlockSpec((None, None, nc, 2 * A_DQK, CHUNK), lambda i, h: (i, h, 0, 0, 0)),
                  blk(P_AV), blk(P_AO),
                  pl.BlockSpec((None, None, nc, 4, CHUNK), lambda i, h: (i, h, 0, 0, 0)),
                  pl.BlockSpec((None, None, 4, 1), lambda i, h: (l, h, 0, 0)),
                  pl.BlockSpec((None, 1, LANES), lambda i, h: (l, 0, h))],
        out_specs=pl.BlockSpec((None, t, LANES), lambda i, h: (i, 0, h)),
        out_shape=jax.ShapeDtypeStruct((b, t, BRANCH_W), BF16),
        scratch_shapes=[pltpu.VMEM((t, LANES), F32), pltpu.VMEM((t, LANES), F32)],
        compiler_params=_params(("arbitrary", "arbitrary")),
        name="mlstm",
    )(qkt, p, p, grow, bcol, nw)


def _headnorm_rope(x, bd, w, cos, sin_signed, lane_half):
    hi, lo = _split(x * x)
    ssum = jnp.dot(hi, bd, preferred_element_type=F32) + jnp.dot(lo, bd, preferred_element_type=F32)
    y = x * lax.rsqrt(ssum * (1.0 / B_DH) + EPS) * w
    swapped = jnp.where(lane_half, pltpu.roll(y, LANES - 16, axis=1), pltpu.roll(y, 16, axis=1))
    return y * cos + swapped * sin_signed


def _gqa_kernel(q_ref, k_ref, v_ref, y_ref, vb_ref, *, tc, first_q):
    tq = q_ref.shape[0]
    t = k_ref.shape[0]
    qb = pl.program_id(1) + first_q

    @pl.when(pl.program_id(1) == 0)
    def _():
        v = v_ref[...]
        first = lax.broadcasted_iota(jnp.int32, (t, LANES), 1) < B_DH
        vb_ref[0] = jnp.where(first, v, 1.0).astype(BF16)
        vb_ref[1] = jnp.where(first, pltpu.roll(v, B_DH, axis=1), 1.0).astype(BF16)

    group = B_HEADS // B_KV_HEADS

    def attend(nk):
        scores = []
        for kvh in range(B_KV_HEADS):
            kh = k_ref[0:nk, kvh * B_DH:(kvh + 1) * B_DH]
            qs = jnp.concatenate([q_ref[:, h * B_DH:(h + 1) * B_DH]
                                  for h in range(kvh * group, (kvh + 1) * group)], axis=0)
            scores.append(lax.dot_general(qs, kh, (((1,), (1,)), ((), ())), preferred_element_type=F32))
        for kvh, s in enumerate(scores):
            p = jnp.exp2(s - jnp.max(s, axis=-1, keepdims=True))
            oe = jnp.dot(p.astype(BF16), vb_ref[kvh, 0:nk, :], preferred_element_type=F32)
            o = oe * pltpu.roll(1.0 / oe, B_DH, axis=1)
            for g in range(group):
                h = kvh * group + g
                y_ref[:, h * B_DH:(h + 1) * B_DH] = o[g * tq:(g + 1) * tq, :B_DH].astype(y_ref.dtype)

    nqc = tc // tq
    if first_q < nqc:
        @pl.when(qb < nqc)
        def _():
            attend(tc)

    @pl.when(qb >= nqc)
    def _():
        attend(t)


def _gqa_call(bq, bk, p, tc, latent_only):
    b, t, _ = p.shape
    tq = Q_TILE
    first_q = tc // tq if latent_only else 0
    return pl.pallas_call(
        functools.partial(_gqa_kernel, tc=tc, first_q=first_q),
        grid=(b, t // tq - first_q),
        in_specs=[pl.BlockSpec((None, tq, B_HEADS * B_DH), lambda i, j: (i, j + first_q, 0)),
                  pl.BlockSpec((None, t, LANES), lambda i, j: (i, 0, 0)),
                  pl.BlockSpec((None, t, LANES), lambda i, j: (i, 0, P_BV // LANES))],
        out_specs=pl.BlockSpec((None, tq, BRANCH_W), lambda i, j: (i, j, 0)),
        out_shape=jax.ShapeDtypeStruct((b, t - first_q * tq, BRANCH_W), BF16),
        scratch_shapes=[pltpu.VMEM((B_KV_HEADS, t, LANES), BF16)],
        compiler_params=_params(("arbitrary", "arbitrary")),
        name="gqa",
    )(bq, bk, p)


def _group_size(nc):
    return max(g for g in range(1, SCAN_GROUP + 1) if nc % g == 0)


def _gdn_kernel(qn_ref, kn_ref, vn_ref, z_ref, gcol_ref, grow_ref, par_ref, nw_ref, y_ref,
                n_ref, kq_ref, egl_ref, o_ref, *, nc, ncc):
    L = CHUNK
    le, ge, lt, gt = _chunk_masks()
    par = par_ref[...]

    grp = _group_size(nc)
    ng = nc // grp
    eye = jnp.where(le & ge, 1.0, 0.0).astype(F32)
    rowi = lax.broadcasted_iota(jnp.int32, (L, L), 0)
    coli = lax.broadcasted_iota(jnp.int32, (L, L), 1)
    blk = {b_: (rowi // (2 * b_) == coli // (2 * b_)) & (rowi // b_ != coli // b_) for b_ in (1, 2, 4, 8, 16, 32)}

    def prep_stages(gi):
        chains = []
        for j in range(grp):
            it = gi * grp + j
            for d, c in ((0, it), (1, _bwd_chunk(it, nc, ncc))):
                r0 = pl.multiple_of(c * L, L)
                chains.append(dict(it=it, d=d, c=c, r0=r0, q=qn_ref[pl.ds(r0, L), :],
                                   k=kn_ref[pl.ds(r0, L), :], v=vn_ref[pl.ds(r0, L), :]))
        for ch in chains:
            ch["kkqk"] = _dot_nt(jnp.concatenate([ch["k"], ch["q"]], axis=0), ch["k"])
        yield
        for ch in chains:
            d = ch["d"]
            gc = gcol_ref[pl.ds(ch["r0"], L), :]
            gr = grow_ref[ch["c"]]
            mask_in, mask_t, strict = (le, ge, lt) if d == 0 else (ge, le, gt)
            neg_rate = -jnp.exp(par[:, d:d + 1])
            g_col = neg_rate * _softplus(gc[:, d:d + 1] + par[:, 2 + d:3 + d])
            g_row = neg_rate * _softplus(gr[d:d + 1, :] + par[:, 2 + d:3 + d])
            beta = _sigmoid(gc[:, 2 + d:3 + d])
            G_col = jnp.sum(jnp.where(mask_in, g_row, 0.0), axis=1, keepdims=True)
            G_row = jnp.sum(jnp.where(mask_t, g_col, 0.0), axis=0, keepdims=True)
            dec = jnp.exp(jnp.where(mask_in, G_col - G_row, NEG))
            g_last = jnp.sum(g_row, axis=1, keepdims=True)
            ch.update(beta=beta, dec=dec, eG=jnp.exp(G_col), kscale=jnp.exp(g_last - G_col),
                      egl=jnp.exp(g_last), x=-jnp.where(strict, beta * ch["kkqk"][:L] * dec, 0.0))
        invs = [eye + jnp.where(blk[1], ch["x"], 0.0) for ch in chains]
        bsz = 2
        while bsz < L:
            offs = [jnp.where(blk[bsz], ch["x"], 0.0) for ch in chains]
            tmps = [_dot(off, inv) for off, inv in zip(offs, invs)]
            yield
            invs = [inv + _dot(inv, tmp) for inv, tmp in zip(invs, tmps)]
            yield
            bsz *= 2
        uws = []
        for ch, inv in zip(chains, invs):
            rhs = jnp.concatenate([ch["v"] * ch["beta"], ch["k"] * (ch["beta"] * ch["eG"])], axis=1)
            uws.append(_dot(inv, rhs))
        yield
        kns = [_dot_tn(ch["k"] * ch["kscale"], uw) for ch, uw in zip(chains, uws)]
        yield
        qos = [_dot(ch["kkqk"][L:] * ch["dec"], uw) for ch, uw in zip(chains, uws)]
        yield
        for ch, kn, qo in zip(chains, kns, qos):
            d, it = ch["d"], ch["it"]
            n_ref[d, it] = kn[:, :C_DV]
            kq_ref[d, it, 0:C_DK, :] = kn[:, C_DV:].astype(BF16)
            kq_ref[d, it, C_DK:C_DK + L, :] = (ch["q"] * ch["eG"] - qo[:, C_DV:]).astype(BF16)
            o_ref[d, pl.ds(ch["r0"], L), :] = qo[:, :C_DV]
            egl_ref[d, it] = jnp.broadcast_to(ch["egl"], (8, LANES))

    def recur(it, S):
        out = []
        for d, c in ((0, it), (1, _bwd_chunk(it, nc, ncc))):
            r0 = pl.multiple_of(c * L, L)
            ks_qs = jnp.dot(kq_ref[d, it], S[d].astype(BF16), preferred_element_type=F32)
            o_ref[d, pl.ds(r0, L), :] += ks_qs[C_DK:]
            out.append(S[d] * egl_ref[d, it][0:1, 0:1] + (n_ref[d, it] - ks_qs[:C_DK]))
        return out

    for _ in prep_stages(0):
        pass

    def body(gi, carry):
        S = list(carry)
        stages = prep_stages(gi)
        for j in range(grp):
            next(stages, None)
            S = recur((gi - 1) * grp + j, S)
        for _ in stages:
            pass
        return tuple(S)

    s0 = jnp.zeros((C_DK, C_DV), F32)
    S = list(lax.fori_loop(1, ng, body, (s0, s0)))
    for j in range(grp):
        S = recur((ng - 1) * grp + j, S)

    osum = o_ref[0] + o_ref[1]
    on = osum * lax.rsqrt(jnp.mean(osum * osum, axis=-1, keepdims=True) + EPS) * nw_ref[...]
    y_ref[...] = (on * _silu(z_ref[...])).astype(y_ref.dtype)


def _gdn_call(p, gcol, grow, par, nw, l, ncc):
    b, t, _ = p.shape
    nc = t // CHUNK
    blk = lambda base: pl.BlockSpec((None, t, LANES), lambda i, h: (i, 0, base // LANES + h))
    big = lambda dt: pltpu.VMEM((2, t, LANES), dt)
    return pl.pallas_call(
        functools.partial(_gdn_kernel, nc=nc, ncc=ncc),
        grid=(b, C_HEADS),
        in_specs=[blk(P_CQ), blk(P_CK), blk(P_CV), blk(P_CZ),
                  pl.BlockSpec((None, None, t, 4), lambda i, h: (i, h, 0, 0)),
                  pl.BlockSpec((None, None, nc, 4, CHUNK), lambda i, h: (i, h, 0, 0, 0)),
                  pl.BlockSpec((None, None, 1, 4), lambda i, h: (l, h, 0, 0)),
                  pl.BlockSpec((None, 1, LANES), lambda i, h: (l, 0, 0))],
        out_specs=pl.BlockSpec((None, t, LANES), lambda i, h: (i, 0, h)),
        out_shape=jax.ShapeDtypeStruct((b, t, BRANCH_W), BF16),
        scratch_shapes=[pltpu.VMEM((2, nc, C_DK, C_DV), F32),
                        pltpu.VMEM((2, nc, C_DK + CHUNK, C_DV), BF16),
                        pltpu.VMEM((2, nc, 8, LANES), F32),
                        big(F32)],
        compiler_params=_params(("arbitrary", "arbitrary")),
        name="gdn",
    )(p, p, p, p, gcol, grow, par, nw)


def _merge_kernel(ya_ref, yb_ref, yc_ref, ga_ref, gb_ref, gc_ref, h_ref, mod_ref, wb_ref, wo_ref, o_ref):
    y = (_sigmoid(ga_ref[...].astype(F32)) * _dot(ya_ref[...], wb_ref[0])
         + _sigmoid(gb_ref[...].astype(F32)) * _dot(yb_ref[...], wb_ref[1])
         + _sigmoid(gc_ref[...].astype(F32)) * _dot(yc_ref[...], wb_ref[2]))
    o_ref[...] = h_ref[...] + mod_ref[2:3, :] * _dot(y, wo_ref[...])


def _mod_row(j, nct):
    return 1 if nct == 0 else jnp.minimum(j // nct, 1)


def _merge_call(ya, yb, yc, gates, h, mod, wb, wo, l, tm, nct, latent_only):
    b, t, d = h.shape
    skip = nct if latent_only else 0
    yblk = lambda: pl.BlockSpec((None, tm, BRANCH_W), lambda i, j: (i, j + skip, 0))
    yb_skip = skip - (t - yb.shape[1]) // tm
    gblk = lambda g: pl.BlockSpec((None, tm, d), lambda i, j: (i, j + skip, g))
    return pl.pallas_call(
        _merge_kernel,
        grid=(b, t // tm - skip),
        in_specs=[yblk(), pl.BlockSpec((None, tm, BRANCH_W), lambda i, j: (i, j + yb_skip, 0)), yblk(),
                  gblk(0), gblk(1), gblk(2),
                  pl.BlockSpec((None, tm, d), lambda i, j: (i, j + skip, 0)),
                  pl.BlockSpec((None, None, None, 6, d), lambda i, j: (l, i, _mod_row(j + skip, nct), 0, 0)),
                  _resident((None, 3, BRANCH_W, d), lambda i, j: (l, 0, 0, 0)),
                  _resident((None, d, d), lambda i, j: (l, 0, 0))],
        out_specs=pl.BlockSpec((None, tm, d), lambda i, j: (i, j, 0)),
        out_shape=jax.ShapeDtypeStruct((b, t - skip * tm, d), F32),
        compiler_params=_params(("arbitrary", "arbitrary")),
        name="merge",
    )(ya, yb, yc, gates, gates, gates, h, mod, wb, wo)


def _ffn_kernel(h_ref, hp_ref, hn_ref, mod_ref, nw_ref, wup_ref, cw_ref, wdn_ref, o_ref, acc_ref,
                *, nct, ntiles):
    tm = h_ref.shape[0]
    j = pl.program_id(1)
    mod = mod_ref[...]
    nw = nw_ref[...]
    h = h_ref[...]
    xn = _modnorm(h, nw, mod[3:4], mod[4:5]).astype(BF16)
    halo = jnp.concatenate([hp_ref[...], hn_ref[...]], axis=0)
    xh = _modnorm(halo, nw, mod[3:4], mod[4:5]).astype(BF16)
    has_prev = jnp.logical_and(j != 0, j != nct).astype(F32)
    has_next = jnp.logical_and(j != nct - 1, j != ntiles - 1).astype(F32)
    rows = lax.broadcasted_iota(jnp.int32, (tm, 1), 0)
    first = rows == 0
    last = rows == tm - 1

    def cols(kind, jf):
        return slice(kind * D_FF + jf * FFN_TF, kind * D_FF + (jf + 1) * FFN_TF)

    x_all = jnp.concatenate([xn, xh], axis=0)

    def up(jf):
        us = [jnp.dot(x_all, wup_ref[:, cols(kind, jf)], preferred_element_type=F32) for kind in range(2)]
        return [(u[:tm], u[tm:]) for u in us]

    def conv(kind, jf, u, uh):
        cw = cw_ref[:, cols(kind, jf)]
        prev = jnp.where(first, uh[7:8] * has_prev, pltpu.roll(u, 1, axis=0))
        nxt = jnp.where(last, uh[8:9] * has_next, pltpu.roll(u, tm - 1, axis=0))
        return prev * cw[0:1] + u * cw[1:2] + nxt * cw[2:3]

    nj = D_FF // FFN_TF
    ahead = [up(jf) for jf in range(min(FFN_AHEAD, nj))]
    acts = []
    for jf in range(nj):
        if jf + FFN_AHEAD < nj:
            ahead.append(up(jf + FFN_AHEAD))
        cur = ahead.pop(0)
        acts.append((conv(0, jf, *cur[0]) * _silu(conv(1, jf, *cur[1]))).astype(BF16))
        if len(acts) == FFN_DOWN_GROUP or jf == nj - 1:
            r1 = (jf + 1) * FFN_TF
            part = jnp.dot(jnp.concatenate(acts, axis=1), wdn_ref[r1 - len(acts) * FFN_TF:r1, :],
                           preferred_element_type=F32)
            if r1 == len(acts) * FFN_TF:
                acc_ref[...] = part
            else:
                acc_ref[...] += part
            acts = []
    o_ref[...] = h + mod[5:6] * acc_ref[...]


def _ffn_call(h, mod, nw, wup, cw, wdn, l, tm, nct):
    b, t, d = h.shape
    ntiles = t // tm
    hb = tm // 8
    return pl.pallas_call(
        functools.partial(_ffn_kernel, nct=nct, ntiles=ntiles),
        grid=(b, ntiles),
        in_specs=[pl.BlockSpec((None, tm, d), lambda i, j: (i, j, 0)),
                  pl.BlockSpec((None, 8, d), lambda i, j: (i, jnp.maximum(j * hb - 1, 0), 0)),
                  pl.BlockSpec((None, 8, d), lambda i, j: (i, jnp.minimum((j + 1) * hb, t // 8 - 1), 0)),
                  pl.BlockSpec((None, None, None, 6, d), lambda i, j: (l, i, _mod_row(j, nct), 0, 0)),
                  pl.BlockSpec((None, 1, d), lambda i, j: (l, 0, 0)),
                  _resident((None,) + wup.shape[1:], lambda i, j: (l, 0, 0)),
                  _resident((None,) + cw.shape[1:], lambda i, j: (l, 0, 0)),
                  _resident((None,) + wdn.shape[1:], lambda i, j: (l, 0, 0))],
        out_specs=pl.BlockSpec((None, tm, d), lambda i, j: (i, j, 0)),
        out_shape=jax.ShapeDtypeStruct((b, t, d), F32),
        scratch_shapes=[pltpu.VMEM((tm, d), F32)],
        compiler_params=_params(("arbitrary", "arbitrary")),
        name="ffn",
    )(h, h, h, mod, nw, wup, cw, wdn)


def _rope_tables(tc, tl):
    rows = tl // GRID_W
    row = jnp.repeat(jnp.arange(rows, dtype=F32), GRID_W)
    col = jnp.tile(jnp.arange(GRID_W, dtype=F32), rows)
    n_freq = B_DH // 4
    inv = ROPE_BASE ** (-jnp.arange(n_freq, dtype=F32) / n_freq)
    ang_r = row[:, None] * inv
    ang_c = col[:, None] * inv
    cos = jnp.concatenate([jnp.cos(ang_r)] * 2 + [jnp.cos(ang_c)] * 2, axis=1)
    sin = jnp.concatenate([-jnp.sin(ang_r), jnp.sin(ang_r), -jnp.sin(ang_c), jnp.sin(ang_c)], axis=1)
    cos = jnp.concatenate([jnp.ones((tc, B_DH), F32), cos], axis=0)
    sin = jnp.concatenate([jnp.zeros((tc, B_DH), F32), sin], axis=0)
    return jnp.tile(cos, (1, 2)), jnp.tile(sin, (1, 2))


def _pack_w_in(w_in):
    depth, d, _ = w_in.shape
    w = w_in.astype(BF16)
    o = 0
    parts = {}
    for name, width in (("aq", 256), ("ak", 256), ("av", 512), ("ao", 512), ("ag", 16),
                        ("bq", 512), ("bk", 128), ("bv", 128),
                        ("cq", 512), ("ck", 512), ("cv", 512), ("cz", 512), ("ca", 8), ("cb", 8),
                        ("gate", 3 * D_MODEL)):
        parts[name] = w[:, :, o:o + width]
        o += width
    aqk = jnp.concatenate([parts["aq"].reshape(depth, d, A_HEADS, A_DQK),
                           parts["ak"].reshape(depth, d, A_HEADS, A_DQK)], axis=3).reshape(depth, d, 512)
    small = jnp.concatenate([parts["ag"], parts["ca"], parts["cb"]], axis=2)
    pad = jnp.zeros((depth, d, W_COLS - W_SMALL - small.shape[2]), BF16)
    cols = [parts["gate"], parts["av"], parts["ao"], parts["bq"], parts["bk"], parts["bv"],
            parts["cq"], parts["ck"], parts["cv"], parts["cz"], aqk, small, pad]
    return jnp.concatenate(cols, axis=2)


def _gate_layouts(x, heads, kinds):
    b, t, _ = x.shape
    g = x.reshape(b, t, kinds, heads)
    col = g.transpose(0, 3, 1, 2)
    row = g.reshape(b, t // CHUNK, CHUNK, kinds, heads).transpose(0, 4, 1, 3, 2)
    return col, row


def kernel(x, c, ctx, c_ctx, norm1_w, norm2_w, ada_w, ada_b, w_in, a_gate_b, a_norm_w, b_qnorm_w,
           b_knorm_w, c_conv_w, c_a_log, c_dt_bias, c_norm_w, w_branch, w_out, w_up, ffn_conv_w, w_down):
    b, tl, d = x.shape
    tc = ctx.shape[1]
    depth = w_in.shape[0]
    t = tc + tl
    tm = 256 if (tc % 256 == 0 and tl % 256 == 0) else 128
    nct = tc // tm
    ncc = tc // CHUNK
    nj = D_FF // FFN_TF

    cc = jnp.zeros((16, d), F32).at[:b].set(c).at[b].set(c_ctx)
    mods = _ada_call(cc, ada_w, ada_b).reshape(depth, 16, 6, d)
    mod = jnp.stack([jnp.broadcast_to(mods[:, b][:, None], (depth, b, 6, d)), mods[:, :b]], axis=2)

    w_in_p = _pack_w_in(w_in)
    wb = w_branch.astype(BF16)
    wo = w_out.astype(BF16)
    wup = w_up.astype(BF16)
    wdn = w_down.astype(BF16)
    cos2, sin2 = _rope_tables(tc, tl)
    qw2 = jnp.tile(b_qnorm_w, (1, 2))[:, None]
    kw2 = jnp.tile(b_knorm_w, (1, 2))[:, None]
    a_bcol = a_gate_b.transpose(0, 2, 1)[:, :, :, None]
    c_par = jnp.concatenate([c_a_log, c_dt_bias], axis=1).transpose(0, 2, 1)[:, :, None, :]

    h = jnp.concatenate([ctx, x], axis=1)
    for l in range(depth):
        last = l == depth - 1
        gates, p, bq, bk, qkt, small = _proj_call(h, mod, norm1_w[:, None], w_in_p, c_conv_w, qw2, kw2, cos2, sin2,
                                                  l, tm, nct)
        _, a_grow = _gate_layouts(small[:, :, :16], A_HEADS, 4)
        c_gcol, c_grow = _gate_layouts(small[:, :, 16:32], C_HEADS, 4)
        ya = _mlstm_call(p, qkt, a_grow, a_bcol, a_norm_w[:, None], l, ncc)
        yb = _gqa_call(bq, bk, p, tc, latent_only=last)
        yc = _gdn_call(p, c_gcol, c_grow, c_par, c_norm_w[:, None], l, ncc)
        h = _merge_call(ya, yb, yc, gates, h, mod, wb, wo, l, tm, nct, latent_only=last)
        h = _ffn_call(h, mod, norm2_w[:, None], wup, ffn_conv_w, wdn, l, tm, 0 if last else nct)
    return h
```

```python
import functools

import jax
import jax.numpy as jnp
from jax import lax
from jax.experimental import pallas as pl
from jax.experimental.pallas import tpu as pltpu

F32 = jnp.float32
BF16 = jnp.bfloat16

D_MODEL = 1024
GRID_W = 64
A_HEADS, A_DQK, A_DV = 4, 64, 128
B_HEADS, B_KV_HEADS, B_DH = 8, 2, 64
C_HEADS, C_DK, C_DV = 4, 128, 128
BRANCH_W = 512
D_FF = 2816
CHUNK = 64
ROPE_BASE = 10000.0
EPS = 1e-6
M_INIT = -1e30
LOG2_E = 1.4426950408889634
NEG = -1e30

LANES = 128
W_GATES = 3 * D_MODEL
W_AV = 3072
W_BQ = 4096
W_BK = 4608
W_CQ = 4864
W_CZ = 6400
W_AQK = 6912
W_SMALL = 7424
W_COLS = 7552
P_AV = 0
P_AO = 512
P_BV = 1024
P_CQ = 1152
P_CK = 1664
P_CV = 2176
P_CZ = 2688
P_COLS = 3200
PROJ_TN = 512
FFN_TF = 256
FFN_AHEAD = 3
FFN_DOWN_GROUP = 4
Q_TILE = 256
SCAN_GROUP = 12
MLSTM_GROUP = 18
VMEM_LIMIT = 56 * 1024 * 1024


def _dot(a, b):
    return jnp.dot(a.astype(BF16), b.astype(BF16), preferred_element_type=F32)


def _dot_nt(a, b):
    return lax.dot_general(a.astype(BF16), b.astype(BF16), (((1,), (1,)), ((), ())),
                           preferred_element_type=F32)


def _dot_tn(a, b):
    return lax.dot_general(a.astype(BF16), b.astype(BF16), (((0,), (0,)), ((), ())),
                           preferred_element_type=F32)


def _split(a):
    hi = a.astype(BF16)
    lo = (a - hi.astype(F32)).astype(BF16)
    return hi, lo


def _dot3(a, b):
    ah, al = _split(a)
    bh, bl = _split(b)
    d = functools.partial(jnp.dot, preferred_element_type=F32)
    return d(ah, bh) + (d(al, bh) + d(ah, bl))


def _sigmoid(x):
    return 1.0 / (1.0 + jnp.exp(-x))


def _silu(x):
    return x * _sigmoid(x)


def _softplus(x):
    return jnp.maximum(x, 0.0) + jnp.log1p(jnp.exp(-jnp.abs(x)))


def _logsigmoid(x):
    return -_softplus(-x)


def _modnorm(x, w, shift, scale):
    y = x * lax.rsqrt(jnp.mean(x * x, axis=-1, keepdims=True) + EPS)
    return (y * w) * (1.0 + scale) + shift


def _params(sem, vmem=VMEM_LIMIT):
    return pltpu.CompilerParams(dimension_semantics=sem, vmem_limit_bytes=vmem)


def _resident(shape, index_map):
    return pl.BlockSpec(shape, index_map, pipeline_mode=pl.Buffered(1))


def _ada_kernel(c_ref, w_ref, b_ref, o_ref):
    o_ref[...] = _dot3(_silu(c_ref[...]), w_ref[...]) + b_ref[...]


def _ada_call(cc, ada_w, ada_b):
    depth, d, n = ada_w.shape
    tn = 1536
    return pl.pallas_call(
        _ada_kernel,
        grid=(depth, n // tn),
        in_specs=[pl.BlockSpec(cc.shape, lambda l, j: (0, 0)),
                  pl.BlockSpec((None, d, tn), lambda l, j: (l, 0, j)),
                  pl.BlockSpec((None, 1, tn), lambda l, j: (l, 0, j))],
        out_specs=pl.BlockSpec((None, cc.shape[0], tn), lambda l, j: (l, 0, j)),
        out_shape=jax.ShapeDtypeStruct((depth, cc.shape[0], n), F32),
        compiler_params=_params(("arbitrary", "arbitrary")),
        name="ada",
    )(cc, ada_w, ada_b.reshape(depth, 1, n))


def _proj_kernel(h_ref, hp_ref, hn_ref, mod_ref, nw_ref, w_ref, cw_ref, qw_ref, kw_ref, cos_ref, sin_ref,
                 g_ref, p_ref, bq_ref, bk_ref, qkt_ref, small_ref, *, nct, ntiles):
    tm = h_ref.shape[0]
    j = pl.program_id(1)
    mod = mod_ref[...]
    nw = nw_ref[...]
    xn = _modnorm(h_ref[...], nw, mod[0:1], mod[1:2]).astype(BF16)

    def proj(c0, width):
        return jnp.dot(xn, w_ref[:, c0:c0 + width], preferred_element_type=F32)

    for c0 in range(0, W_GATES, PROJ_TN):
        g_ref[:, c0:c0 + PROJ_TN] = proj(c0, PROJ_TN).astype(BF16)
    for c0 in range(0, P_BV, PROJ_TN):
        p_ref[:, c0:c0 + PROJ_TN] = proj(W_AV + c0, PROJ_TN)
    p_ref[:, P_CZ:P_COLS] = proj(W_CZ, P_COLS - P_CZ)

    r = lax.broadcasted_iota(jnp.int32, (LANES, LANES), 0)
    c = lax.broadcasted_iota(jnp.int32, (LANES, LANES), 1)
    bd = jnp.where(r // B_DH == c // B_DH, 1.0, 0.0).astype(BF16)
    lane_half = (lax.broadcasted_iota(jnp.int32, (tm, LANES), 1) % 32) < 16
    cos, sin = cos_ref[...], sin_ref[...]
    bq = proj(W_BQ, B_HEADS * B_DH)
    for s0 in range(0, B_HEADS * B_DH, LANES):
        y = _headnorm_rope(bq[:, s0:s0 + LANES], bd, qw_ref[...], cos, sin, lane_half)
        bq_ref[:, s0:s0 + LANES] = (y * (B_DH ** -0.5 * LOG2_E)).astype(BF16)
    kv = proj(W_BK, 2 * LANES)
    bk_ref[...] = _headnorm_rope(kv[:, :LANES], bd, kw_ref[...], cos, sin, lane_half).astype(BF16)
    p_ref[:, P_BV:P_BV + LANES] = kv[:, LANES:]

    halo = jnp.concatenate([hp_ref[...], hn_ref[...]], axis=0)
    x_all = jnp.concatenate([xn, _modnorm(halo, nw, mod[0:1], mod[1:2]).astype(BF16)], axis=0)
    has_prev = jnp.logical_and(j != 0, j != nct).astype(F32)
    has_next = jnp.logical_and(j != nct - 1, j != ntiles - 1).astype(F32)
    rows = lax.broadcasted_iota(jnp.int32, (tm, 1), 0)
    width = C_HEADS * C_DK
    for part in range(3):
        c0 = part * width
        u_all = jnp.dot(x_all, w_ref[:, W_CQ + c0:W_CQ + c0 + width], preferred_element_type=F32)
        u = u_all[:tm]
        cw = cw_ref[:, c0:c0 + width]
        prev = jnp.where(rows == 0, u_all[tm + 7:tm + 8] * has_prev, pltpu.roll(u, 1, axis=0))
        nxt = jnp.where(rows == tm - 1, u_all[tm + 8:tm + 9] * has_next, pltpu.roll(u, tm - 1, axis=0))
        y = _silu(prev * cw[0:1] + u * cw[1:2] + nxt * cw[2:3])
        for hd in range(C_HEADS):
            yh = y[:, hd * C_DK:(hd + 1) * C_DK]
            if part < 2:
                yh = yh * lax.rsqrt(jnp.sum(yh * yh, axis=-1, keepdims=True) + EPS)
            if part == 0:
                yh = yh * (C_DK ** -0.5)
            p_ref[:, P_CQ + c0 + hd * C_DK:P_CQ + c0 + (hd + 1) * C_DK] = yh
    qk = jnp.dot(xn, w_ref[:, W_AQK:W_SMALL], preferred_element_type=F32)
    for hd in range(A_HEADS):
        for c in range(tm // CHUNK):
            qkt_ref[hd, c] = qk[c * CHUNK:(c + 1) * CHUNK, hd * LANES:(hd + 1) * LANES].T
    small_ref[...] = jnp.dot(xn, w_ref[:, W_SMALL:W_COLS], preferred_element_type=F32)


def _proj_call(h, mod, nw, w, cw, qw, kw, cos2, sin2, l, tm, nct):
    b, t, d = h.shape
    cpt = tm // CHUNK
    ntiles = t // tm
    hb = tm // 8
    vec = lambda: pl.BlockSpec((None, 1, LANES), lambda i, j: (l, 0, 0))
    table = lambda: pl.BlockSpec((tm, LANES), lambda i, j: (j, 0))
    return pl.pallas_call(
        functools.partial(_proj_kernel, nct=nct, ntiles=ntiles),
        grid=(b, ntiles),
        in_specs=[pl.BlockSpec((None, tm, d), lambda i, j: (i, j, 0)),
                  pl.BlockSpec((None, 8, d), lambda i, j: (i, jnp.maximum(j * hb - 1, 0), 0)),
                  pl.BlockSpec((None, 8, d), lambda i, j: (i, jnp.minimum((j + 1) * hb, t // 8 - 1), 0)),
                  pl.BlockSpec((None, None, None, 6, d), lambda i, j: (l, i, _mod_row(j, nct), 0, 0)),
                  pl.BlockSpec((None, 1, d), lambda i, j: (l, 0, 0)),
                  _resident((None, d, W_COLS), lambda i, j: (l, 0, 0)),
                  _resident((None,) + cw.shape[1:], lambda i, j: (l, 0, 0)),
                  vec(), vec(), table(), table()],
        out_specs=[pl.BlockSpec((None, tm, W_GATES), lambda i, j: (i, j, 0)),
                   pl.BlockSpec((None, tm, P_COLS), lambda i, j: (i, j, 0)),
                   pl.BlockSpec((None, tm, B_HEADS * B_DH), lambda i, j: (i, j, 0)),
                   pl.BlockSpec((None, tm, LANES), lambda i, j: (i, j, 0)),
                   pl.BlockSpec((None, A_HEADS, cpt, 2 * A_DQK, CHUNK), lambda i, j: (i, 0, j, 0, 0)),
                   pl.BlockSpec((None, tm, LANES), lambda i, j: (i, j, 0))],
        out_shape=[jax.ShapeDtypeStruct((b, t, W_GATES), BF16),
                   jax.ShapeDtypeStruct((b, t, P_COLS), F32),
                   jax.ShapeDtypeStruct((b, t, B_HEADS * B_DH), BF16),
                   jax.ShapeDtypeStruct((b, t, LANES), BF16),
                   jax.ShapeDtypeStruct((b, A_HEADS, t // CHUNK, 2 * A_DQK, CHUNK), F32),
                   jax.ShapeDtypeStruct((b, t, LANES), F32)],
        compiler_params=_params(("arbitrary", "arbitrary")),
        name="proj_in",
    )(h, h, h, mod, nw, w, cw, qw, kw, cos2, sin2)


def _chunk_masks():
    row = lax.broadcasted_iota(jnp.int32, (CHUNK, CHUNK), 0)
    col = lax.broadcasted_iota(jnp.int32, (CHUNK, CHUNK), 1)
    return col <= row, col >= row, col < row, col > row


def _bwd_chunk(it, nc, ncc):
    return jnp.where(it < ncc, ncc - 1 - it, nc - 1 - (it - ncc))


def _rows3(x, n_rows=16):
    hi = x.astype(BF16).astype(F32)
    mid = (x - hi).astype(BF16).astype(F32)
    lo = ((x - hi) - mid).astype(BF16).astype(F32)
    r = lax.broadcasted_iota(jnp.int32, (n_rows, x.shape[1]), 0)
    return jnp.where(r == 0, hi, jnp.where(r == 1, mid, jnp.where(r == 2, lo, 0.0))).astype(BF16)


def _mlstm_kernel(qkt_ref, v_ref, o_ref, grow_ref, bcol_ref, nw_ref, y_ref, hf_ref, hb_ref, *, nc, ncc):
    L = CHUNK
    le, ge, _, _ = _chunk_masks()
    vis = (ge, le)
    row_l = lax.broadcasted_iota(jnp.int32, (L, LANES), 0)
    col_l = lax.broadcasted_iota(jnp.int32, (L, LANES), 1)
    ones_l = jnp.ones((L, LANES), BF16)
    zeros_l = jnp.zeros((L, LANES), BF16)
    stat_rhs = [jnp.concatenate(
        [ones_l, jnp.where((col_l < L) & ((col_l >= row_l) if d == 0 else (col_l <= row_l)), 1.0, 0.0).astype(BF16)],
        axis=1) for d in range(2)]
    ones3 = jnp.where(lax.broadcasted_iota(jnp.int32, (16, LANES), 0) < 3, 1.0, 0.0).astype(BF16)
    tail_rhs = jnp.concatenate([jnp.zeros((16, 2 * LANES), BF16), ones3], axis=1)
    bias_col = bcol_ref[...]
    grp = _group_size(nc, MLSTM_GROUP)
    h_refs = (hf_ref, hb_ref)

    def sum3(x):
        return x[0:1] + x[1:2] + x[2:3]

    def body(gi, carry):
        C = [carry[0], carry[2]]
        m = [carry[1], carry[3]]
        chains = []
        for j in range(grp):
            it = gi * grp + j
            for d, c in ((0, it), (1, _bwd_chunk(it, nc, ncc))):
                gr = grow_ref[c] + bias_col
                x = qkt_ref[c]
                chains.append(dict(d=d, r0=pl.multiple_of(c * L, L), i_row=gr[2 * d:2 * d + 1, :],
                                   f3=_rows3(_logsigmoid(gr[2 * d + 1:2 * d + 2, :])),
                                   qt=x[:A_DQK] * (A_DQK ** -0.5), kt=x[A_DQK:]))
        for ch in chains:
            ch["v"] = v_ref[pl.ds(ch["r0"], L), :].astype(BF16)
            ch["sqt"] = _dot_tn(ch["kt"], ch["qt"])
        for ch in chains:
            st = jnp.dot(ch["f3"], stat_rhs[ch["d"]], preferred_element_type=F32)
            ch["btot"] = sum3(st[:, :LANES])
            ch["bcum"] = sum3(st[:, LANES:LANES + L])
            ch["a_row"] = ch["i_row"] - ch["bcum"]
        for ch in chains:
            ch["a_col"] = lax.dot_general(_rows3(ch["a_row"]), ones3, (((0,), (0,)), ((), ())),
                                          preferred_element_type=F32)
        for ch in chains:
            d = ch["d"]
            m_new = jnp.maximum(ch["btot"] + m[d], ch["btot"] + jnp.max(ch["a_col"], axis=0, keepdims=True))
            dlog = jnp.where(vis[d], ch["bcum"] + ch["a_col"][:, :L], NEG)
            inter = ch["bcum"] + m[d][:, :L]
            mt = jnp.maximum(inter, jnp.max(dlog, axis=0, keepdims=True))
            ch["st"] = (ch["sqt"] * jnp.exp(dlog - mt)).astype(BF16)
            ch["qe"] = (ch["qt"] * jnp.exp(inter - mt)).astype(BF16)
            ch["e3"] = _rows3(jnp.exp(-mt))
            ch["kw"] = (ch["kt"] * jnp.exp(ch["btot"][:, :L] + ch["a_row"] - m_new[:, :L])).astype(BF16)
            decay = jnp.exp(ch["btot"] + m[d] - m_new)
            ch["decay"] = jnp.concatenate([decay, decay], axis=1)
            m[d] = m_new
        for ch in chains:
            ch["kv"] = jnp.dot(ch["kw"], jnp.concatenate([ch["v"], ones_l], axis=1), preferred_element_type=F32)
            lhs = jnp.concatenate([ch["st"], ch["e3"]], axis=0)
            rhs = jnp.concatenate([jnp.concatenate([ch["v"], ones_l, zeros_l], axis=1), tail_rhs], axis=0)
            ch["intra"] = lax.dot_general(lhs, rhs, (((0,), (0,)), ((), ())), preferred_element_type=F32)
        for ch in chains:
            d = ch["d"]
            ch["inter"] = lax.dot_general(ch["qe"], C[d].astype(BF16), (((0,), (0,)), ((), ())),
                                          preferred_element_type=F32)
            C[d] = ch["decay"] * C[d] + ch["kv"]
        for ch in chains:
            num = ch["intra"][:, :A_DV] + ch["inter"][:, :A_DV]
            den = ch["intra"][:, A_DV:2 * A_DV] + ch["inter"][:, A_DV:]
            h_refs[ch["d"]][pl.ds(ch["r0"], L), :] = num / jnp.maximum(jnp.abs(den), ch["intra"][:, 2 * A_DV:])
        return C[0], m[0], C[1], m[1]

    c0 = jnp.zeros((A_DQK, 2 * LANES), F32)
    m0 = jnp.full((1, LANES), M_INIT, F32)
    lax.fori_loop(0, nc // grp, body, (c0, m0, c0, m0))

    hsum = hf_ref[...] + hb_ref[...]
    hn = hsum * lax.rsqrt(jnp.mean(hsum * hsum, axis=-1, keepdims=True) + EPS)
    y_ref[...] = (hn * nw_ref[...] * _sigmoid(o_ref[...])).astype(y_ref.dtype)


def _mlstm_call(p, qkt, grow, bcol, nw, l, ncc):
    b, t, _ = p.shape
    nc = t // CHUNK
    blk = lambda base: pl.BlockSpec((None, t, LANES), lambda i, h: (i, 0, base // LANES + h))
    return pl.pallas_call(
        functools.partial(_mlstm_kernel, nc=nc, ncc=ncc),
        grid=(b, A_HEADS),
        in_specs=[pl.BlockSpec((None, None, nc, 2 * A_DQK, CHUNK), lambda i, h: (i, h, 0, 0, 0)),
                  blk(P_AV), blk(P_AO),
                  pl.BlockSpec((None, None, nc, 4, CHUNK), lambda i, h: (i, h, 0, 0, 0)),
                  pl.BlockSpec((None, None, 4, 1), lambda i, h: (l, h, 0, 0)),
                  pl.BlockSpec((None, 1, LANES), lambda i, h: (l, 0, h))],
        out_specs=pl.BlockSpec((None, t, LANES), lambda i, h: (i, 0, h)),
        out_shape=jax.ShapeDtypeStruct((b, t, BRANCH_W), BF16),
        scratch_shapes=[pltpu.VMEM((t, LANES), F32), pltpu.VMEM((t, LANES), F32)],
        compiler_params=_params(("arbitrary", "arbitrary")),
        name="mlstm",
    )(qkt, p, p, grow, bcol, nw)


def _headnorm_rope(x, bd, w, cos, sin_signed, lane_half):
    hi, lo = _split(x * x)
    ssum = jnp.dot(hi, bd, preferred_element_type=F32) + jnp.dot(lo, bd, preferred_element_type=F32)
    y = x * lax.rsqrt(ssum * (1.0 / B_DH) + EPS) * w
    swapped = jnp.where(lane_half, pltpu.roll(y, LANES - 16, axis=1), pltpu.roll(y, 16, axis=1))
    return y * cos + swapped * sin_signed


def _gqa_kernel(q_ref, k_ref, v_ref, y_ref, vb_ref, *, tc, first_q):
    tq = q_ref.shape[0]
    t = k_ref.shape[0]
    qb = pl.program_id(1) + first_q

    @pl.when(pl.program_id(1) == 0)
    def _():
        v = v_ref[...]
        first = lax.broadcasted_iota(jnp.int32, (t, LANES), 1) < B_DH
        vb_ref[0] = jnp.where(first, v, 1.0).astype(BF16)
        vb_ref[1] = jnp.where(first, pltpu.roll(v, B_DH, axis=1), 1.0).astype(BF16)

    group = B_HEADS // B_KV_HEADS

    def attend(nk):
        scores = []
        for kvh in range(B_KV_HEADS):
            kh = k_ref[0:nk, kvh * B_DH:(kvh + 1) * B_DH]
            qs = jnp.concatenate([q_ref[:, h * B_DH:(h + 1) * B_DH]
                                  for h in range(kvh * group, (kvh + 1) * group)], axis=0)
            scores.append(lax.dot_general(qs, kh, (((1,), (1,)), ((), ())), preferred_element_type=F32))
        for kvh, s in enumerate(scores):
            p = jnp.exp2(s - jnp.max(s, axis=-1, keepdims=True))
            oe = jnp.dot(p.astype(BF16), vb_ref[kvh, 0:nk, :], preferred_element_type=F32)
            o = oe * pltpu.roll(1.0 / oe, B_DH, axis=1)
            for g in range(group):
                h = kvh * group + g
                y_ref[:, h * B_DH:(h + 1) * B_DH] = o[g * tq:(g + 1) * tq, :B_DH].astype(y_ref.dtype)

    nqc = tc // tq
    if first_q < nqc:
        @pl.when(qb < nqc)
        def _():
            attend(tc)

    @pl.when(qb >= nqc)
    def _():
        attend(t)


def _gqa_call(bq, bk, p, tc, latent_only):
    b, t, _ = p.shape
    tq = Q_TILE if (tc % Q_TILE == 0 and t % Q_TILE == 0) else Q_TILE // 2
    first_q = tc // tq if latent_only else 0
    return pl.pallas_call(
        functools.partial(_gqa_kernel, tc=tc, first_q=first_q),
        grid=(b, t // tq - first_q),
        in_specs=[pl.BlockSpec((None, tq, B_HEADS * B_DH), lambda i, j: (i, j + first_q, 0)),
                  pl.BlockSpec((None, t, LANES), lambda i, j: (i, 0, 0)),
                  pl.BlockSpec((None, t, LANES), lambda i, j: (i, 0, P_BV // LANES))],
        out_specs=pl.BlockSpec((None, tq, BRANCH_W), lambda i, j: (i, j, 0)),
        out_shape=jax.ShapeDtypeStruct((b, t - first_q * tq, BRANCH_W), BF16),
        scratch_shapes=[pltpu.VMEM((B_KV_HEADS, t, LANES), BF16)],
        compiler_params=_params(("arbitrary", "arbitrary")),
        name="gqa",
    )(bq, bk, p)


def _group_size(nc, limit=None):
    return max(g for g in range(1, (limit or SCAN_GROUP) + 1) if nc % g == 0)


def _gdn_kernel(qn_ref, kn_ref, vn_ref, z_ref, gcol_ref, grow_ref, par_ref, nw_ref, y_ref,
                n_ref, kq_ref, egl_ref, o_ref, *, nc, ncc):
    L = CHUNK
    le, ge, lt, gt = _chunk_masks()
    par = par_ref[...]

    grp = _group_size(nc)
    ng = nc // grp
    eye = jnp.where(le & ge, 1.0, 0.0).astype(F32)
    rowi = lax.broadcasted_iota(jnp.int32, (L, L), 0)
    coli = lax.broadcasted_iota(jnp.int32, (L, L), 1)
    blk = {b_: (rowi // (2 * b_) == coli // (2 * b_)) & (rowi // b_ != coli // b_) for b_ in (1, 2, 4, 8, 16, 32)}

    def prep_stages(gi):
        chains = []
        for j in range(grp):
            it = gi * grp + j
            for d, c in ((0, it), (1, _bwd_chunk(it, nc, ncc))):
                r0 = pl.multiple_of(c * L, L)
                chains.append(dict(it=it, d=d, c=c, r0=r0, q=qn_ref[pl.ds(r0, L), :],
                                   k=kn_ref[pl.ds(r0, L), :], v=vn_ref[pl.ds(r0, L), :]))
        for ch in chains:
            ch["kkqk"] = _dot_nt(jnp.concatenate([ch["k"], ch["q"]], axis=0), ch["k"])
        yield
        for ch in chains:
            d = ch["d"]
            gc = gcol_ref[pl.ds(ch["r0"], L), :]
            gr = grow_ref[ch["c"]]
            mask_in, mask_t, strict = (le, ge, lt) if d == 0 else (ge, le, gt)
            neg_rate = -jnp.exp(par[:, d:d + 1])
            g_col = neg_rate * _softplus(gc[:, d:d + 1] + par[:, 2 + d:3 + d])
            g_row = neg_rate * _softplus(gr[d:d + 1, :] + par[:, 2 + d:3 + d])
            beta = _sigmoid(gc[:, 2 + d:3 + d])
            G_col = jnp.sum(jnp.where(mask_in, g_row, 0.0), axis=1, keepdims=True)
            G_row = jnp.sum(jnp.where(mask_t, g_col, 0.0), axis=0, keepdims=True)
            dec = jnp.exp(jnp.where(mask_in, G_col - G_row, NEG))
            g_last = jnp.sum(g_row, axis=1, keepdims=True)
            ch.update(beta=beta, dec=dec, eG=jnp.exp(G_col), kscale=jnp.exp(g_last - G_col),
                      egl=jnp.exp(g_last), x=-jnp.where(strict, beta * ch["kkqk"][:L] * dec, 0.0))
        invs = [eye + jnp.where(blk[1], ch["x"], 0.0) for ch in chains]
        bsz = 2
        while bsz < L:
            offs = [jnp.where(blk[bsz], ch["x"], 0.0) for ch in chains]
            tmps = [_dot(off, inv) for off, inv in zip(offs, invs)]
            yield
            invs = [inv + _dot(inv, tmp) for inv, tmp in zip(invs, tmps)]
            yield
            bsz *= 2
        uws = []
        for ch, inv in zip(chains, invs):
            rhs = jnp.concatenate([ch["v"] * ch["beta"], ch["k"] * (ch["beta"] * ch["eG"])], axis=1)
            uws.append(_dot(inv, rhs))
        yield
        kns = [_dot_tn(ch["k"] * ch["kscale"], uw) for ch, uw in zip(chains, uws)]
        yield
        qos = [_dot(ch["kkqk"][L:] * ch["dec"], uw) for ch, uw in zip(chains, uws)]
        yield
        for ch, kn, qo in zip(chains, kns, qos):
            d, it = ch["d"], ch["it"]
            n_ref[d, it] = kn[:, :C_DV]
            kq_ref[d, it, 0:C_DK, :] = kn[:, C_DV:].astype(BF16)
            kq_ref[d, it, C_DK:C_DK + L, :] = (ch["q"] * ch["eG"] - qo[:, C_DV:]).astype(BF16)
            o_ref[d, pl.ds(ch["r0"], L), :] = qo[:, :C_DV]
            egl_ref[d, it] = jnp.broadcast_to(ch["egl"], (8, LANES))

    def recur(it, S):
        out = []
        for d, c in ((0, it), (1, _bwd_chunk(it, nc, ncc))):
            r0 = pl.multiple_of(c * L, L)
            ks_qs = jnp.dot(kq_ref[d, it], S[d].astype(BF16), preferred_element_type=F32)
            o_ref[d, pl.ds(r0, L), :] += ks_qs[C_DK:]
            out.append(S[d] * egl_ref[d, it][0:1, 0:1] + (n_ref[d, it] - ks_qs[:C_DK]))
        return out

    for _ in prep_stages(0):
        pass

    def body(gi, carry):
        S = list(carry)
        stages = prep_stages(gi)
        for j in range(grp):
            next(stages, None)
            S = recur((gi - 1) * grp + j, S)
        for _ in stages:
            pass
        return tuple(S)

    s0 = jnp.zeros((C_DK, C_DV), F32)
    S = list(lax.fori_loop(1, ng, body, (s0, s0)))
    for j in range(grp):
        S = recur((ng - 1) * grp + j, S)

    osum = o_ref[0] + o_ref[1]
    on = osum * lax.rsqrt(jnp.mean(osum * osum, axis=-1, keepdims=True) + EPS) * nw_ref[...]
    y_ref[...] = (on * _silu(z_ref[...])).astype(y_ref.dtype)


def _gdn_call(p, gcol, grow, par, nw, l, ncc):
    b, t, _ = p.shape
    nc = t // CHUNK
    blk = lambda base: pl.BlockSpec((None, t, LANES), lambda i, h: (i, 0, base // LANES + h))
    big = lambda dt: pltpu.VMEM((2, t, LANES), dt)
    return pl.pallas_call(
        functools.partial(_gdn_kernel, nc=nc, ncc=ncc),
        grid=(b, C_HEADS),
        in_specs=[blk(P_CQ), blk(P_CK), blk(P_CV), blk(P_CZ),
                  pl.BlockSpec((None, None, t, 4), lambda i, h: (i, h, 0, 0)),
                  pl.BlockSpec((None, None, nc, 4, CHUNK), lambda i, h: (i, h, 0, 0, 0)),
                  pl.BlockSpec((None, None, 1, 4), lambda i, h: (l, h, 0, 0)),
                  pl.BlockSpec((None, 1, LANES), lambda i, h: (l, 0, 0))],
        out_specs=pl.BlockSpec((None, t, LANES), lambda i, h: (i, 0, h)),
        out_shape=jax.ShapeDtypeStruct((b, t, BRANCH_W), BF16),
        scratch_shapes=[pltpu.VMEM((2, nc, C_DK, C_DV), F32),
                        pltpu.VMEM((2, nc, C_DK + CHUNK, C_DV), BF16),
                        pltpu.VMEM((2, nc, 8, LANES), F32),
                        big(F32)],
        compiler_params=_params(("arbitrary", "arbitrary")),
        name="gdn",
    )(p, p, p, p, gcol, grow, par, nw)


def _merge_kernel(ya_ref, yb_ref, yc_ref, ga_ref, gb_ref, gc_ref, h_ref, mod_ref, wb_ref, wo_ref, o_ref):
    y = (_sigmoid(ga_ref[...].astype(F32)) * _dot(ya_ref[...], wb_ref[0])
         + _sigmoid(gb_ref[...].astype(F32)) * _dot(yb_ref[...], wb_ref[1])
         + _sigmoid(gc_ref[...].astype(F32)) * _dot(yc_ref[...], wb_ref[2]))
    o_ref[...] = h_ref[...] + mod_ref[2:3, :] * _dot(y, wo_ref[...])


def _mod_row(j, nct):
    return 1 if nct == 0 else jnp.minimum(j // nct, 1)


def _merge_call(ya, yb, yc, gates, h, mod, wb, wo, l, tm, nct, latent_only):
    b, t, d = h.shape
    skip = nct if latent_only else 0
    yblk = lambda: pl.BlockSpec((None, tm, BRANCH_W), lambda i, j: (i, j + skip, 0))
    yb_skip = skip - (t - yb.shape[1]) // tm
    gblk = lambda g: pl.BlockSpec((None, tm, d), lambda i, j: (i, j + skip, g))
    return pl.pallas_call(
        _merge_kernel,
        grid=(b, t // tm - skip),
        in_specs=[yblk(), pl.BlockSpec((None, tm, BRANCH_W), lambda i, j: (i, j + yb_skip, 0)), yblk(),
                  gblk(0), gblk(1), gblk(2),
                  pl.BlockSpec((None, tm, d), lambda i, j: (i, j + skip, 0)),
                  pl.BlockSpec((None, None, None, 6, d), lambda i, j: (l, i, _mod_row(j + skip, nct), 0, 0)),
                  _resident((None, 3, BRANCH_W, d), lambda i, j: (l, 0, 0, 0)),
                  _resident((None, d, d), lambda i, j: (l, 0, 0))],
        out_specs=pl.BlockSpec((None, tm, d), lambda i, j: (i, j, 0)),
        out_shape=jax.ShapeDtypeStruct((b, t - skip * tm, d), F32),
        compiler_params=_params(("arbitrary", "arbitrary")),
        name="merge",
    )(ya, yb, yc, gates, gates, gates, h, mod, wb, wo)


def _ffn_kernel(h_ref, hp_ref, hn_ref, mod_ref, nw_ref, wup_ref, cw_ref, wdn_ref, o_ref, acc_ref,
                *, nct, ntiles):
    tm = h_ref.shape[0]
    j = pl.program_id(1)
    mod = mod_ref[...]
    nw = nw_ref[...]
    h = h_ref[...]
    xn = _modnorm(h, nw, mod[3:4], mod[4:5]).astype(BF16)
    halo = jnp.concatenate([hp_ref[...], hn_ref[...]], axis=0)
    xh = _modnorm(halo, nw, mod[3:4], mod[4:5]).astype(BF16)
    has_prev = jnp.logical_and(j != 0, j != nct).astype(F32)
    has_next = jnp.logical_and(j != nct - 1, j != ntiles - 1).astype(F32)
    rows = lax.broadcasted_iota(jnp.int32, (tm, 1), 0)
    first = rows == 0
    last = rows == tm - 1

    def cols(kind, jf):
        return slice(kind * D_FF + jf * FFN_TF, kind * D_FF + (jf + 1) * FFN_TF)

    x_all = jnp.concatenate([xn, xh], axis=0)

    def up(jf):
        us = [jnp.dot(x_all, wup_ref[:, cols(kind, jf)], preferred_element_type=F32) for kind in range(2)]
        return [(u[:tm], u[tm:]) for u in us]

    def conv(kind, jf, u, uh):
        cw = cw_ref[:, cols(kind, jf)]
        prev = jnp.where(first, uh[7:8] * has_prev, pltpu.roll(u, 1, axis=0))
        nxt = jnp.where(last, uh[8:9] * has_next, pltpu.roll(u, tm - 1, axis=0))
        return prev * cw[0:1] + u * cw[1:2] + nxt * cw[2:3]

    nj = D_FF // FFN_TF
    ahead = [up(jf) for jf in range(min(FFN_AHEAD, nj))]
    acts = []
    for jf in range(nj):
        if jf + FFN_AHEAD < nj:
            ahead.append(up(jf + FFN_AHEAD))
        cur = ahead.pop(0)
        acts.append((conv(0, jf, *cur[0]) * _silu(conv(1, jf, *cur[1]))).astype(BF16))
        if len(acts) == FFN_DOWN_GROUP or jf == nj - 1:
            r1 = (jf + 1) * FFN_TF
            part = jnp.dot(jnp.concatenate(acts, axis=1), wdn_ref[r1 - len(acts) * FFN_TF:r1, :],
                           preferred_element_type=F32)
            if r1 == len(acts) * FFN_TF:
                acc_ref[...] = part
            else:
                acc_ref[...] += part
            acts = []
    o_ref[...] = h + mod[5:6] * acc_ref[...]


def _ffn_call(h, mod, nw, wup, cw, wdn, l, tm, nct):
    b, t, d = h.shape
    ntiles = t // tm
    hb = tm // 8
    return pl.pallas_call(
        functools.partial(_ffn_kernel, nct=nct, ntiles=ntiles),
        grid=(b, ntiles),
        in_specs=[pl.BlockSpec((None, tm, d), lambda i, j: (i, j, 0)),
                  pl.BlockSpec((None, 8, d), lambda i, j: (i, jnp.maximum(j * hb - 1, 0), 0)),
                  pl.BlockSpec((None, 8, d), lambda i, j: (i, jnp.minimum((j + 1) * hb, t // 8 - 1), 0)),
                  pl.BlockSpec((None, None, None, 6, d), lambda i, j: (l, i, _mod_row(j, nct), 0, 0)),
                  pl.BlockSpec((None, 1, d), lambda i, j: (l, 0, 0)),
                  _resident((None,) + wup.shape[1:], lambda i, j: (l, 0, 0)),
                  _resident((None,) + cw.shape[1:], lambda i, j: (l, 0, 0)),
                  _resident((None,) + wdn.shape[1:], lambda i, j: (l, 0, 0))],
        out_specs=pl.BlockSpec((None, tm, d), lambda i, j: (i, j, 0)),
        out_shape=jax.ShapeDtypeStruct((b, t, d), F32),
        scratch_shapes=[pltpu.VMEM((tm, d), F32)],
        compiler_params=_params(("arbitrary", "arbitrary")),
        name="ffn",
    )(h, h, h, mod, nw, wup, cw, wdn)


def _rope_tables(tc, tl):
    rows = tl // GRID_W
    row = jnp.repeat(jnp.arange(rows, dtype=F32), GRID_W)
    col = jnp.tile(jnp.arange(GRID_W, dtype=F32), rows)
    n_freq = B_DH // 4
    inv = ROPE_BASE ** (-jnp.arange(n_freq, dtype=F32) / n_freq)
    ang_r = row[:, None] * inv
    ang_c = col[:, None] * inv
    cos = jnp.concatenate([jnp.cos(ang_r)] * 2 + [jnp.cos(ang_c)] * 2, axis=1)
    sin = jnp.concatenate([-jnp.sin(ang_r), jnp.sin(ang_r), -jnp.sin(ang_c), jnp.sin(ang_c)], axis=1)
    cos = jnp.concatenate([jnp.ones((tc, B_DH), F32), cos], axis=0)
    sin = jnp.concatenate([jnp.zeros((tc, B_DH), F32), sin], axis=0)
    return jnp.tile(cos, (1, 2)), jnp.tile(sin, (1, 2))


def _pack_w_in(w_in):
    depth, d, _ = w_in.shape
    w = w_in.astype(BF16)
    o = 0
    parts = {}
    for name, width in (("aq", 256), ("ak", 256), ("av", 512), ("ao", 512), ("ag", 16),
                        ("bq", 512), ("bk", 128), ("bv", 128),
                        ("cq", 512), ("ck", 512), ("cv", 512), ("cz", 512), ("ca", 8), ("cb", 8),
                        ("gate", 3 * D_MODEL)):
        parts[name] = w[:, :, o:o + width]
        o += width
    aqk = jnp.concatenate([parts["aq"].reshape(depth, d, A_HEADS, A_DQK),
                           parts["ak"].reshape(depth, d, A_HEADS, A_DQK)], axis=3).reshape(depth, d, 512)
    small = jnp.concatenate([parts["ag"], parts["ca"], parts["cb"]], axis=2)
    pad = jnp.zeros((depth, d, W_COLS - W_SMALL - small.shape[2]), BF16)
    cols = [parts["gate"], parts["av"], parts["ao"], parts["bq"], parts["bk"], parts["bv"],
            parts["cq"], parts["ck"], parts["cv"], parts["cz"], aqk, small, pad]
    return jnp.concatenate(cols, axis=2)


def _gate_layouts(x, heads, kinds):
    b, t, _ = x.shape
    g = x.reshape(b, t, kinds, heads)
    col = g.transpose(0, 3, 1, 2)
    row = g.reshape(b, t // CHUNK, CHUNK, kinds, heads).transpose(0, 4, 1, 3, 2)
    return col, row


def kernel(x, c, ctx, c_ctx, norm1_w, norm2_w, ada_w, ada_b, w_in, a_gate_b, a_norm_w, b_qnorm_w,
           b_knorm_w, c_conv_w, c_a_log, c_dt_bias, c_norm_w, w_branch, w_out, w_up, ffn_conv_w, w_down):
    b, tl, d = x.shape
    tc = ctx.shape[1]
    depth = w_in.shape[0]
    t = tc + tl
    tm = 256 if (tc % 256 == 0 and tl % 256 == 0) else 128
    nct = tc // tm
    ncc = tc // CHUNK
    nj = D_FF // FFN_TF

    cc = jnp.zeros((16, d), F32).at[:b].set(c).at[b].set(c_ctx)
    mods = _ada_call(cc, ada_w, ada_b).reshape(depth, 16, 6, d)
    mod = jnp.stack([jnp.broadcast_to(mods[:, b][:, None], (depth, b, 6, d)), mods[:, :b]], axis=2)

    w_in_p = _pack_w_in(w_in)
    wb = w_branch.astype(BF16)
    wo = w_out.astype(BF16)
    wup = w_up.astype(BF16)
    wdn = w_down.astype(BF16)
    cos2, sin2 = _rope_tables(tc, tl)
    qw2 = jnp.tile(b_qnorm_w, (1, 2))[:, None]
    kw2 = jnp.tile(b_knorm_w, (1, 2))[:, None]
    a_bcol = a_gate_b.transpose(0, 2, 1)[:, :, :, None]
    c_par = jnp.concatenate([c_a_log, c_dt_bias], axis=1).transpose(0, 2, 1)[:, :, None, :]

    h = jnp.concatenate([ctx, x], axis=1)
    for l in range(depth):
        last = l == depth - 1
        gates, p, bq, bk, qkt, small = _proj_call(h, mod, norm1_w[:, None], w_in_p, c_conv_w, qw2, kw2, cos2, sin2,
                                                  l, tm, nct)
        _, a_grow = _gate_layouts(small[:, :, :16], A_HEADS, 4)
        c_gcol, c_grow = _gate_layouts(small[:, :, 16:32], C_HEADS, 4)
        ya = _mlstm_call(p, qkt, a_grow, a_bcol, a_norm_w[:, None], l, ncc)
        yb = _gqa_call(bq, bk, p, tc, latent_only=last)
        yc = _gdn_call(p, c_gcol, c_grow, c_par, c_norm_w[:, None], l, ncc)
        h = _merge_call(ya, yb, yc, gates, h, mod, wb, wo, l, tm, nct, latent_only=last)
        h = _ffn_call(h, mod, norm2_w[:, None], wup, ffn_conv_w, wdn, l, tm, 0 if last else nct)
    return h
```

```python
import functools

import jax
import jax.numpy as jnp
from jax import lax
from jax.experimental import pallas as pl
from jax.experimental.pallas import tpu as pltpu

F32 = jnp.float32
BF16 = jnp.bfloat16

D_MODEL = 1024
GRID_W = 64
A_HEADS, A_DQK, A_DV = 4, 64, 128
B_HEADS, B_KV_HEADS, B_DH = 8, 2, 64
C_HEADS, C_DK, C_DV = 4, 128, 128
BRANCH_W = 512
D_FF = 2816
CHUNK = 64
ROPE_BASE = 10000.0
EPS = 1e-6
M_INIT = -1e30
LOG2_E = 1.4426950408889634
NEG = -1e30

LANES = 128
W_GATES = 3 * D_MODEL
W_AV = 3072
W_BQ = 4096
W_BK = 4608
W_CQ = 4864
W_CZ = 6400
W_AQK = 6912
W_SMALL = 7424
W_COLS = 7552
P_AV = 0
P_AO = 512
P_BV = 1024
P_CQ = 1152
P_CK = 1664
P_CV = 2176
P_CZ = 2688
P_COLS = 3200
PROJ_TN = 512
FFN_TF = 256
FFN_AHEAD = 3
FFN_DOWN_GROUP = 4
Q_TILE = 256
SCAN_GROUP = 12
MLSTM_GROUP = 18
VMEM_LIMIT = 56 * 1024 * 1024


def _dot(a, b):
    return jnp.dot(a.astype(BF16), b.astype(BF16), preferred_element_type=F32)


def _dot_nt(a, b):
    return lax.dot_general(a.astype(BF16), b.astype(BF16), (((1,), (1,)), ((), ())),
                           preferred_element_type=F32)


def _dot_tn(a, b):
    return lax.dot_general(a.astype(BF16), b.astype(BF16), (((0,), (0,)), ((), ())),
                           preferred_element_type=F32)


def _split(a):
    hi = a.astype(BF16)
    lo = (a - hi.astype(F32)).astype(BF16)
    return hi, lo


def _dot3(a, b):
    ah, al = _split(a)
    bh, bl = _split(b)
    d = functools.partial(jnp.dot, preferred_element_type=F32)
    return d(ah, bh) + (d(al, bh) + d(ah, bl))


def _sigmoid(x):
    return 1.0 / (1.0 + jnp.exp(-x))


def _silu(x):
    return x * _sigmoid(x)


def _softplus(x):
    return jnp.maximum(x, 0.0) + jnp.log1p(jnp.exp(-jnp.abs(x)))


def _logsigmoid(x):
    return -_softplus(-x)


def _modnorm(x, w, shift, scale):
    y = x * lax.rsqrt(jnp.mean(x * x, axis=-1, keepdims=True) + EPS)
    return (y * w) * (1.0 + scale) + shift


def _params(sem, vmem=VMEM_LIMIT):
    return pltpu.CompilerParams(dimension_semantics=sem, vmem_limit_bytes=vmem)


def _resident(shape, index_map):
    return pl.BlockSpec(shape, index_map, pipeline_mode=pl.Buffered(1))


def _ada_kernel(c_ref, w_ref, b_ref, o_ref):
    o_ref[...] = _dot3(_silu(c_ref[...]), w_ref[...]) + b_ref[...]


def _ada_call(cc, ada_w, ada_b):
    depth, d, n = ada_w.shape
    tn = 1536
    return pl.pallas_call(
        _ada_kernel,
        grid=(depth, n // tn),
        in_specs=[pl.BlockSpec(cc.shape, lambda l, j: (0, 0)),
                  pl.BlockSpec((None, d, tn), lambda l, j: (l, 0, j)),
                  pl.BlockSpec((None, 1, tn), lambda l, j: (l, 0, j))],
        out_specs=pl.BlockSpec((None, cc.shape[0], tn), lambda l, j: (l, 0, j)),
        out_shape=jax.ShapeDtypeStruct((depth, cc.shape[0], n), F32),
        compiler_params=_params(("arbitrary", "arbitrary")),
        name="ada",
    )(cc, ada_w, ada_b.reshape(depth, 1, n))


def _proj_kernel(h_ref, hp_ref, hn_ref, mod_ref, nw_ref, w_ref, cw_ref, qw_ref, kw_ref, cos_ref, sin_ref,
                 g_ref, p_ref, bq_ref, bk_ref, qkt_ref, agrow_ref, cgrow_ref, cgcol_ref, *, nct, ntiles):
    tm = h_ref.shape[0]
    j = pl.program_id(1)
    mod = mod_ref[...]
    nw = nw_ref[...]
    xn = _modnorm(h_ref[...], nw, mod[0:1], mod[1:2]).astype(BF16)

    def proj(c0, width):
        return jnp.dot(xn, w_ref[:, c0:c0 + width], preferred_element_type=F32)

    for c0 in range(0, W_GATES, PROJ_TN):
        g_ref[:, c0:c0 + PROJ_TN] = proj(c0, PROJ_TN).astype(BF16)
    for c0 in range(0, P_BV, PROJ_TN):
        p_ref[:, c0:c0 + PROJ_TN] = proj(W_AV + c0, PROJ_TN)
    p_ref[:, P_CZ:P_COLS] = proj(W_CZ, P_COLS - P_CZ)

    r = lax.broadcasted_iota(jnp.int32, (LANES, LANES), 0)
    c = lax.broadcasted_iota(jnp.int32, (LANES, LANES), 1)
    bd = jnp.where(r // B_DH == c // B_DH, 1.0, 0.0).astype(BF16)
    lane_half = (lax.broadcasted_iota(jnp.int32, (tm, LANES), 1) % 32) < 16
    cos, sin = cos_ref[...], sin_ref[...]
    bq = proj(W_BQ, B_HEADS * B_DH)
    for s0 in range(0, B_HEADS * B_DH, LANES):
        y = _headnorm_rope(bq[:, s0:s0 + LANES], bd, qw_ref[...], cos, sin, lane_half)
        bq_ref[:, s0:s0 + LANES] = (y * (B_DH ** -0.5 * LOG2_E)).astype(BF16)
    kv = proj(W_BK, 2 * LANES)
    bk_ref[...] = _headnorm_rope(kv[:, :LANES], bd, kw_ref[...], cos, sin, lane_half).astype(BF16)
    p_ref[:, P_BV:P_BV + LANES] = kv[:, LANES:]

    halo = jnp.concatenate([hp_ref[...], hn_ref[...]], axis=0)
    x_all = jnp.concatenate([xn, _modnorm(halo, nw, mod[0:1], mod[1:2]).astype(BF16)], axis=0)
    has_prev = jnp.logical_and(j != 0, j != nct).astype(F32)
    has_next = jnp.logical_and(j != nct - 1, j != ntiles - 1).astype(F32)
    rows = lax.broadcasted_iota(jnp.int32, (tm, 1), 0)
    width = C_HEADS * C_DK
    for part in range(3):
        c0 = part * width
        u_all = jnp.dot(x_all, w_ref[:, W_CQ + c0:W_CQ + c0 + width], preferred_element_type=F32)
        u = u_all[:tm]
        cw = cw_ref[:, c0:c0 + width]
        prev = jnp.where(rows == 0, u_all[tm + 7:tm + 8] * has_prev, pltpu.roll(u, 1, axis=0))
        nxt = jnp.where(rows == tm - 1, u_all[tm + 8:tm + 9] * has_next, pltpu.roll(u, tm - 1, axis=0))
        y = _silu(prev * cw[0:1] + u * cw[1:2] + nxt * cw[2:3])
        for hd in range(C_HEADS):
            yh = y[:, hd * C_DK:(hd + 1) * C_DK]
            if part < 2:
                yh = yh * lax.rsqrt(jnp.sum(yh * yh, axis=-1, keepdims=True) + EPS)
            if part == 0:
                yh = yh * (C_DK ** -0.5)
            p_ref[:, P_CQ + c0 + hd * C_DK:P_CQ + c0 + (hd + 1) * C_DK] = yh
    qk = jnp.dot(xn, w_ref[:, W_AQK:W_SMALL], preferred_element_type=F32)
    for hd in range(A_HEADS):
        for c in range(tm // CHUNK):
            qkt_ref[hd, c] = qk[c * CHUNK:(c + 1) * CHUNK, hd * LANES:(hd + 1) * LANES].T
    small = jnp.dot(xn, w_ref[:, W_SMALL:W_COLS], preferred_element_type=F32)
    for hd in range(C_HEADS):
        cgcol_ref[hd] = small[:, 4 * A_HEADS + 4 * hd:4 * A_HEADS + 4 * hd + 4]
    for c in range(tm // CHUNK):
        st = small[c * CHUNK:(c + 1) * CHUNK, :].T
        for hd in range(A_HEADS):
            agrow_ref[hd, c] = st[4 * hd:4 * hd + 4]
        for hd in range(C_HEADS):
            cgrow_ref[hd, c] = st[4 * A_HEADS + 4 * hd:4 * A_HEADS + 4 * hd + 4]


def _proj_call(h, mod, nw, w, cw, qw, kw, cos2, sin2, l, tm, nct):
    b, t, d = h.shape
    cpt = tm // CHUNK
    ntiles = t // tm
    hb = tm // 8
    vec = lambda: pl.BlockSpec((None, 1, LANES), lambda i, j: (l, 0, 0))
    table = lambda: pl.BlockSpec((tm, LANES), lambda i, j: (j, 0))
    return pl.pallas_call(
        functools.partial(_proj_kernel, nct=nct, ntiles=ntiles),
        grid=(b, ntiles),
        in_specs=[pl.BlockSpec((None, tm, d), lambda i, j: (i, j, 0)),
                  pl.BlockSpec((None, 8, d), lambda i, j: (i, jnp.maximum(j * hb - 1, 0), 0)),
                  pl.BlockSpec((None, 8, d), lambda i, j: (i, jnp.minimum((j + 1) * hb, t // 8 - 1), 0)),
                  pl.BlockSpec((None, None, None, 6, d), lambda i, j: (l, i, _mod_row(j, nct), 0, 0)),
                  pl.BlockSpec((None, 1, d), lambda i, j: (l, 0, 0)),
                  _resident((None, d, W_COLS), lambda i, j: (l, 0, 0)),
                  _resident((None,) + cw.shape[1:], lambda i, j: (l, 0, 0)),
                  vec(), vec(), table(), table()],
        out_specs=[pl.BlockSpec((None, tm, W_GATES), lambda i, j: (i, j, 0)),
                   pl.BlockSpec((None, tm, P_COLS), lambda i, j: (i, j, 0)),
                   pl.BlockSpec((None, tm, B_HEADS * B_DH), lambda i, j: (i, j, 0)),
                   pl.BlockSpec((None, tm, LANES), lambda i, j: (i, j, 0)),
                   pl.BlockSpec((None, A_HEADS, cpt, 2 * A_DQK, CHUNK), lambda i, j: (i, 0, j, 0, 0)),
                   pl.BlockSpec((None, A_HEADS, cpt, 4, CHUNK), lambda i, j: (i, 0, j, 0, 0)),
                   pl.BlockSpec((None, C_HEADS, cpt, 4, CHUNK), lambda i, j: (i, 0, j, 0, 0)),
                   pl.BlockSpec((None, C_HEADS, tm, 4), lambda i, j: (i, 0, j, 0))],
        out_shape=[jax.ShapeDtypeStruct((b, t, W_GATES), BF16),
                   jax.ShapeDtypeStruct((b, t, P_COLS), F32),
                   jax.ShapeDtypeStruct((b, t, B_HEADS * B_DH), BF16),
                   jax.ShapeDtypeStruct((b, t, LANES), BF16),
                   jax.ShapeDtypeStruct((b, A_HEADS, t // CHUNK, 2 * A_DQK, CHUNK), F32),
                   jax.ShapeDtypeStruct((b, A_HEADS, t // CHUNK, 4, CHUNK), F32),
                   jax.ShapeDtypeStruct((b, C_HEADS, t // CHUNK, 4, CHUNK), F32),
                   jax.ShapeDtypeStruct((b, C_HEADS, t, 4), F32)],
        compiler_params=_params(("arbitrary", "arbitrary")),
        name="proj_in",
    )(h, h, h, mod, nw, w, cw, qw, kw, cos2, sin2)


def _chunk_masks():
    row = lax.broadcasted_iota(jnp.int32, (CHUNK, CHUNK), 0)
    col = lax.broadcasted_iota(jnp.int32, (CHUNK, CHUNK), 1)
    return col <= row, col >= row, col < row, col > row


def _bwd_chunk(it, nc, ncc):
    return jnp.where(it < ncc, ncc - 1 - it, nc - 1 - (it - ncc))


def _rows3(x, n_rows=16):
    hi = x.astype(BF16).astype(F32)
    mid = (x - hi).astype(BF16).astype(F32)
    lo = ((x - hi) - mid).astype(BF16).astype(F32)
    r = lax.broadcasted_iota(jnp.int32, (n_rows, x.shape[1]), 0)
    return jnp.where(r == 0, hi, jnp.where(r == 1, mid, jnp.where(r == 2, lo, 0.0))).astype(BF16)


def _mlstm_kernel(qkt_ref, v_ref, o_ref, grow_ref, bcol_ref, nw_ref, y_ref, hf_ref, hb_ref, *, nc, ncc):
    L = CHUNK
    le, ge, _, _ = _chunk_masks()
    vis = (ge, le)
    row_l = lax.broadcasted_iota(jnp.int32, (L, LANES), 0)
    col_l = lax.broadcasted_iota(jnp.int32, (L, LANES), 1)
    ones_l = jnp.ones((L, LANES), BF16)
    zeros_l = jnp.zeros((L, LANES), BF16)
    stat_rhs = [jnp.concatenate(
        [ones_l, jnp.where((col_l < L) & ((col_l >= row_l) if d == 0 else (col_l <= row_l)), 1.0, 0.0).astype(BF16)],
        axis=1) for d in range(2)]
    ones3 = jnp.where(lax.broadcasted_iota(jnp.int32, (16, LANES), 0) < 3, 1.0, 0.0).astype(BF16)
    tail_rhs = jnp.concatenate([jnp.zeros((16, 2 * LANES), BF16), ones3], axis=1)
    bias_col = bcol_ref[...]
    grp = _group_size(nc, MLSTM_GROUP)
    h_refs = (hf_ref, hb_ref)

    def sum3(x):
        return x[0:1] + x[1:2] + x[2:3]

    def body(gi, carry):
        C = [carry[0], carry[2]]
        m = [carry[1], carry[3]]
        chains = []
        for j in range(grp):
            it = gi * grp + j
            for d, c in ((0, it), (1, _bwd_chunk(it, nc, ncc))):
                gr = grow_ref[c] + bias_col
                x = qkt_ref[c]
                chains.append(dict(d=d, r0=pl.multiple_of(c * L, L), i_row=gr[2 * d:2 * d + 1, :],
                                   f3=_rows3(_logsigmoid(gr[2 * d + 1:2 * d + 2, :])),
                                   qt=x[:A_DQK] * (A_DQK ** -0.5), kt=x[A_DQK:]))
        for ch in chains:
            ch["v"] = v_ref[pl.ds(ch["r0"], L), :].astype(BF16)
            ch["sqt"] = _dot_tn(ch["kt"], ch["qt"])
        for ch in chains:
            st = jnp.dot(ch["f3"], stat_rhs[ch["d"]], preferred_element_type=F32)
            ch["btot"] = sum3(st[:, :LANES])
            ch["bcum"] = sum3(st[:, LANES:LANES + L])
            ch["a_row"] = ch["i_row"] - ch["bcum"]
        for ch in chains:
            ch["a_col"] = lax.dot_general(_rows3(ch["a_row"]), ones3, (((0,), (0,)), ((), ())),
                                          preferred_element_type=F32)
        for ch in chains:
            d = ch["d"]
            m_new = jnp.maximum(ch["btot"] + m[d], ch["btot"] + jnp.max(ch["a_col"], axis=0, keepdims=True))
            dlog = jnp.where(vis[d], ch["bcum"] + ch["a_col"][:, :L], NEG)
            inter = ch["bcum"] + m[d][:, :L]
            mt = jnp.maximum(inter, jnp.max(dlog, axis=0, keepdims=True))
            ch["st"] = (ch["sqt"] * jnp.exp(dlog - mt)).astype(BF16)
            ch["qe"] = (ch["qt"] * jnp.exp(inter - mt)).astype(BF16)
            ch["e3"] = _rows3(jnp.exp(-mt))
            ch["kw"] = (ch["kt"] * jnp.exp(ch["btot"][:, :L] + ch["a_row"] - m_new[:, :L])).astype(BF16)
            decay = jnp.exp(ch["btot"] + m[d] - m_new)
            ch["decay"] = jnp.concatenate([decay, decay], axis=1)
            m[d] = m_new
        for ch in chains:
            ch["kv"] = jnp.dot(ch["kw"], jnp.concatenate([ch["v"], ones_l], axis=1), preferred_element_type=F32)
            lhs = jnp.concatenate([ch["st"], ch["e3"]], axis=0)
            rhs = jnp.concatenate([jnp.concatenate([ch["v"], ones_l, zeros_l], axis=1), tail_rhs], axis=0)
            ch["intra"] = lax.dot_general(lhs, rhs, (((0,), (0,)), ((), ())), preferred_element_type=F32)
        for ch in chains:
            d = ch["d"]
            ch["inter"] = lax.dot_general(ch["qe"], C[d].astype(BF16), (((0,), (0,)), ((), ())),
                                          preferred_element_type=F32)
            C[d] = ch["decay"] * C[d] + ch["kv"]
        for ch in chains:
            num = ch["intra"][:, :A_DV] + ch["inter"][:, :A_DV]
            den = ch["intra"][:, A_DV:2 * A_DV] + ch["inter"][:, A_DV:]
            h_refs[ch["d"]][pl.ds(ch["r0"], L), :] = num / jnp.maximum(jnp.abs(den), ch["intra"][:, 2 * A_DV:])
        return C[0], m[0], C[1], m[1]

    c0 = jnp.zeros((A_DQK, 2 * LANES), F32)
    m0 = jnp.full((1, LANES), M_INIT, F32)
    lax.fori_loop(0, nc // grp, body, (c0, m0, c0, m0))

    hsum = hf_ref[...] + hb_ref[...]
    hn = hsum * lax.rsqrt(jnp.mean(hsum * hsum, axis=-1, keepdims=True) + EPS)
    y_ref[...] = (hn * nw_ref[...] * _sigmoid(o_ref[...])).astype(y_ref.dtype)


def _mlstm_call(p, qkt, grow, bcol, nw, l, ncc):
    b, t, _ = p.shape
    nc = t // CHUNK
    blk = lambda base: pl.BlockSpec((None, t, LANES), lambda i, h: (i, 0, base // LANES + h))
    return pl.pallas_call(
        functools.partial(_mlstm_kernel, nc=nc, ncc=ncc),
        grid=(b, A_HEADS),
        in_specs=[pl.BlockSpec((None, None, nc, 2 * A_DQK, CHUNK), lambda i, h: (i, h, 0, 0, 0)),
                  blk(P_AV), blk(P_AO),
                  pl.BlockSpec((None, None, nc, 4, CHUNK), lambda i, h: (i, h, 0, 0, 0)),
                  pl.BlockSpec((None, None, 4, 1), lambda i, h: (l, h, 0, 0)),
                  pl.BlockSpec((None, 1, LANES), lambda i, h: (l, 0, h))],
        out_specs=pl.BlockSpec((None, t, LANES), lambda i, h: (i, 0, h)),
        out_shape=jax.ShapeDtypeStruct((b, t, BRANCH_W), BF16),
        scratch_shapes=[pltpu.VMEM((t, LANES), F32), pltpu.VMEM((t, LANES), F32)],
        compiler_params=_params(("arbitrary", "arbitrary")),
        name="mlstm",
    )(qkt, p, p, grow, bcol, nw)


def _headnorm_rope(x, bd, w, cos, sin_signed, lane_half):
    hi, lo = _split(x * x)
    ssum = jnp.dot(hi, bd, preferred_element_type=F32) + jnp.dot(lo, bd, preferred_element_type=F32)
    y = x * lax.rsqrt(ssum * (1.0 / B_DH) + EPS) * w
    swapped = jnp.where(lane_half, pltpu.roll(y, LANES - 16, axis=1), pltpu.roll(y, 16, axis=1))
    return y * cos + swapped * sin_signed


def _gqa_kernel(q_ref, k_ref, v_ref, y_ref, vb_ref, *, tc, first_q):
    tq = q_ref.shape[0]
    t = k_ref.shape[0]
    qb = pl.program_id(1) + first_q

    @pl.when(pl.program_id(1) == 0)
    def _():
        v = v_ref[...]
        first = lax.broadcasted_iota(jnp.int32, (t, LANES), 1) < B_DH
        vb_ref[0] = jnp.where(first, v, 1.0).astype(BF16)
        vb_ref[1] = jnp.where(first, pltpu.roll(v, B_DH, axis=1), 1.0).astype(BF16)

    group = B_HEADS // B_KV_HEADS

    def attend(nk):
        scores = []
        for kvh in range(B_KV_HEADS):
            kh = k_ref[0:nk, kvh * B_DH:(kvh + 1) * B_DH]
            qs = jnp.concatenate([q_ref[:, h * B_DH:(h + 1) * B_DH]
                                  for h in range(kvh * group, (kvh + 1) * group)], axis=0)
            scores.append(lax.dot_general(qs, kh, (((1,), (1,)), ((), ())), preferred_element_type=F32))
        for kvh, s in enumerate(scores):
            p = jnp.exp2(s - jnp.max(s, axis=-1, keepdims=True))
            oe = jnp.dot(p.astype(BF16), vb_ref[kvh, 0:nk, :], preferred_element_type=F32)
            o = oe * pltpu.roll(1.0 / oe, B_DH, axis=1)
            for g in range(group):
                h = kvh * group + g
                y_ref[:, h * B_DH:(h + 1) * B_DH] = o[g * tq:(g + 1) * tq, :B_DH].astype(y_ref.dtype)

    nqc = tc // tq
    if first_q < nqc:
        @pl.when(qb < nqc)
        def _():
            attend(tc)

    @pl.when(qb >= nqc)
    def _():
        attend(t)


def _gqa_call(bq, bk, p, tc, latent_only):
    b, t, _ = p.shape
    tq = Q_TILE if (tc % Q_TILE == 0 and t % Q_TILE == 0) else Q_TILE // 2
    first_q = tc // tq if latent_only else 0
    return pl.pallas_call(
        functools.partial(_gqa_kernel, tc=tc, first_q=first_q),
        grid=(b, t // tq - first_q),
        in_specs=[pl.BlockSpec((None, tq, B_HEADS * B_DH), lambda i, j: (i, j + first_q, 0)),
                  pl.BlockSpec((None, t, LANES), lambda i, j: (i, 0, 0)),
                  pl.BlockSpec((None, t, LANES), lambda i, j: (i, 0, P_BV // LANES))],
        out_specs=pl.BlockSpec((None, tq, BRANCH_W), lambda i, j: (i, j, 0)),
        out_shape=jax.ShapeDtypeStruct((b, t - first_q * tq, BRANCH_W), BF16),
        scratch_shapes=[pltpu.VMEM((B_KV_HEADS, t, LANES), BF16)],
        compiler_params=_params(("arbitrary", "arbitrary")),
        name="gqa",
    )(bq, bk, p)


def _group_size(nc, limit=None):
    return max(g for g in range(1, (limit or SCAN_GROUP) + 1) if nc % g == 0)


def _gdn_kernel(qn_ref, kn_ref, vn_ref, z_ref, gcol_ref, grow_ref, par_ref, nw_ref, y_ref,
                n_ref, kq_ref, egl_ref, o_ref, *, nc, ncc):
    L = CHUNK
    le, ge, lt, gt = _chunk_masks()
    par = par_ref[...]

    grp = _group_size(nc)
    ng = nc // grp
    eye = jnp.where(le & ge, 1.0, 0.0).astype(F32)
    rowi = lax.broadcasted_iota(jnp.int32, (L, L), 0)
    coli = lax.broadcasted_iota(jnp.int32, (L, L), 1)
    blk = {b_: (rowi // (2 * b_) == coli // (2 * b_)) & (rowi // b_ != coli // b_) for b_ in (1, 2, 4, 8, 16, 32)}

    def prep_stages(gi):
        chains = []
        for j in range(grp):
            it = gi * grp + j
            for d, c in ((0, it), (1, _bwd_chunk(it, nc, ncc))):
                r0 = pl.multiple_of(c * L, L)
                chains.append(dict(it=it, d=d, c=c, r0=r0, q=qn_ref[pl.ds(r0, L), :],
                                   k=kn_ref[pl.ds(r0, L), :], v=vn_ref[pl.ds(r0, L), :]))
        for ch in chains:
            ch["kkqk"] = _dot_nt(jnp.concatenate([ch["k"], ch["q"]], axis=0), ch["k"])
        yield
        for ch in chains:
            d = ch["d"]
            gc = gcol_ref[pl.ds(ch["r0"], L), :]
            gr = grow_ref[ch["c"]]
            mask_in, mask_t, strict = (le, ge, lt) if d == 0 else (ge, le, gt)
            neg_rate = -jnp.exp(par[:, d:d + 1])
            g_col = neg_rate * _softplus(gc[:, d:d + 1] + par[:, 2 + d:3 + d])
            g_row = neg_rate * _softplus(gr[d:d + 1, :] + par[:, 2 + d:3 + d])
            beta = _sigmoid(gc[:, 2 + d:3 + d])
            G_col = jnp.sum(jnp.where(mask_in, g_row, 0.0), axis=1, keepdims=True)
            G_row = jnp.sum(jnp.where(mask_t, g_col, 0.0), axis=0, keepdims=True)
            dec = jnp.exp(jnp.where(mask_in, G_col - G_row, NEG))
            g_last = jnp.sum(g_row, axis=1, keepdims=True)
            ch.update(beta=beta, dec=dec, eG=jnp.exp(G_col), kscale=jnp.exp(g_last - G_col),
                      egl=jnp.exp(g_last), x=-jnp.where(strict, beta * ch["kkqk"][:L] * dec, 0.0))
        invs = [eye + jnp.where(blk[1], ch["x"], 0.0) for ch in chains]
        bsz = 2
        while bsz < L:
            offs = [jnp.where(blk[bsz], ch["x"], 0.0) for ch in chains]
            tmps = [_dot(off, inv) for off, inv in zip(offs, invs)]
            yield
            invs = [inv + _dot(inv, tmp) for inv, tmp in zip(invs, tmps)]
            yield
            bsz *= 2
        uws = []
        for ch, inv in zip(chains, invs):
            rhs = jnp.concatenate([ch["v"] * ch["beta"], ch["k"] * (ch["beta"] * ch["eG"])], axis=1)
            uws.append(_dot(inv, rhs))
        yield
        kns = [_dot_tn(ch["k"] * ch["kscale"], uw) for ch, uw in zip(chains, uws)]
        yield
        qos = [_dot(ch["kkqk"][L:] * ch["dec"], uw) for ch, uw in zip(chains, uws)]
        yield
        for ch, kn, qo in zip(chains, kns, qos):
            d, it = ch["d"], ch["it"]
            n_ref[d, it] = kn[:, :C_DV]
            kq_ref[d, it, 0:C_DK, :] = kn[:, C_DV:].astype(BF16)
            kq_ref[d, it, C_DK:C_DK + L, :] = (ch["q"] * ch["eG"] - qo[:, C_DV:]).astype(BF16)
            o_ref[d, pl.ds(ch["r0"], L), :] = qo[:, :C_DV]
            egl_ref[d, it] = jnp.broadcast_to(ch["egl"], (8, LANES))

    def recur(it, S):
        out = []
        for d, c in ((0, it), (1, _bwd_chunk(it, nc, ncc))):
            r0 = pl.multiple_of(c * L, L)
            ks_qs = jnp.dot(kq_ref[d, it], S[d].astype(BF16), preferred_element_type=F32)
            o_ref[d, pl.ds(r0, L), :] += ks_qs[C_DK:]
            out.append(S[d] * egl_ref[d, it][0:1, 0:1] + (n_ref[d, it] - ks_qs[:C_DK]))
        return out

    for _ in prep_stages(0):
        pass

    def body(gi, carry):
        S = list(carry)
        stages = prep_stages(gi)
        for j in range(grp):
            next(stages, None)
            S = recur((gi - 1) * grp + j, S)
        for _ in stages:
            pass
        return tuple(S)

    s0 = jnp.zeros((C_DK, C_DV), F32)
    S = list(lax.fori_loop(1, ng, body, (s0, s0)))
    for j in range(grp):
        S = recur((ng - 1) * grp + j, S)

    osum = o_ref[0] + o_ref[1]
    on = osum * lax.rsqrt(jnp.mean(osum * osum, axis=-1, keepdims=True) + EPS) * nw_ref[...]
    y_ref[...] = (on * _silu(z_ref[...])).astype(y_ref.dtype)


def _gdn_call(p, gcol, grow, par, nw, l, ncc):
    b, t, _ = p.shape
    nc = t // CHUNK
    blk = lambda base: pl.BlockSpec((None, t, LANES), lambda i, h: (i, 0, base // LANES + h))
    big = lambda dt: pltpu.VMEM((2, t, LANES), dt)
    return pl.pallas_call(
        functools.partial(_gdn_kernel, nc=nc, ncc=ncc),
        grid=(b, C_HEADS),
        in_specs=[blk(P_CQ), blk(P_CK), blk(P_CV), blk(P_CZ),
                  pl.BlockSpec((None, None, t, 4), lambda i, h: (i, h, 0, 0)),
                  pl.BlockSpec((None, None, nc, 4, CHUNK), lambda i, h: (i, h, 0, 0, 0)),
                  pl.BlockSpec((None, None, 1, 4), lambda i, h: (l, h, 0, 0)),
                  pl.BlockSpec((None, 1, LANES), lambda i, h: (l, 0, 0))],
        out_specs=pl.BlockSpec((None, t, LANES), lambda i, h: (i, 0, h)),
        out_shape=jax.ShapeDtypeStruct((b, t, BRANCH_W), BF16),
        scratch_shapes=[pltpu.VMEM((2, nc, C_DK, C_DV), F32),
                        pltpu.VMEM((2, nc, C_DK + CHUNK, C_DV), BF16),
                        pltpu.VMEM((2, nc, 8, LANES), F32),
                        big(F32)],
        compiler_params=_params(("arbitrary", "arbitrary")),
        name="gdn",
    )(p, p, p, p, gcol, grow, par, nw)


def _merge_kernel(ya_ref, yb_ref, yc_ref, ga_ref, gb_ref, gc_ref, h_ref, mod_ref, wb_ref, wo_ref, o_ref):
    y = (_sigmoid(ga_ref[...].astype(F32)) * _dot(ya_ref[...], wb_ref[0])
         + _sigmoid(gb_ref[...].astype(F32)) * _dot(yb_ref[...], wb_ref[1])
         + _sigmoid(gc_ref[...].astype(F32)) * _dot(yc_ref[...], wb_ref[2]))
    o_ref[...] = h_ref[...] + mod_ref[2:3, :] * _dot(y, wo_ref[...])


def _mod_row(j, nct):
    return 1 if nct == 0 else jnp.minimum(j // nct, 1)


def _merge_call(ya, yb, yc, gates, h, mod, wb, wo, l, tm, nct, latent_only):
    b, t, d = h.shape
    skip = nct if latent_only else 0
    yblk = lambda: pl.BlockSpec((None, tm, BRANCH_W), lambda i, j: (i, j + skip, 0))
    yb_skip = skip - (t - yb.shape[1]) // tm
    gblk = lambda g: pl.BlockSpec((None, tm, d), lambda i, j: (i, j + skip, g))
    return pl.pallas_call(
        _merge_kernel,
        grid=(b, t // tm - skip),
        in_specs=[yblk(), pl.BlockSpec((None, tm, BRANCH_W), lambda i, j: (i, j + yb_skip, 0)), yblk(),
                  gblk(0), gblk(1), gblk(2),
                  pl.BlockSpec((None, tm, d), lambda i, j: (i, j + skip, 0)),
                  pl.BlockSpec((None, None, None, 6, d), lambda i, j: (l, i, _mod_row(j + skip, nct), 0, 0)),
                  _resident((None, 3, BRANCH_W, d), lambda i, j: (l, 0, 0, 0)),
                  _resident((None, d, d), lambda i, j: (l, 0, 0))],
        out_specs=pl.BlockSpec((None, tm, d), lambda i, j: (i, j, 0)),
        out_shape=jax.ShapeDtypeStruct((b, t - skip * tm, d), F32),
        compiler_params=_params(("arbitrary", "arbitrary")),
        name="merge",
    )(ya, yb, yc, gates, gates, gates, h, mod, wb, wo)


def _ffn_kernel(h_ref, hp_ref, hn_ref, mod_ref, nw_ref, wup_ref, cw_ref, wdn_ref, o_ref, acc_ref,
                *, nct, ntiles):
    tm = h_ref.shape[0]
    j = pl.program_id(1)
    mod = mod_ref[...]
    nw = nw_ref[...]
    h = h_ref[...]
    xn = _modnorm(h, nw, mod[3:4], mod[4:5]).astype(BF16)
    halo = jnp.concatenate([hp_ref[...], hn_ref[...]], axis=0)
    xh = _modnorm(halo, nw, mod[3:4], mod[4:5]).astype(BF16)
    has_prev = jnp.logical_and(j != 0, j != nct).astype(F32)
    has_next = jnp.logical_and(j != nct - 1, j != ntiles - 1).astype(F32)
    rows = lax.broadcasted_iota(jnp.int32, (tm, 1), 0)
    first = rows == 0
    last = rows == tm - 1

    def cols(kind, jf):
        return slice(kind * D_FF + jf * FFN_TF, kind * D_FF + (jf + 1) * FFN_TF)

    x_all = jnp.concatenate([xn, xh], axis=0)

    def up(jf):
        us = [jnp.dot(x_all, wup_ref[:, cols(kind, jf)], preferred_element_type=F32) for kind in range(2)]
        return [(u[:tm], u[tm:]) for u in us]

    def conv(kind, jf, u, uh):
        cw = cw_ref[:, cols(kind, jf)]
        prev = jnp.where(first, uh[7:8] * has_prev, pltpu.roll(u, 1, axis=0))
        nxt = jnp.where(last, uh[8:9] * has_next, pltpu.roll(u, tm - 1, axis=0))
        return prev * cw[0:1] + u * cw[1:2] + nxt * cw[2:3]

    nj = D_FF // FFN_TF
    ahead = [up(jf) for jf in range(min(FFN_AHEAD, nj))]
    acts = []
    for jf in range(nj):
        if jf + FFN_AHEAD < nj:
            ahead.append(up(jf + FFN_AHEAD))
        cur = ahead.pop(0)
        acts.append((conv(0, jf, *cur[0]) * _silu(conv(1, jf, *cur[1]))).astype(BF16))
        if len(acts) == FFN_DOWN_GROUP or jf == nj - 1:
            r1 = (jf + 1) * FFN_TF
            part = jnp.dot(jnp.concatenate(acts, axis=1), wdn_ref[r1 - len(acts) * FFN_TF:r1, :],
                           preferred_element_type=F32)
            if r1 == len(acts) * FFN_TF:
                acc_ref[...] = part
            else:
                acc_ref[...] += part
            acts = []
    o_ref[...] = h + mod[5:6] * acc_ref[...]


def _ffn_call(h, mod, nw, wup, cw, wdn, l, tm, nct):
    b, t, d = h.shape
    ntiles = t // tm
    hb = tm // 8
    return pl.pallas_call(
        functools.partial(_ffn_kernel, nct=nct, ntiles=ntiles),
        grid=(b, ntiles),
        in_specs=[pl.BlockSpec((None, tm, d), lambda i, j: (i, j, 0)),
                  pl.BlockSpec((None, 8, d), lambda i, j: (i, jnp.maximum(j * hb - 1, 0), 0)),
                  pl.BlockSpec((None, 8, d), lambda i, j: (i, jnp.minimum((j + 1) * hb, t // 8 - 1), 0)),
                  pl.BlockSpec((None, None, None, 6, d), lambda i, j: (l, i, _mod_row(j, nct), 0, 0)),
                  pl.BlockSpec((None, 1, d), lambda i, j: (l, 0, 0)),
                  _resident((None,) + wup.shape[1:], lambda i, j: (l, 0, 0)),
                  _resident((None,) + cw.shape[1:], lambda i, j: (l, 0, 0)),
                  _resident((None,) + wdn.shape[1:], lambda i, j: (l, 0, 0))],
        out_specs=pl.BlockSpec((None, tm, d), lambda i, j: (i, j, 0)),
        out_shape=jax.ShapeDtypeStruct((b, t, d), F32),
        scratch_shapes=[pltpu.VMEM((tm, d), F32)],
        compiler_params=_params(("arbitrary", "arbitrary")),
        name="ffn",
    )(h, h, h, mod, nw, wup, cw, wdn)


def _rope_tables(tc, tl):
    rows = tl // GRID_W
    row = jnp.repeat(jnp.arange(rows, dtype=F32), GRID_W)
    col = jnp.tile(jnp.arange(GRID_W, dtype=F32), rows)
    n_freq = B_DH // 4
    inv = ROPE_BASE ** (-jnp.arange(n_freq, dtype=F32) / n_freq)
    ang_r = row[:, None] * inv
    ang_c = col[:, None] * inv
    cos = jnp.concatenate([jnp.cos(ang_r)] * 2 + [jnp.cos(ang_c)] * 2, axis=1)
    sin = jnp.concatenate([-jnp.sin(ang_r), jnp.sin(ang_r), -jnp.sin(ang_c), jnp.sin(ang_c)], axis=1)
    cos = jnp.concatenate([jnp.ones((tc, B_DH), F32), cos], axis=0)
    sin = jnp.concatenate([jnp.zeros((tc, B_DH), F32), sin], axis=0)
    return jnp.tile(cos, (1, 2)), jnp.tile(sin, (1, 2))


def _pack_w_in(w_in):
    depth, d, _ = w_in.shape
    w = w_in.astype(BF16)
    o = 0
    parts = {}
    for name, width in (("aq", 256), ("ak", 256), ("av", 512), ("ao", 512), ("ag", 16),
                        ("bq", 512), ("bk", 128), ("bv", 128),
                        ("cq", 512), ("ck", 512), ("cv", 512), ("cz", 512), ("ca", 8), ("cb", 8),
                        ("gate", 3 * D_MODEL)):
        parts[name] = w[:, :, o:o + width]
        o += width
    aqk = jnp.concatenate([parts["aq"].reshape(depth, d, A_HEADS, A_DQK),
                           parts["ak"].reshape(depth, d, A_HEADS, A_DQK)], axis=3).reshape(depth, d, 512)
    ag = parts["ag"].reshape(depth, d, 4, A_HEADS).transpose(0, 1, 3, 2).reshape(depth, d, 4 * A_HEADS)
    cg = (jnp.concatenate([parts["ca"], parts["cb"]], axis=2).reshape(depth, d, 4, C_HEADS)
          .transpose(0, 1, 3, 2).reshape(depth, d, 4 * C_HEADS))
    small = jnp.concatenate([ag, cg], axis=2)
    pad = jnp.zeros((depth, d, W_COLS - W_SMALL - small.shape[2]), BF16)
    cols = [parts["gate"], parts["av"], parts["ao"], parts["bq"], parts["bk"], parts["bv"],
            parts["cq"], parts["ck"], parts["cv"], parts["cz"], aqk, small, pad]
    return jnp.concatenate(cols, axis=2)


def kernel(x, c, ctx, c_ctx, norm1_w, norm2_w, ada_w, ada_b, w_in, a_gate_b, a_norm_w, b_qnorm_w,
           b_knorm_w, c_conv_w, c_a_log, c_dt_bias, c_norm_w, w_branch, w_out, w_up, ffn_conv_w, w_down):
    b, tl, d = x.shape
    tc = ctx.shape[1]
    depth = w_in.shape[0]
    t = tc + tl
    tm = 256 if (tc % 256 == 0 and tl % 256 == 0) else 128
    nct = tc // tm
    ncc = tc // CHUNK
    nj = D_FF // FFN_TF

    cc = jnp.zeros((16, d), F32).at[:b].set(c).at[b].set(c_ctx)
    mods = _ada_call(cc, ada_w, ada_b).reshape(depth, 16, 6, d)
    mod = jnp.stack([jnp.broadcast_to(mods[:, b][:, None], (depth, b, 6, d)), mods[:, :b]], axis=2)

    w_in_p = _pack_w_in(w_in)
    wb = w_branch.astype(BF16)
    wo = w_out.astype(BF16)
    wup = w_up.astype(BF16)
    wdn = w_down.astype(BF16)
    cos2, sin2 = _rope_tables(tc, tl)
    qw2 = jnp.tile(b_qnorm_w, (1, 2))[:, None]
    kw2 = jnp.tile(b_knorm_w, (1, 2))[:, None]
    a_bcol = a_gate_b.transpose(0, 2, 1)[:, :, :, None]
    c_par = jnp.concatenate([c_a_log, c_dt_bias], axis=1).transpose(0, 2, 1)[:, :, None, :]

    h = jnp.concatenate([ctx, x], axis=1)
    for l in range(depth):
        last = l == depth - 1
        gates, p, bq, bk, qkt, a_grow, c_grow, c_gcol = _proj_call(h, mod, norm1_w[:, None], w_in_p, c_conv_w, qw2, kw2, cos2, sin2,
                                                  l, tm, nct)
        ya = _mlstm_call(p, qkt, a_grow, a_bcol, a_norm_w[:, None], l, ncc)
        yb = _gqa_call(bq, bk, p, tc, latent_only=last)
        yc = _gdn_call(p, c_gcol, c_grow, c_par, c_norm_w[:, None], l, ncc)
        h = _merge_call(ya, yb, yc, gates, h, mod, wb, wo, l, tm, nct, latent_only=last)
        h = _ffn_call(h, mod, norm2_w[:, None], wup, ffn_conv_w, wdn, l, tm, 0 if last else nct)
    return h
```

```python
import functools

import jax
import jax.numpy as jnp
from jax import lax
from jax.experimental import pallas as pl
from jax.experimental.pallas import tpu as pltpu

F32 = jnp.float32
BF16 = jnp.bfloat16

D_MODEL = 1024
GRID_W = 64
A_HEADS, A_DQK, A_DV = 4, 64, 128
B_HEADS, B_KV_HEADS, B_DH = 8, 2, 64
C_HEADS, C_DK, C_DV = 4, 128, 128
BRANCH_W = 512
D_FF = 2816
CHUNK = 64
ROPE_BASE = 10000.0
EPS = 1e-6
M_INIT = -1e30
LOG2_E = 1.4426950408889634
NEG = -1e30

LANES = 128
W_GATES = 3 * D_MODEL
W_AV = 3072
W_BQ = 4096
W_BK = 4608
W_CQ = 4864
W_CZ = 6400
W_AQK = 6912
W_SMALL = 7424
W_COLS = 7552
P_AV = 0
P_AO = 512
P_BV = 1024
P_CQ = 1152
P_CK = 1664
P_CV = 2176
P_CZ = 2688
P_COLS = 3200
PROJ_TN = 512
FFN_TF = 256
FFN_AHEAD = 3
FFN_DOWN_GROUP = 4
Q_TILE = 256
SCAN_GROUP = 12
MLSTM_GROUP = 18
VMEM_LIMIT = 56 * 1024 * 1024


def _dot(a, b):
    return jnp.dot(a.astype(BF16), b.astype(BF16), preferred_element_type=F32)


def _dot_nt(a, b):
    return lax.dot_general(a.astype(BF16), b.astype(BF16), (((1,), (1,)), ((), ())),
                           preferred_element_type=F32)


def _dot_tn(a, b):
    return lax.dot_general(a.astype(BF16), b.astype(BF16), (((0,), (0,)), ((), ())),
                           preferred_element_type=F32)


def _split(a):
    hi = a.astype(BF16)
    lo = (a - hi.astype(F32)).astype(BF16)
    return hi, lo


def _dot3(a, b):
    ah, al = _split(a)
    bh, bl = _split(b)
    d = functools.partial(jnp.dot, preferred_element_type=F32)
    return d(ah, bh) + (d(al, bh) + d(ah, bl))


def _sigmoid(x):
    return 1.0 / (1.0 + jnp.exp(-x))


def _silu(x):
    return x * _sigmoid(x)


def _softplus(x):
    return jnp.maximum(x, 0.0) + jnp.log1p(jnp.exp(-jnp.abs(x)))


def _logsigmoid(x):
    return -_softplus(-x)


def _modnorm(x, w, shift, scale):
    y = x * lax.rsqrt(jnp.mean(x * x, axis=-1, keepdims=True) + EPS)
    return (y * w) * (1.0 + scale) + shift


def _params(sem, vmem=VMEM_LIMIT):
    return pltpu.CompilerParams(dimension_semantics=sem, vmem_limit_bytes=vmem)


def _resident(shape, index_map):
    return pl.BlockSpec(shape, index_map, pipeline_mode=pl.Buffered(1))


def _ada_kernel(c_ref, w_ref, b_ref, o_ref):
    o_ref[...] = _dot3(_silu(c_ref[...]), w_ref[...]) + b_ref[...]


def _ada_call(cc, ada_w, ada_b):
    depth, d, n = ada_w.shape
    tn = 1536
    return pl.pallas_call(
        _ada_kernel,
        grid=(depth, n // tn),
        in_specs=[pl.BlockSpec(cc.shape, lambda l, j: (0, 0)),
                  pl.BlockSpec((None, d, tn), lambda l, j: (l, 0, j)),
                  pl.BlockSpec((None, 1, tn), lambda l, j: (l, 0, j))],
        out_specs=pl.BlockSpec((None, cc.shape[0], tn), lambda l, j: (l, 0, j)),
        out_shape=jax.ShapeDtypeStruct((depth, cc.shape[0], n), F32),
        compiler_params=_params(("arbitrary", "arbitrary")),
        name="ada",
    )(cc, ada_w, ada_b.reshape(depth, 1, n))


def _proj_kernel(h_ref, hp_ref, hn_ref, mod_ref, nw_ref, w_ref, cw_ref, qw_ref, kw_ref, cos_ref, sin_ref,
                 g_ref, p_ref, bq_ref, bk_ref, qkt_ref, agrow_ref, cgrow_ref, cgcol_ref, *, nct, ntiles):
    tm = h_ref.shape[0]
    j = pl.program_id(1)
    mod = mod_ref[...]
    nw = nw_ref[...]
    xn = _modnorm(h_ref[...], nw, mod[0:1], mod[1:2]).astype(BF16)

    def proj(c0, width):
        return jnp.dot(xn, w_ref[:, c0:c0 + width], preferred_element_type=F32)

    for c0 in range(0, W_GATES, PROJ_TN):
        g_ref[:, c0:c0 + PROJ_TN] = proj(c0, PROJ_TN).astype(BF16)
    for c0 in range(0, P_BV, PROJ_TN):
        p_ref[:, c0:c0 + PROJ_TN] = proj(W_AV + c0, PROJ_TN)
    p_ref[:, P_CZ:P_COLS] = proj(W_CZ, P_COLS - P_CZ)

    r = lax.broadcasted_iota(jnp.int32, (LANES, LANES), 0)
    c = lax.broadcasted_iota(jnp.int32, (LANES, LANES), 1)
    bd = jnp.where(r // B_DH == c // B_DH, 1.0, 0.0).astype(BF16)
    lane_half = (lax.broadcasted_iota(jnp.int32, (tm, LANES), 1) % 32) < 16
    cos, sin = cos_ref[...], sin_ref[...]
    bq = proj(W_BQ, B_HEADS * B_DH)
    for s0 in range(0, B_HEADS * B_DH, LANES):
        y = _headnorm_rope(bq[:, s0:s0 + LANES], bd, qw_ref[...], cos, sin, lane_half)
        bq_ref[:, s0:s0 + LANES] = (y * (B_DH ** -0.5 * LOG2_E)).astype(BF16)
    kv = proj(W_BK, 2 * LANES)
    bk_ref[...] = _headnorm_rope(kv[:, :LANES], bd, kw_ref[...], cos, sin, lane_half).astype(BF16)
    p_ref[:, P_BV:P_BV + LANES] = kv[:, LANES:]

    halo = jnp.concatenate([hp_ref[...], hn_ref[...]], axis=0)
    x_all = jnp.concatenate([xn, _modnorm(halo, nw, mod[0:1], mod[1:2]).astype(BF16)], axis=0)
    has_prev = jnp.logical_and(j != 0, j != nct).astype(F32)
    has_next = jnp.logical_and(j != nct - 1, j != ntiles - 1).astype(F32)
    rows = lax.broadcasted_iota(jnp.int32, (tm, 1), 0)
    width = C_HEADS * C_DK
    for part in range(3):
        c0 = part * width
        u_all = jnp.dot(x_all, w_ref[:, W_CQ + c0:W_CQ + c0 + width], preferred_element_type=F32)
        u = u_all[:tm]
        cw = cw_ref[:, c0:c0 + width]
        prev = jnp.where(rows == 0, u_all[tm + 7:tm + 8] * has_prev, pltpu.roll(u, 1, axis=0))
        nxt = jnp.where(rows == tm - 1, u_all[tm + 8:tm + 9] * has_next, pltpu.roll(u, tm - 1, axis=0))
        y = _silu(prev * cw[0:1] + u * cw[1:2] + nxt * cw[2:3])
        for hd in range(C_HEADS):
            yh = y[:, hd * C_DK:(hd + 1) * C_DK]
            if part < 2:
                yh = yh * lax.rsqrt(jnp.sum(yh * yh, axis=-1, keepdims=True) + EPS)
            if part == 0:
                yh = yh * (C_DK ** -0.5)
            p_ref[:, P_CQ + c0 + hd * C_DK:P_CQ + c0 + (hd + 1) * C_DK] = yh
    qk = jnp.dot(xn, w_ref[:, W_AQK:W_SMALL], preferred_element_type=F32)
    for hd in range(A_HEADS):
        for c in range(tm // CHUNK):
            qkt_ref[hd, c] = qk[c * CHUNK:(c + 1) * CHUNK, hd * LANES:(hd + 1) * LANES].T
    small = jnp.dot(xn, w_ref[:, W_SMALL:W_COLS], preferred_element_type=F32)
    for hd in range(C_HEADS):
        cgcol_ref[hd] = small[:, 4 * A_HEADS + 4 * hd:4 * A_HEADS + 4 * hd + 4]
    for c in range(tm // CHUNK):
        st = small[c * CHUNK:(c + 1) * CHUNK, :].T
        for hd in range(A_HEADS):
            agrow_ref[hd, c] = st[4 * hd:4 * hd + 4]
        for hd in range(C_HEADS):
            cgrow_ref[hd, c] = st[4 * A_HEADS + 4 * hd:4 * A_HEADS + 4 * hd + 4]


def _proj_call(h, mod, nw, w, cw, qw, kw, cos2, sin2, l, tm, nct):
    b, t, d = h.shape
    cpt = tm // CHUNK
    ntiles = t // tm
    hb = tm // 8
    vec = lambda: pl.BlockSpec((None, 1, LANES), lambda i, j: (l, 0, 0))
    table = lambda: pl.BlockSpec((tm, LANES), lambda i, j: (j, 0))
    return pl.pallas_call(
        functools.partial(_proj_kernel, nct=nct, ntiles=ntiles),
        grid=(b, ntiles),
        in_specs=[pl.BlockSpec((None, tm, d), lambda i, j: (i, j, 0)),
                  pl.BlockSpec((None, 8, d), lambda i, j: (i, jnp.maximum(j * hb - 1, 0), 0)),
                  pl.BlockSpec((None, 8, d), lambda i, j: (i, jnp.minimum((j + 1) * hb, t // 8 - 1), 0)),
                  pl.BlockSpec((None, None, None, 6, d), lambda i, j: (l, i, _mod_row(j, nct), 0, 0)),
                  pl.BlockSpec((None, 1, d), lambda i, j: (l, 0, 0)),
                  _resident((None, d, W_COLS), lambda i, j: (l, 0, 0)),
                  _resident((None,) + cw.shape[1:], lambda i, j: (l, 0, 0)),
                  vec(), vec(), table(), table()],
        out_specs=[pl.BlockSpec((None, tm, W_GATES), lambda i, j: (i, j, 0)),
                   pl.BlockSpec((None, tm, P_COLS), lambda i, j: (i, j, 0)),
                   pl.BlockSpec((None, tm, B_HEADS * B_DH), lambda i, j: (i, j, 0)),
                   pl.BlockSpec((None, tm, LANES), lambda i, j: (i, j, 0)),
                   pl.BlockSpec((None, A_HEADS, cpt, 2 * A_DQK, CHUNK), lambda i, j: (i, 0, j, 0, 0)),
                   pl.BlockSpec((None, A_HEADS, cpt, 4, CHUNK), lambda i, j: (i, 0, j, 0, 0)),
                   pl.BlockSpec((None, C_HEADS, cpt, 4, CHUNK), lambda i, j: (i, 0, j, 0, 0)),
                   pl.BlockSpec((None, C_HEADS, tm, 4), lambda i, j: (i, 0, j, 0))],
        out_shape=[jax.ShapeDtypeStruct((b, t, W_GATES), BF16),
                   jax.ShapeDtypeStruct((b, t, P_COLS), F32),
                   jax.ShapeDtypeStruct((b, t, B_HEADS * B_DH), BF16),
                   jax.ShapeDtypeStruct((b, t, LANES), BF16),
                   jax.ShapeDtypeStruct((b, A_HEADS, t // CHUNK, 2 * A_DQK, CHUNK), F32),
                   jax.ShapeDtypeStruct((b, A_HEADS, t // CHUNK, 4, CHUNK), F32),
                   jax.ShapeDtypeStruct((b, C_HEADS, t // CHUNK, 4, CHUNK), F32),
                   jax.ShapeDtypeStruct((b, C_HEADS, t, 4), F32)],
        compiler_params=_params(("arbitrary", "arbitrary")),
        name="proj_in",
    )(h, h, h, mod, nw, w, cw, qw, kw, cos2, sin2)


def _chunk_masks():
    row = lax.broadcasted_iota(jnp.int32, (CHUNK, CHUNK), 0)
    col = lax.broadcasted_iota(jnp.int32, (CHUNK, CHUNK), 1)
    return col <= row, col >= row, col < row, col > row


def _bwd_chunk(it, nc, ncc):
    return jnp.where(it < ncc, ncc - 1 - it, nc - 1 - (it - ncc))


def _rows3(x, n_rows=16):
    hi = x.astype(BF16).astype(F32)
    mid = (x - hi).astype(BF16).astype(F32)
    lo = ((x - hi) - mid).astype(BF16).astype(F32)
    r = lax.broadcasted_iota(jnp.int32, (n_rows, x.shape[1]), 0)
    return jnp.where(r == 0, hi, jnp.where(r == 1, mid, jnp.where(r == 2, lo, 0.0))).astype(BF16)


def _mlstm_kernel(qkt_ref, v_ref, o_ref, grow_ref, bcol_ref, nw_ref, y_ref, hf_ref, hb_ref, *, nc, ncc):
    L = CHUNK
    le, ge, _, _ = _chunk_masks()
    vis = (ge, le)
    row_l = lax.broadcasted_iota(jnp.int32, (L, LANES), 0)
    col_l = lax.broadcasted_iota(jnp.int32, (L, LANES), 1)
    ones_l = jnp.ones((L, LANES), BF16)
    zeros_l = jnp.zeros((L, LANES), BF16)
    stat_rhs = [jnp.concatenate(
        [ones_l, jnp.where((col_l < L) & ((col_l >= row_l) if d == 0 else (col_l <= row_l)), 1.0, 0.0).astype(BF16)],
        axis=1) for d in range(2)]
    ones3 = jnp.where(lax.broadcasted_iota(jnp.int32, (16, LANES), 0) < 3, 1.0, 0.0).astype(BF16)
    tail_rhs = jnp.concatenate([jnp.zeros((16, 2 * LANES), BF16), ones3], axis=1)
    bias_col = bcol_ref[...]
    grp = _group_size(nc, MLSTM_GROUP)
    h_refs = (hf_ref, hb_ref)

    def sum3(x):
        return x[0:1] + x[1:2] + x[2:3]

    def body(gi, carry):
        C = [carry[0], carry[2]]
        m = [carry[1], carry[3]]
        chains = []
        for j in range(grp):
            it = gi * grp + j
            for d, c in ((0, it), (1, _bwd_chunk(it, nc, ncc))):
                gr = grow_ref[c] + bias_col
                x = qkt_ref[c]
                chains.append(dict(d=d, r0=pl.multiple_of(c * L, L), i_row=gr[2 * d:2 * d + 1, :],
                                   f3=_rows3(_logsigmoid(gr[2 * d + 1:2 * d + 2, :])),
                                   qt=x[:A_DQK] * (A_DQK ** -0.5), kt=x[A_DQK:]))
        for ch in chains:
            ch["v"] = v_ref[pl.ds(ch["r0"], L), :].astype(BF16)
            ch["sqt"] = _dot_tn(ch["kt"], ch["qt"])
        for ch in chains:
            st = jnp.dot(ch["f3"], stat_rhs[ch["d"]], preferred_element_type=F32)
            ch["btot"] = sum3(st[:, :LANES])
            ch["bcum"] = sum3(st[:, LANES:LANES + L])
            ch["a_row"] = ch["i_row"] - ch["bcum"]
        for ch in chains:
            ch["a_col"] = lax.dot_general(_rows3(ch["a_row"]), ones3, (((0,), (0,)), ((), ())),
                                          preferred_element_type=F32)
        for ch in chains:
            d = ch["d"]
            m_new = jnp.maximum(ch["btot"] + m[d], ch["btot"] + jnp.max(ch["a_col"], axis=0, keepdims=True))
            dlog = jnp.where(vis[d], ch["bcum"] + ch["a_col"][:, :L], NEG)
            inter = ch["bcum"] + m[d][:, :L]
            mt = jnp.maximum(inter, jnp.max(dlog, axis=0, keepdims=True))
            ch["st"] = (ch["sqt"] * jnp.exp(dlog - mt)).astype(BF16)
            ch["qe"] = (ch["qt"] * jnp.exp(inter - mt)).astype(BF16)
            ch["e3"] = _rows3(jnp.exp(-mt))
            ch["kw"] = (ch["kt"] * jnp.exp(ch["btot"][:, :L] + ch["a_row"] - m_new[:, :L])).astype(BF16)
            decay = jnp.exp(ch["btot"] + m[d] - m_new)
            ch["decay"] = jnp.concatenate([decay, decay], axis=1)
            m[d] = m_new
        for ch in chains:
            ch["kv"] = jnp.dot(ch["kw"], jnp.concatenate([ch["v"], ones_l], axis=1), preferred_element_type=F32)
            lhs = jnp.concatenate([ch["st"], ch["e3"]], axis=0)
            rhs = jnp.concatenate([jnp.concatenate([ch["v"], ones_l, zeros_l], axis=1), tail_rhs], axis=0)
            ch["intra"] = lax.dot_general(lhs, rhs, (((0,), (0,)), ((), ())), preferred_element_type=F32)
        for ch in chains:
            d = ch["d"]
            ch["inter"] = lax.dot_general(ch["qe"], C[d].astype(BF16), (((0,), (0,)), ((), ())),
                                          preferred_element_type=F32)
            C[d] = ch["decay"] * C[d] + ch["kv"]
        for ch in chains:
            num = ch["intra"][:, :A_DV] + ch["inter"][:, :A_DV]
            den = ch["intra"][:, A_DV:2 * A_DV] + ch["inter"][:, A_DV:]
            h_refs[ch["d"]][pl.ds(ch["r0"], L), :] = num / jnp.maximum(jnp.abs(den), ch["intra"][:, 2 * A_DV:])
        return C[0], m[0], C[1], m[1]

    c0 = jnp.zeros((A_DQK, 2 * LANES), F32)
    m0 = jnp.full((1, LANES), M_INIT, F32)
    lax.fori_loop(0, nc // grp, body, (c0, m0, c0, m0))

    hsum = hf_ref[...] + hb_ref[...]
    hn = hsum * lax.rsqrt(jnp.mean(hsum * hsum, axis=-1, keepdims=True) + EPS)
    y_ref[...] = (hn * nw_ref[...] * _sigmoid(o_ref[...])).astype(y_ref.dtype)


def _mlstm_call(p, qkt, grow, bcol, nw, l, ncc):
    b, t, _ = p.shape
    nc = t // CHUNK
    blk = lambda base: pl.BlockSpec((None, t, LANES), lambda i, h: (i, 0, base // LANES + h))
    return pl.pallas_call(
        functools.partial(_mlstm_kernel, nc=nc, ncc=ncc),
        grid=(b, A_HEADS),
        in_specs=[pl.BlockSpec((None, None, nc, 2 * A_DQK, CHUNK), lambda i, h: (i, h, 0, 0, 0)),
                  blk(P_AV), blk(P_AO),
                  pl.BlockSpec((None, None, nc, 4, CHUNK), lambda i, h: (i, h, 0, 0, 0)),
                  pl.BlockSpec((None, None, 4, 1), lambda i, h: (l, h, 0, 0)),
                  pl.BlockSpec((None, 1, LANES), lambda i, h: (l, 0, h))],
        out_specs=pl.BlockSpec((None, t, LANES), lambda i, h: (i, 0, h)),
        out_shape=jax.ShapeDtypeStruct((b, t, BRANCH_W), BF16),
        scratch_shapes=[pltpu.VMEM((t, LANES), F32), pltpu.VMEM((t, LANES), F32)],
        compiler_params=_params(("arbitrary", "arbitrary")),
        name="mlstm",
    )(qkt, p, p, grow, bcol, nw)


def _headnorm_rope(x, bd, w, cos, sin_signed, lane_half):
    hi, lo = _split(x * x)
    ssum = jnp.dot(hi, bd, preferred_element_type=F32) + jnp.dot(lo, bd, preferred_element_type=F32)
    y = x * lax.rsqrt(ssum * (1.0 / B_DH) + EPS) * w
    swapped = jnp.where(lane_half, pltpu.roll(y, LANES - 16, axis=1), pltpu.roll(y, 16, axis=1))
    return y * cos + swapped * sin_signed


def _gqa_kernel(q_ref, k_ref, v_ref, y_ref, vb_ref, *, tc, first_q):
    tq = q_ref.shape[0]
    t = k_ref.shape[0]
    qb = pl.program_id(1) + first_q

    @pl.when(pl.program_id(1) == 0)
    def _():
        v = v_ref[...]
        first = lax.broadcasted_iota(jnp.int32, (t, LANES), 1) < B_DH
        vb_ref[0] = jnp.where(first, v, 1.0).astype(BF16)
        vb_ref[1] = jnp.where(first, pltpu.roll(v, B_DH, axis=1), 1.0).astype(BF16)

    group = B_HEADS // B_KV_HEADS

    def attend(nk):
        scores = []
        for kvh in range(B_KV_HEADS):
            kh = k_ref[0:nk, kvh * B_DH:(kvh + 1) * B_DH]
            qs = jnp.concatenate([q_ref[:, h * B_DH:(h + 1) * B_DH]
                                  for h in range(kvh * group, (kvh + 1) * group)], axis=0)
            scores.append(lax.dot_general(qs, kh, (((1,), (1,)), ((), ())), preferred_element_type=F32))
        for kvh, s in enumerate(scores):
            p = jnp.exp2(s - jnp.max(s, axis=-1, keepdims=True))
            oe = jnp.dot(p.astype(BF16), vb_ref[kvh, 0:nk, :], preferred_element_type=F32)
            o = oe * pltpu.roll(1.0 / oe, B_DH, axis=1)
            for g in range(group):
                h = kvh * group + g
                y_ref[:, h * B_DH:(h + 1) * B_DH] = o[g * tq:(g + 1) * tq, :B_DH].astype(y_ref.dtype)

    nqc = tc // tq
    if first_q < nqc:
        @pl.when(qb < nqc)
        def _():
            attend(tc)

    @pl.when(qb >= nqc)
    def _():
        attend(t)


def _gqa_call(bq, bk, p, tc, latent_only):
    b, t, _ = p.shape
    tq = Q_TILE if (tc % Q_TILE == 0 and t % Q_TILE == 0) else Q_TILE // 2
    first_q = tc // tq if latent_only else 0
    return pl.pallas_call(
        functools.partial(_gqa_kernel, tc=tc, first_q=first_q),
        grid=(b, t // tq - first_q),
        in_specs=[pl.BlockSpec((None, tq, B_HEADS * B_DH), lambda i, j: (i, j + first_q, 0)),
                  pl.BlockSpec((None, t, LANES), lambda i, j: (i, 0, 0)),
                  pl.BlockSpec((None, t, LANES), lambda i, j: (i, 0, P_BV // LANES))],
        out_specs=pl.BlockSpec((None, tq, BRANCH_W), lambda i, j: (i, j, 0)),
        out_shape=jax.ShapeDtypeStruct((b, t - first_q * tq, BRANCH_W), BF16),
        scratch_shapes=[pltpu.VMEM((B_KV_HEADS, t, LANES), BF16)],
        compiler_params=_params(("arbitrary", "arbitrary")),
        name="gqa",
    )(bq, bk, p)


def _group_size(nc, limit=None):
    return max(g for g in range(1, (limit or SCAN_GROUP) + 1) if nc % g == 0)


def _gdn_kernel(qn_ref, kn_ref, vn_ref, z_ref, gcol_ref, grow_ref, par_ref, nw_ref, y_ref,
                n_ref, kq_ref, egl_ref, o_ref, *, nc, ncc):
    L = CHUNK
    le, ge, lt, gt = _chunk_masks()
    par = par_ref[...]

    grp = _group_size(nc)
    ng = nc // grp
    eye = jnp.where(le & ge, 1.0, 0.0).astype(F32)
    rowi = lax.broadcasted_iota(jnp.int32, (L, L), 0)
    coli = lax.broadcasted_iota(jnp.int32, (L, L), 1)
    blk = {b_: (rowi // (2 * b_) == coli // (2 * b_)) & (rowi // b_ != coli // b_) for b_ in (1, 2, 4, 8, 16, 32)}

    def prep_stages(gi):
        chains = []
        for j in range(grp):
            it = gi * grp + j
            for d, c in ((0, it), (1, _bwd_chunk(it, nc, ncc))):
                r0 = pl.multiple_of(c * L, L)
                chains.append(dict(it=it, d=d, c=c, r0=r0, q=qn_ref[pl.ds(r0, L), :],
                                   k=kn_ref[pl.ds(r0, L), :], v=vn_ref[pl.ds(r0, L), :]))
        for ch in chains:
            ch["kkqk"] = _dot_nt(jnp.concatenate([ch["k"], ch["q"]], axis=0), ch["k"])
        yield
        for ch in chains:
            d = ch["d"]
            gc = gcol_ref[pl.ds(ch["r0"], L), :]
            gr = grow_ref[ch["c"]]
            mask_in, mask_t, strict = (le, ge, lt) if d == 0 else (ge, le, gt)
            neg_rate = -jnp.exp(par[:, d:d + 1])
            g_col = neg_rate * _softplus(gc[:, d:d + 1] + par[:, 2 + d:3 + d])
            g_row = neg_rate * _softplus(gr[d:d + 1, :] + par[:, 2 + d:3 + d])
            beta = _sigmoid(gc[:, 2 + d:3 + d])
            G_col = jnp.sum(jnp.where(mask_in, g_row, 0.0), axis=1, keepdims=True)
            G_row = jnp.sum(jnp.where(mask_t, g_col, 0.0), axis=0, keepdims=True)
            dec = jnp.exp(jnp.where(mask_in, G_col - G_row, NEG))
            g_last = jnp.sum(g_row, axis=1, keepdims=True)
            ch.update(beta=beta, dec=dec, eG=jnp.exp(G_col), kscale=jnp.exp(g_last - G_col),
                      egl=jnp.exp(g_last), x=-jnp.where(strict, beta * ch["kkqk"][:L] * dec, 0.0))
        invs = [eye + jnp.where(blk[1], ch["x"], 0.0) for ch in chains]
        bsz = 2
        while bsz < L:
            offs = [jnp.where(blk[bsz], ch["x"], 0.0) for ch in chains]
            tmps = [_dot(off, inv) for off, inv in zip(offs, invs)]
            yield
            invs = [inv + _dot(inv, tmp) for inv, tmp in zip(invs, tmps)]
            yield
            bsz *= 2
        uws = []
        for ch, inv in zip(chains, invs):
            rhs = jnp.concatenate([ch["v"] * ch["beta"], ch["k"] * (ch["beta"] * ch["eG"])], axis=1)
            uws.append(_dot(inv, rhs))
        yield
        kns = [_dot_tn(ch["k"] * ch["kscale"], uw) for ch, uw in zip(chains, uws)]
        yield
        qos = [_dot(ch["kkqk"][L:] * ch["dec"], uw) for ch, uw in zip(chains, uws)]
        yield
        for ch, kn, qo in zip(chains, kns, qos):
            d, it = ch["d"], ch["it"]
            n_ref[d, it] = kn[:, :C_DV]
            kq_ref[d, it, 0:C_DK, :] = kn[:, C_DV:].astype(BF16)
            kq_ref[d, it, C_DK:C_DK + L, :] = (ch["q"] * ch["eG"] - qo[:, C_DV:]).astype(BF16)
            o_ref[d, pl.ds(ch["r0"], L), :] = qo[:, :C_DV]
            egl_ref[d, it] = jnp.broadcast_to(ch["egl"], (8, LANES))

    def recur(it, S):
        out = []
        for d, c in ((0, it), (1, _bwd_chunk(it, nc, ncc))):
            r0 = pl.multiple_of(c * L, L)
            ks_qs = jnp.dot(kq_ref[d, it], S[d].astype(BF16), preferred_element_type=F32)
            o_ref[d, pl.ds(r0, L), :] += ks_qs[C_DK:]
            out.append(S[d] * egl_ref[d, it][0:1, 0:1] + (n_ref[d, it] - ks_qs[:C_DK]))
        return out

    for _ in prep_stages(0):
        pass

    def body(gi, carry):
        S = list(carry)
        stages = prep_stages(gi)
        for j in range(grp):
            next(stages, None)
            S = recur((gi - 1) * grp + j, S)
        for _ in stages:
            pass
        return tuple(S)

    s0 = jnp.zeros((C_DK, C_DV), F32)
    S = list(lax.fori_loop(1, ng, body, (s0, s0)))
    for j in range(grp):
        S = recur((ng - 1) * grp + j, S)

    osum = o_ref[0] + o_ref[1]
    on = osum * lax.rsqrt(jnp.mean(osum * osum, axis=-1, keepdims=True) + EPS) * nw_ref[...]
    y_ref[...] = (on * _silu(z_ref[...])).astype(y_ref.dtype)


def _gdn_call(p, gcol, grow, par, nw, l, ncc):
    b, t, _ = p.shape
    nc = t // CHUNK
    blk = lambda base: pl.BlockSpec((None, t, LANES), lambda i, h: (i, 0, base // LANES + h))
    big = lambda dt: pltpu.VMEM((2, t, LANES), dt)
    return pl.pallas_call(
        functools.partial(_gdn_kernel, nc=nc, ncc=ncc),
        grid=(b, C_HEADS),
        in_specs=[blk(P_CQ), blk(P_CK), blk(P_CV), blk(P_CZ),
                  pl.BlockSpec((None, None, t, 4), lambda i, h: (i, h, 0, 0)),
                  pl.BlockSpec((None, None, nc, 4, CHUNK), lambda i, h: (i, h, 0, 0, 0)),
                  pl.BlockSpec((None, None, 1, 4), lambda i, h: (l, h, 0, 0)),
                  pl.BlockSpec((None, 1, LANES), lambda i, h: (l, 0, 0))],
        out_specs=pl.BlockSpec((None, t, LANES), lambda i, h: (i, 0, h)),
        out_shape=jax.ShapeDtypeStruct((b, t, BRANCH_W), BF16),
        scratch_shapes=[pltpu.VMEM((2, nc, C_DK, C_DV), F32),
                        pltpu.VMEM((2, nc, C_DK + CHUNK, C_DV), BF16),
                        pltpu.VMEM((2, nc, 8, LANES), F32),
                        big(F32)],
        compiler_params=_params(("arbitrary", "arbitrary")),
        name="gdn",
    )(p, p, p, p, gcol, grow, par, nw)


def _merge_kernel(ya_ref, yb_ref, yc_ref, ga_ref, gb_ref, gc_ref, h_ref, mod_ref, wb_ref, wo_ref, o_ref):
    y = (_sigmoid(ga_ref[...].astype(F32)) * _dot(ya_ref[...], wb_ref[0])
         + _sigmoid(gb_ref[...].astype(F32)) * _dot(yb_ref[...], wb_ref[1])
         + _sigmoid(gc_ref[...].astype(F32)) * _dot(yc_ref[...], wb_ref[2]))
    o_ref[...] = h_ref[...] + mod_ref[2:3, :] * _dot(y, wo_ref[...])


def _mod_row(j, nct):
    return 1 if nct == 0 else jnp.minimum(j // nct, 1)


def _merge_call(ya, yb, yc, gates, h, mod, wb, wo, l, tm, nct, latent_only):
    b, t, d = h.shape
    skip = nct if latent_only else 0
    yblk = lambda: pl.BlockSpec((None, tm, BRANCH_W), lambda i, j: (i, j + skip, 0))
    yb_skip = skip - (t - yb.shape[1]) // tm
    gblk = lambda g: pl.BlockSpec((None, tm, d), lambda i, j: (i, j + skip, g))
    return pl.pallas_call(
        _merge_kernel,
        grid=(b, t // tm - skip),
        in_specs=[yblk(), pl.BlockSpec((None, tm, BRANCH_W), lambda i, j: (i, j + yb_skip, 0)), yblk(),
                  gblk(0), gblk(1), gblk(2),
                  pl.BlockSpec((None, tm, d), lambda i, j: (i, j + skip, 0)),
                  pl.BlockSpec((None, None, None, 6, d), lambda i, j: (l, i, _mod_row(j + skip, nct), 0, 0)),
                  _resident((None, 3, BRANCH_W, d), lambda i, j: (l, 0, 0, 0)),
                  _resident((None, d, d), lambda i, j: (l, 0, 0))],
        out_specs=pl.BlockSpec((None, tm, d), lambda i, j: (i, j, 0)),
        out_shape=jax.ShapeDtypeStruct((b, t - skip * tm, d), F32),
        compiler_params=_params(("arbitrary", "arbitrary")),
        name="merge",
    )(ya, yb, yc, gates, gates, gates, h, mod, wb, wo)


def _ffn_kernel(h_ref, hp_ref, hn_ref, mod_ref, nw_ref, wup_ref, cw_ref, wdn_ref, o_ref, acc_ref,
                *, nct, ntiles):
    tm = h_ref.shape[0]
    j = pl.program_id(1)
    mod = mod_ref[...]
    nw = nw_ref[...]
    h = h_ref[...]
    xn = _modnorm(h, nw, mod[3:4], mod[4:5]).astype(BF16)
    halo = jnp.concatenate([hp_ref[...], hn_ref[...]], axis=0)
    xh = _modnorm(halo, nw, mod[3:4], mod[4:5]).astype(BF16)
    has_prev = jnp.logical_and(j != 0, j != nct).astype(F32)
    has_next = jnp.logical_and(j != nct - 1, j != ntiles - 1).astype(F32)
    rows = lax.broadcasted_iota(jnp.int32, (tm, 1), 0)
    first = rows == 0
    last = rows == tm - 1

    def cols(kind, jf):
        return slice(kind * D_FF + jf * FFN_TF, kind * D_FF + (jf + 1) * FFN_TF)

    x_all = jnp.concatenate([xn, xh], axis=0)

    def up(jf):
        us = [jnp.dot(x_all, wup_ref[:, cols(kind, jf)], preferred_element_type=F32) for kind in range(2)]
        return [(u[:tm], u[tm:]) for u in us]

    def conv(kind, jf, u, uh):
        cw = cw_ref[:, cols(kind, jf)]
        prev = jnp.where(first, uh[7:8] * has_prev, pltpu.roll(u, 1, axis=0))
        nxt = jnp.where(last, uh[8:9] * has_next, pltpu.roll(u, tm - 1, axis=0))
        return prev * cw[0:1] + u * cw[1:2] + nxt * cw[2:3]

    nj = D_FF // FFN_TF
    ahead = [up(jf) for jf in range(min(FFN_AHEAD, nj))]
    acts = []
    for jf in range(nj):
        if jf + FFN_AHEAD < nj:
            ahead.append(up(jf + FFN_AHEAD))
        cur = ahead.pop(0)
        acts.append((conv(0, jf, *cur[0]) * _silu(conv(1, jf, *cur[1]))).astype(BF16))
        if len(acts) == FFN_DOWN_GROUP or jf == nj - 1:
            r1 = (jf + 1) * FFN_TF
            part = jnp.dot(jnp.concatenate(acts, axis=1), wdn_ref[r1 - len(acts) * FFN_TF:r1, :],
                           preferred_element_type=F32)
            if r1 == len(acts) * FFN_TF:
                acc_ref[...] = part
            else:
                acc_ref[...] += part
            acts = []
    o_ref[...] = h + mod[5:6] * acc_ref[...]


def _ffn_call(h, mod, nw, wup, cw, wdn, l, tm, nct):
    b, t, d = h.shape
    ntiles = t // tm
    hb = tm // 8
    return pl.pallas_call(
        functools.partial(_ffn_kernel, nct=nct, ntiles=ntiles),
        grid=(b, ntiles),
        in_specs=[pl.BlockSpec((None, tm, d), lambda i, j: (i, j, 0)),
                  pl.BlockSpec((None, 8, d), lambda i, j: (i, jnp.maximum(j * hb - 1, 0), 0)),
                  pl.BlockSpec((None, 8, d), lambda i, j: (i, jnp.minimum((j + 1) * hb, t // 8 - 1), 0)),
                  pl.BlockSpec((None, None, None, 6, d), lambda i, j: (l, i, _mod_row(j, nct), 0, 0)),
                  pl.BlockSpec((None, 1, d), lambda i, j: (l, 0, 0)),
                  _resident((None,) + wup.shape[1:], lambda i, j: (l, 0, 0)),
                  _resident((None,) + cw.shape[1:], lambda i, j: (l, 0, 0)),
                  _resident((None,) + wdn.shape[1:], lambda i, j: (l, 0, 0))],
        out_specs=pl.BlockSpec((None, tm, d), lambda i, j: (i, j, 0)),
        out_shape=jax.ShapeDtypeStruct((b, t, d), F32),
        scratch_shapes=[pltpu.VMEM((tm, d), F32)],
        compiler_params=_params(("arbitrary", "arbitrary")),
        name="ffn",
    )(h, h, h, mod, nw, wup, cw, wdn)


def _rope_tables(tc, tl):
    rows = tl // GRID_W
    row = jnp.repeat(jnp.arange(rows, dtype=F32), GRID_W)
    col = jnp.tile(jnp.arange(GRID_W, dtype=F32), rows)
    n_freq = B_DH // 4
    inv = ROPE_BASE ** (-jnp.arange(n_freq, dtype=F32) / n_freq)
    ang_r = row[:, None] * inv
    ang_c = col[:, None] * inv
    cos = jnp.concatenate([jnp.cos(ang_r)] * 2 + [jnp.cos(ang_c)] * 2, axis=1)
    sin = jnp.concatenate([-jnp.sin(ang_r), jnp.sin(ang_r), -jnp.sin(ang_c), jnp.sin(ang_c)], axis=1)
    cos = jnp.concatenate([jnp.ones((tc, B_DH), F32), cos], axis=0)
    sin = jnp.concatenate([jnp.zeros((tc, B_DH), F32), sin], axis=0)
    return jnp.tile(cos, (1, 2)), jnp.tile(sin, (1, 2))


def _pack_kernel(w_ref, o_ref):
    def src(c0, c1):
        return w_ref[:, c0:c1].astype(BF16)

    o_ref[:, 0:W_GATES] = src(4384, 7456)
    o_ref[:, W_AV:W_BQ] = src(512, 1536)
    o_ref[:, W_BQ:W_CQ] = src(1552, 2320)
    o_ref[:, W_CQ:W_AQK] = src(2320, 4368)
    for hd in range(A_HEADS):
        c0 = W_AQK + 2 * A_DQK * hd
        o_ref[:, c0:c0 + A_DQK] = src(A_DQK * hd, A_DQK * (hd + 1))
        o_ref[:, c0 + A_DQK:c0 + 2 * A_DQK] = src(256 + A_DQK * hd, 256 + A_DQK * (hd + 1))
    sm = jnp.concatenate([src(1536, 1552), src(4368, 4384)], axis=1)
    r = lax.broadcasted_iota(jnp.int32, (32, LANES), 0)
    c = lax.broadcasted_iota(jnp.int32, (32, LANES), 1)
    perm = jnp.where((c < 32) & (r == (c // 16) * 16 + (c % 4) * 4 + (c % 16) // 4), 1.0, 0.0).astype(BF16)
    o_ref[:, W_SMALL:W_COLS] = jnp.dot(sm, perm, preferred_element_type=F32).astype(BF16)


def _pack_call(w_in):
    depth, d, n = w_in.shape
    rows = 256
    return pl.pallas_call(
        _pack_kernel,
        grid=(depth, d // rows),
        in_specs=[pl.BlockSpec((None, rows, n), lambda l, i: (l, i, 0))],
        out_specs=pl.BlockSpec((None, rows, W_COLS), lambda l, i: (l, i, 0)),
        out_shape=jax.ShapeDtypeStruct((depth, d, W_COLS), BF16),
        compiler_params=_params(("arbitrary", "arbitrary")),
        name="pack_w_in",
    )(w_in)


def kernel(x, c, ctx, c_ctx, norm1_w, norm2_w, ada_w, ada_b, w_in, a_gate_b, a_norm_w, b_qnorm_w,
           b_knorm_w, c_conv_w, c_a_log, c_dt_bias, c_norm_w, w_branch, w_out, w_up, ffn_conv_w, w_down):
    b, tl, d = x.shape
    tc = ctx.shape[1]
    depth = w_in.shape[0]
    t = tc + tl
    tm = 256 if (tc % 256 == 0 and tl % 256 == 0) else 128
    nct = tc // tm
    ncc = tc // CHUNK
    nj = D_FF // FFN_TF

    cc = jnp.zeros((16, d), F32).at[:b].set(c).at[b].set(c_ctx)
    mods = _ada_call(cc, ada_w, ada_b).reshape(depth, 16, 6, d)
    mod = jnp.stack([jnp.broadcast_to(mods[:, b][:, None], (depth, b, 6, d)), mods[:, :b]], axis=2)

    w_in_p = _pack_call(w_in)
    wb = w_branch.astype(BF16)
    wo = w_out.astype(BF16)
    wup = w_up.astype(BF16)
    wdn = w_down.astype(BF16)
    cos2, sin2 = _rope_tables(tc, tl)
    qw2 = jnp.tile(b_qnorm_w, (1, 2))[:, None]
    kw2 = jnp.tile(b_knorm_w, (1, 2))[:, None]
    a_bcol = a_gate_b.transpose(0, 2, 1)[:, :, :, None]
    c_par = jnp.concatenate([c_a_log, c_dt_bias], axis=1).transpose(0, 2, 1)[:, :, None, :]

    h = jnp.concatenate([ctx, x], axis=1)
    for l in range(depth):
        last = l == depth - 1
        gates, p, bq, bk, qkt, a_grow, c_grow, c_gcol = _proj_call(h, mod, norm1_w[:, None], w_in_p, c_conv_w, qw2, kw2, cos2, sin2,
                                                  l, tm, nct)
        ya = _mlstm_call(p, qkt, a_grow, a_bcol, a_norm_w[:, None], l, ncc)
        yb = _gqa_call(bq, bk, p, tc, latent_only=last)
        yc = _gdn_call(p, c_gcol, c_grow, c_par, c_norm_w[:, None], l, ncc)
        h = _merge_call(ya, yb, yc, gates, h, mod, wb, wo, l, tm, nct, latent_only=last)
        h = _ffn_call(h, mod, norm2_w[:, None], wup, ffn_conv_w, wdn, l, tm, 0 if last else nct)
    return h
```

```python
import functools

import jax
import jax.numpy as jnp
from jax import lax
from jax.experimental import pallas as pl
from jax.experimental.pallas import tpu as pltpu

F32 = jnp.float32
BF16 = jnp.bfloat16

D_MODEL = 1024
GRID_W = 64
A_HEADS, A_DQK, A_DV = 4, 64, 128
B_HEADS, B_KV_HEADS, B_DH = 8, 2, 64
C_HEADS, C_DK, C_DV = 4, 128, 128
BRANCH_W = 512
D_FF = 2816
CHUNK = 64
ROPE_BASE = 10000.0
EPS = 1e-6
M_INIT = -1e30
LOG2_E = 1.4426950408889634
NEG = -1e30

LANES = 128
W_GATES = 3 * D_MODEL
W_AV = 3072
W_BQ = 4096
W_BK = 4608
W_CQ = 4864
W_CZ = 6400
W_AQK = 6912
W_SMALL = 7424
W_COLS = 7552
P_AV = 0
P_AO = 512
P_BV = 1024
P_CQ = 1152
P_CK = 1664
P_CV = 2176
P_CZ = 2688
P_COLS = 3200
PROJ_TN = 512
FFN_TF = 256
FFN_AHEAD = 3
FFN_DOWN_GROUP = 4
Q_TILE = 256
SCAN_GROUP = 12
MLSTM_GROUP = 36
VMEM_LIMIT = 56 * 1024 * 1024


def _dot(a, b):
    return jnp.dot(a.astype(BF16), b.astype(BF16), preferred_element_type=F32)


def _dot_nt(a, b):
    return lax.dot_general(a.astype(BF16), b.astype(BF16), (((1,), (1,)), ((), ())),
                           preferred_element_type=F32)


def _dot_tn(a, b):
    return lax.dot_general(a.astype(BF16), b.astype(BF16), (((0,), (0,)), ((), ())),
                           preferred_element_type=F32)


def _split(a):
    hi = a.astype(BF16)
    lo = (a - hi.astype(F32)).astype(BF16)
    return hi, lo


def _dot3(a, b):
    ah, al = _split(a)
    bh, bl = _split(b)
    d = functools.partial(jnp.dot, preferred_element_type=F32)
    return d(ah, bh) + (d(al, bh) + d(ah, bl))


def _sigmoid(x):
    return 1.0 / (1.0 + jnp.exp(-x))


def _silu(x):
    return x * _sigmoid(x)


def _softplus(x):
    return jnp.maximum(x, 0.0) + jnp.log1p(jnp.exp(-jnp.abs(x)))


def _logsigmoid(x):
    return -_softplus(-x)


def _modnorm(x, w, shift, scale):
    y = x * lax.rsqrt(jnp.mean(x * x, axis=-1, keepdims=True) + EPS)
    return (y * w) * (1.0 + scale) + shift


def _params(sem, vmem=VMEM_LIMIT):
    return pltpu.CompilerParams(dimension_semantics=sem, vmem_limit_bytes=vmem)


def _resident(shape, index_map):
    return pl.BlockSpec(shape, index_map, pipeline_mode=pl.Buffered(1))


def _ada_kernel(c_ref, w_ref, b_ref, o_ref):
    o_ref[...] = _dot3(_silu(c_ref[...]), w_ref[...]) + b_ref[...]


def _ada_call(cc, ada_w, ada_b):
    depth, d, n = ada_w.shape
    tn = 1536
    return pl.pallas_call(
        _ada_kernel,
        grid=(depth, n // tn),
        in_specs=[pl.BlockSpec(cc.shape, lambda l, j: (0, 0)),
                  pl.BlockSpec((None, d, tn), lambda l, j: (l, 0, j)),
                  pl.BlockSpec((None, 1, tn), lambda l, j: (l, 0, j))],
        out_specs=pl.BlockSpec((None, cc.shape[0], tn), lambda l, j: (l, 0, j)),
        out_shape=jax.ShapeDtypeStruct((depth, cc.shape[0], n), F32),
        compiler_params=_params(("arbitrary", "arbitrary")),
        name="ada",
    )(cc, ada_w, ada_b.reshape(depth, 1, n))


def _proj_kernel(h_ref, hp_ref, hn_ref, mod_ref, nw_ref, w_ref, cw_ref, qw_ref, kw_ref, cos_ref, sin_ref,
                 g_ref, p_ref, bq_ref, bk_ref, qkt_ref, agrow_ref, cgrow_ref, cgcol_ref, *, nct, ntiles):
    tm = h_ref.shape[0]
    j = pl.program_id(1)
    mod = mod_ref[...]
    nw = nw_ref[...]
    xn = _modnorm(h_ref[...], nw, mod[0:1], mod[1:2]).astype(BF16)

    def proj(c0, width):
        return jnp.dot(xn, w_ref[:, c0:c0 + width], preferred_element_type=F32)

    for c0 in range(0, W_GATES, PROJ_TN):
        g_ref[:, c0:c0 + PROJ_TN] = proj(c0, PROJ_TN).astype(BF16)
    for c0 in range(0, P_BV, PROJ_TN):
        p_ref[:, c0:c0 + PROJ_TN] = proj(W_AV + c0, PROJ_TN)
    p_ref[:, P_CZ:P_COLS] = proj(W_CZ, P_COLS - P_CZ)

    r = lax.broadcasted_iota(jnp.int32, (LANES, LANES), 0)
    c = lax.broadcasted_iota(jnp.int32, (LANES, LANES), 1)
    bd = jnp.where(r // B_DH == c // B_DH, 1.0, 0.0).astype(BF16)
    lane_half = (lax.broadcasted_iota(jnp.int32, (tm, LANES), 1) % 32) < 16
    cos, sin = cos_ref[...], sin_ref[...]
    bq = proj(W_BQ, B_HEADS * B_DH)
    for s0 in range(0, B_HEADS * B_DH, LANES):
        y = _headnorm_rope(bq[:, s0:s0 + LANES], bd, qw_ref[...], cos, sin, lane_half)
        bq_ref[:, s0:s0 + LANES] = (y * (B_DH ** -0.5 * LOG2_E)).astype(BF16)
    kv = proj(W_BK, 2 * LANES)
    bk_ref[...] = _headnorm_rope(kv[:, :LANES], bd, kw_ref[...], cos, sin, lane_half).astype(BF16)
    p_ref[:, P_BV:P_BV + LANES] = kv[:, LANES:]

    halo = jnp.concatenate([hp_ref[...], hn_ref[...]], axis=0)
    x_all = jnp.concatenate([xn, _modnorm(halo, nw, mod[0:1], mod[1:2]).astype(BF16)], axis=0)
    has_prev = jnp.logical_and(j != 0, j != nct).astype(F32)
    has_next = jnp.logical_and(j != nct - 1, j != ntiles - 1).astype(F32)
    rows = lax.broadcasted_iota(jnp.int32, (tm, 1), 0)
    width = C_HEADS * C_DK
    for part in range(3):
        c0 = part * width
        u_all = jnp.dot(x_all, w_ref[:, W_CQ + c0:W_CQ + c0 + width], preferred_element_type=F32)
        u = u_all[:tm]
        cw = cw_ref[:, c0:c0 + width]
        prev = jnp.where(rows == 0, u_all[tm + 7:tm + 8] * has_prev, pltpu.roll(u, 1, axis=0))
        nxt = jnp.where(rows == tm - 1, u_all[tm + 8:tm + 9] * has_next, pltpu.roll(u, tm - 1, axis=0))
        y = _silu(prev * cw[0:1] + u * cw[1:2] + nxt * cw[2:3])
        for hd in range(C_HEADS):
            yh = y[:, hd * C_DK:(hd + 1) * C_DK]
            if part < 2:
                yh = yh * lax.rsqrt(jnp.sum(yh * yh, axis=-1, keepdims=True) + EPS)
            if part == 0:
                yh = yh * (C_DK ** -0.5)
            p_ref[:, P_CQ + c0 + hd * C_DK:P_CQ + c0 + (hd + 1) * C_DK] = yh
    qk = jnp.dot(xn, w_ref[:, W_AQK:W_SMALL], preferred_element_type=F32)
    for hd in range(A_HEADS):
        for c in range(tm // CHUNK):
            qkt_ref[hd, c] = qk[c * CHUNK:(c + 1) * CHUNK, hd * LANES:(hd + 1) * LANES].T
    small = jnp.dot(xn, w_ref[:, W_SMALL:W_COLS], preferred_element_type=F32)
    for hd in range(C_HEADS):
        cgcol_ref[hd] = small[:, 4 * A_HEADS + 4 * hd:4 * A_HEADS + 4 * hd + 4]
    for c in range(tm // CHUNK):
        st = small[c * CHUNK:(c + 1) * CHUNK, :].T
        for hd in range(A_HEADS):
            agrow_ref[hd, c] = st[4 * hd:4 * hd + 4]
        for hd in range(C_HEADS):
            cgrow_ref[hd, c] = st[4 * A_HEADS + 4 * hd:4 * A_HEADS + 4 * hd + 4]


def _proj_call(h, mod, nw, w, cw, qw, kw, cos2, sin2, l, tm, nct):
    b, t, d = h.shape
    cpt = tm // CHUNK
    ntiles = t // tm
    hb = tm // 8
    vec = lambda: pl.BlockSpec((None, 1, LANES), lambda i, j: (l, 0, 0))
    table = lambda: pl.BlockSpec((tm, LANES), lambda i, j: (j, 0))
    return pl.pallas_call(
        functools.partial(_proj_kernel, nct=nct, ntiles=ntiles),
        grid=(b, ntiles),
        in_specs=[pl.BlockSpec((None, tm, d), lambda i, j: (i, j, 0)),
                  pl.BlockSpec((None, 8, d), lambda i, j: (i, jnp.maximum(j * hb - 1, 0), 0)),
                  pl.BlockSpec((None, 8, d), lambda i, j: (i, jnp.minimum((j + 1) * hb, t // 8 - 1), 0)),
                  pl.BlockSpec((None, None, None, 6, d), lambda i, j: (l, i, _mod_row(j, nct), 0, 0)),
                  pl.BlockSpec((None, 1, d), lambda i, j: (l, 0, 0)),
                  _resident((None, d, W_COLS), lambda i, j: (l, 0, 0)),
                  _resident((None,) + cw.shape[1:], lambda i, j: (l, 0, 0)),
                  vec(), vec(), table(), table()],
        out_specs=[pl.BlockSpec((None, tm, W_GATES), lambda i, j: (i, j, 0)),
                   pl.BlockSpec((None, tm, P_COLS), lambda i, j: (i, j, 0)),
                   pl.BlockSpec((None, tm, B_HEADS * B_DH), lambda i, j: (i, j, 0)),
                   pl.BlockSpec((None, tm, LANES), lambda i, j: (i, j, 0)),
                   pl.BlockSpec((None, A_HEADS, cpt, 2 * A_DQK, CHUNK), lambda i, j: (i, 0, j, 0, 0)),
                   pl.BlockSpec((None, A_HEADS, cpt, 4, CHUNK), lambda i, j: (i, 0, j, 0, 0)),
                   pl.BlockSpec((None, C_HEADS, cpt, 4, CHUNK), lambda i, j: (i, 0, j, 0, 0)),
                   pl.BlockSpec((None, C_HEADS, tm, 4), lambda i, j: (i, 0, j, 0))],
        out_shape=[jax.ShapeDtypeStruct((b, t, W_GATES), BF16),
                   jax.ShapeDtypeStruct((b, t, P_COLS), F32),
                   jax.ShapeDtypeStruct((b, t, B_HEADS * B_DH), BF16),
                   jax.ShapeDtypeStruct((b, t, LANES), BF16),
                   jax.ShapeDtypeStruct((b, A_HEADS, t // CHUNK, 2 * A_DQK, CHUNK), F32),
                   jax.ShapeDtypeStruct((b, A_HEADS, t // CHUNK, 4, CHUNK), F32),
                   jax.ShapeDtypeStruct((b, C_HEADS, t // CHUNK, 4, CHUNK), F32),
                   jax.ShapeDtypeStruct((b, C_HEADS, t, 4), F32)],
        compiler_params=_params(("arbitrary", "arbitrary")),
        name="proj_in",
    )(h, h, h, mod, nw, w, cw, qw, kw, cos2, sin2)


def _chunk_masks():
    row = lax.broadcasted_iota(jnp.int32, (CHUNK, CHUNK), 0)
    col = lax.broadcasted_iota(jnp.int32, (CHUNK, CHUNK), 1)
    return col <= row, col >= row, col < row, col > row


def _bwd_chunk(it, nc, ncc):
    return jnp.where(it < ncc, ncc - 1 - it, nc - 1 - (it - ncc))


def _rows3(x, n_rows=16):
    hi = x.astype(BF16).astype(F32)
    mid = (x - hi).astype(BF16).astype(F32)
    lo = ((x - hi) - mid).astype(BF16).astype(F32)
    r = lax.broadcasted_iota(jnp.int32, (n_rows, x.shape[1]), 0)
    return jnp.where(r == 0, hi, jnp.where(r == 1, mid, jnp.where(r == 2, lo, 0.0))).astype(BF16)


def _mlstm_kernel(qkt_ref, v_ref, o_ref, grow_ref, bcol_ref, nw_ref, y_ref, hf_ref, hb_ref, *, nc, ncc):
    L = CHUNK
    le, ge, _, _ = _chunk_masks()
    vis = (ge, le)
    row_l = lax.broadcasted_iota(jnp.int32, (L, LANES), 0)
    col_l = lax.broadcasted_iota(jnp.int32, (L, LANES), 1)
    ones_l = jnp.ones((L, LANES), BF16)
    zeros_l = jnp.zeros((L, LANES), BF16)
    stat_rhs = [jnp.concatenate(
        [ones_l, jnp.where((col_l < L) & ((col_l >= row_l) if d == 0 else (col_l <= row_l)), 1.0, 0.0).astype(BF16)],
        axis=1) for d in range(2)]
    ones3 = jnp.where(lax.broadcasted_iota(jnp.int32, (16, LANES), 0) < 3, 1.0, 0.0).astype(BF16)
    tail_rhs = jnp.concatenate([jnp.zeros((16, 2 * LANES), BF16), ones3], axis=1)
    bias_col = bcol_ref[...]
    grp = _group_size(nc, MLSTM_GROUP)
    h_refs = (hf_ref, hb_ref)

    def sum3(x):
        return x[0:1] + x[1:2] + x[2:3]

    def body(gi, carry):
        C = [carry[0], carry[2]]
        m = [carry[1], carry[3]]
        chains = []
        for j in range(grp):
            it = gi * grp + j
            for d, c in ((0, it), (1, _bwd_chunk(it, nc, ncc))):
                gr = grow_ref[c] + bias_col
                x = qkt_ref[c]
                chains.append(dict(d=d, r0=pl.multiple_of(c * L, L), i_row=gr[2 * d:2 * d + 1, :],
                                   f3=_rows3(_logsigmoid(gr[2 * d + 1:2 * d + 2, :])),
                                   qt=x[:A_DQK] * (A_DQK ** -0.5), kt=x[A_DQK:]))
        for ch in chains:
            ch["v"] = v_ref[pl.ds(ch["r0"], L), :].astype(BF16)
            ch["sqt"] = _dot_tn(ch["kt"], ch["qt"])
        for ch in chains:
            st = jnp.dot(ch["f3"], stat_rhs[ch["d"]], preferred_element_type=F32)
            ch["btot"] = sum3(st[:, :LANES])
            ch["bcum"] = sum3(st[:, LANES:LANES + L])
            ch["a_row"] = ch["i_row"] - ch["bcum"]
        for ch in chains:
            ch["a_col"] = lax.dot_general(_rows3(ch["a_row"]), ones3, (((0,), (0,)), ((), ())),
                                          preferred_element_type=F32)
        for ch in chains:
            d = ch["d"]
            m_new = jnp.maximum(ch["btot"] + m[d], ch["btot"] + jnp.max(ch["a_col"], axis=0, keepdims=True))
            dlog = jnp.where(vis[d], ch["bcum"] + ch["a_col"][:, :L], NEG)
            inter = ch["bcum"] + m[d][:, :L]
            mt = jnp.maximum(inter, jnp.max(dlog, axis=0, keepdims=True))
            ch["st"] = (ch["sqt"] * jnp.exp(dlog - mt)).astype(BF16)
            ch["qe"] = (ch["qt"] * jnp.exp(inter - mt)).astype(BF16)
            ch["e3"] = _rows3(jnp.exp(-mt))
            ch["kw"] = (ch["kt"] * jnp.exp(ch["btot"][:, :L] + ch["a_row"] - m_new[:, :L])).astype(BF16)
            decay = jnp.exp(ch["btot"] + m[d] - m_new)
            ch["decay"] = jnp.concatenate([decay, decay], axis=1)
            m[d] = m_new
        for ch in chains:
            ch["kv"] = jnp.dot(ch["kw"], jnp.concatenate([ch["v"], ones_l], axis=1), preferred_element_type=F32)
            lhs = jnp.concatenate([ch["st"], ch["e3"]], axis=0)
            rhs = jnp.concatenate([jnp.concatenate([ch["v"], ones_l, zeros_l], axis=1), tail_rhs], axis=0)
            ch["intra"] = lax.dot_general(lhs, rhs, (((0,), (0,)), ((), ())), preferred_element_type=F32)
        for ch in chains:
            d = ch["d"]
            ch["inter"] = lax.dot_general(ch["qe"], C[d].astype(BF16), (((0,), (0,)), ((), ())),
                                          preferred_element_type=F32)
            C[d] = ch["decay"] * C[d] + ch["kv"]
        for ch in chains:
            num = ch["intra"][:, :A_DV] + ch["inter"][:, :A_DV]
            den = ch["intra"][:, A_DV:2 * A_DV] + ch["inter"][:, A_DV:]
            h_refs[ch["d"]][pl.ds(ch["r0"], L), :] = num / jnp.maximum(jnp.abs(den), ch["intra"][:, 2 * A_DV:])
        return C[0], m[0], C[1], m[1]

    c0 = jnp.zeros((A_DQK, 2 * LANES), F32)
    m0 = jnp.full((1, LANES), M_INIT, F32)
    lax.fori_loop(0, nc // grp, body, (c0, m0, c0, m0))

    hsum = hf_ref[...] + hb_ref[...]
    hn = hsum * lax.rsqrt(jnp.mean(hsum * hsum, axis=-1, keepdims=True) + EPS)
    y_ref[...] = (hn * nw_ref[...] * _sigmoid(o_ref[...])).astype(y_ref.dtype)


def _mlstm_call(p, qkt, grow, bcol, nw, l, ncc):
    b, t, _ = p.shape
    nc = t // CHUNK
    blk = lambda base: pl.BlockSpec((None, t, LANES), lambda i, h: (i, 0, base // LANES + h))
    return pl.pallas_call(
        functools.partial(_mlstm_kernel, nc=nc, ncc=ncc),
        grid=(b, A_HEADS),
        in_specs=[pl.BlockSpec((None, None, nc, 2 * A_DQK, CHUNK), lambda i, h: (i, h, 0, 0, 0)),
                  blk(P_AV), blk(P_AO),
                  pl.BlockSpec((None, None, nc, 4, CHUNK), lambda i, h: (i, h, 0, 0, 0)),
                  pl.BlockSpec((None, None, 4, 1), lambda i, h: (l, h, 0, 0)),
                  pl.BlockSpec((None, 1, LANES), lambda i, h: (l, 0, h))],
        out_specs=pl.BlockSpec((None, t, LANES), lambda i, h: (i, 0, h)),
        out_shape=jax.ShapeDtypeStruct((b, t, BRANCH_W), BF16),
        scratch_shapes=[pltpu.VMEM((t, LANES), F32), pltpu.VMEM((t, LANES), F32)],
        compiler_params=_params(("arbitrary", "arbitrary")),
        name="mlstm",
    )(qkt, p, p, grow, bcol, nw)


def _headnorm_rope(x, bd, w, cos, sin_signed, lane_half):
    hi, lo = _split(x * x)
    ssum = jnp.dot(hi, bd, preferred_element_type=F32) + jnp.dot(lo, bd, preferred_element_type=F32)
    y = x * lax.rsqrt(ssum * (1.0 / B_DH) + EPS) * w
    swapped = jnp.where(lane_half, pltpu.roll(y, LANES - 16, axis=1), pltpu.roll(y, 16, axis=1))
    return y * cos + swapped * sin_signed


def _gqa_kernel(q_ref, k_ref, v_ref, y_ref, vb_ref, *, tc, first_q):
    tq = q_ref.shape[0]
    t = k_ref.shape[0]
    qb = pl.program_id(1) + first_q

    @pl.when(pl.program_id(1) == 0)
    def _():
        v = v_ref[...]
        first = lax.broadcasted_iota(jnp.int32, (t, LANES), 1) < B_DH
        vb_ref[0] = jnp.where(first, v, 1.0).astype(BF16)
        vb_ref[1] = jnp.where(first, pltpu.roll(v, B_DH, axis=1), 1.0).astype(BF16)

    group = B_HEADS // B_KV_HEADS

    def attend(nk):
        scores = []
        for kvh in range(B_KV_HEADS):
            kh = k_ref[0:nk, kvh * B_DH:(kvh + 1) * B_DH]
            qs = jnp.concatenate([q_ref[:, h * B_DH:(h + 1) * B_DH]
                                  for h in range(kvh * group, (kvh + 1) * group)], axis=0)
            scores.append(lax.dot_general(qs, kh, (((1,), (1,)), ((), ())), preferred_element_type=F32))
        for kvh, s in enumerate(scores):
            p = jnp.exp2(s - jnp.max(s, axis=-1, keepdims=True))
            oe = jnp.dot(p.astype(BF16), vb_ref[kvh, 0:nk, :], preferred_element_type=F32)
            o = oe * pltpu.roll(1.0 / oe, B_DH, axis=1)
            for g in range(group):
                h = kvh * group + g
                y_ref[:, h * B_DH:(h + 1) * B_DH] = o[g * tq:(g + 1) * tq, :B_DH].astype(y_ref.dtype)

    nqc = tc // tq
    if first_q < nqc:
        @pl.when(qb < nqc)
        def _():
            attend(tc)

    @pl.when(qb >= nqc)
    def _():
        attend(t)


def _gqa_call(bq, bk, p, tc, latent_only):
    b, t, _ = p.shape
    tq = Q_TILE if (tc % Q_TILE == 0 and t % Q_TILE == 0) else Q_TILE // 2
    first_q = tc // tq if latent_only else 0
    return pl.pallas_call(
        functools.partial(_gqa_kernel, tc=tc, first_q=first_q),
        grid=(b, t // tq - first_q),
        in_specs=[pl.BlockSpec((None, tq, B_HEADS * B_DH), lambda i, j: (i, j + first_q, 0)),
                  pl.BlockSpec((None, t, LANES), lambda i, j: (i, 0, 0)),
                  pl.BlockSpec((None, t, LANES), lambda i, j: (i, 0, P_BV // LANES))],
        out_specs=pl.BlockSpec((None, tq, BRANCH_W), lambda i, j: (i, j, 0)),
        out_shape=jax.ShapeDtypeStruct((b, t - first_q * tq, BRANCH_W), BF16),
        scratch_shapes=[pltpu.VMEM((B_KV_HEADS, t, LANES), BF16)],
        compiler_params=_params(("arbitrary", "arbitrary")),
        name="gqa",
    )(bq, bk, p)


def _group_size(nc, limit=None):
    return max(g for g in range(1, (limit or SCAN_GROUP) + 1) if nc % g == 0)


def _gdn_kernel(qn_ref, kn_ref, vn_ref, z_ref, gcol_ref, grow_ref, par_ref, nw_ref, y_ref,
                n_ref, kq_ref, egl_ref, o_ref, *, nc, ncc):
    L = CHUNK
    le, ge, lt, gt = _chunk_masks()
    par = par_ref[...]

    grp = _group_size(nc)
    ng = nc // grp
    eye = jnp.where(le & ge, 1.0, 0.0).astype(F32)
    rowi = lax.broadcasted_iota(jnp.int32, (L, L), 0)
    coli = lax.broadcasted_iota(jnp.int32, (L, L), 1)
    blk = {b_: (rowi // (2 * b_) == coli // (2 * b_)) & (rowi // b_ != coli // b_) for b_ in (1, 2, 4, 8, 16, 32)}

    def prep_stages(gi):
        chains = []
        for j in range(grp):
            it = gi * grp + j
            for d, c in ((0, it), (1, _bwd_chunk(it, nc, ncc))):
                r0 = pl.multiple_of(c * L, L)
                chains.append(dict(it=it, d=d, c=c, r0=r0, q=qn_ref[pl.ds(r0, L), :],
                                   k=kn_ref[pl.ds(r0, L), :], v=vn_ref[pl.ds(r0, L), :]))
        for ch in chains:
            ch["kkqk"] = _dot_nt(jnp.concatenate([ch["k"], ch["q"]], axis=0), ch["k"])
        yield
        for ch in chains:
            d = ch["d"]
            gc = gcol_ref[pl.ds(ch["r0"], L), :]
            gr = grow_ref[ch["c"]]
            mask_in, mask_t, strict = (le, ge, lt) if d == 0 else (ge, le, gt)
            neg_rate = -jnp.exp(par[:, d:d + 1])
            g_col = neg_rate * _softplus(gc[:, d:d + 1] + par[:, 2 + d:3 + d])
            g_row = neg_rate * _softplus(gr[d:d + 1, :] + par[:, 2 + d:3 + d])
            beta = _sigmoid(gc[:, 2 + d:3 + d])
            G_col = jnp.sum(jnp.where(mask_in, g_row, 0.0), axis=1, keepdims=True)
            G_row = jnp.sum(jnp.where(mask_t, g_col, 0.0), axis=0, keepdims=True)
            dec = jnp.exp(jnp.where(mask_in, G_col - G_row, NEG))
            g_last = jnp.sum(g_row, axis=1, keepdims=True)
            ch.update(beta=beta, dec=dec, eG=jnp.exp(G_col), kscale=jnp.exp(g_last - G_col),
                      egl=jnp.exp(g_last), x=-jnp.where(strict, beta * ch["kkqk"][:L] * dec, 0.0))
        invs = [eye + jnp.where(blk[1], ch["x"], 0.0) for ch in chains]
        bsz = 2
        while bsz < L:
            offs = [jnp.where(blk[bsz], ch["x"], 0.0) for ch in chains]
            tmps = [_dot(off, inv) for off, inv in zip(offs, invs)]
            yield
            invs = [inv + _dot(inv, tmp) for inv, tmp in zip(invs, tmps)]
            yield
            bsz *= 2
        uws = []
        for ch, inv in zip(chains, invs):
            rhs = jnp.concatenate([ch["v"] * ch["beta"], ch["k"] * (ch["beta"] * ch["eG"])], axis=1)
            uws.append(_dot(inv, rhs))
        yield
        kns = [_dot_tn(ch["k"] * ch["kscale"], uw) for ch, uw in zip(chains, uws)]
        yield
        qos = [_dot(ch["kkqk"][L:] * ch["dec"], uw) for ch, uw in zip(chains, uws)]
        yield
        for ch, kn, qo in zip(chains, kns, qos):
            d, it = ch["d"], ch["it"]
            n_ref[d, it] = kn[:, :C_DV]
            kq_ref[d, it, 0:C_DK, :] = kn[:, C_DV:].astype(BF16)
            kq_ref[d, it, C_DK:C_DK + L, :] = (ch["q"] * ch["eG"] - qo[:, C_DV:]).astype(BF16)
            o_ref[d, pl.ds(ch["r0"], L), :] = qo[:, :C_DV]
            egl_ref[d, it] = jnp.broadcast_to(ch["egl"], (8, LANES))

    def recur(it, S):
        out = []
        for d, c in ((0, it), (1, _bwd_chunk(it, nc, ncc))):
            r0 = pl.multiple_of(c * L, L)
            ks_qs = jnp.dot(kq_ref[d, it], S[d].astype(BF16), preferred_element_type=F32)
            o_ref[d, pl.ds(r0, L), :] += ks_qs[C_DK:]
            out.append(S[d] * egl_ref[d, it][0:1, 0:1] + (n_ref[d, it] - ks_qs[:C_DK]))
        return out

    for _ in prep_stages(0):
        pass

    def body(gi, carry):
        S = list(carry)
        stages = prep_stages(gi)
        for j in range(grp):
            next(stages, None)
            S = recur((gi - 1) * grp + j, S)
        for _ in stages:
            pass
        return tuple(S)

    s0 = jnp.zeros((C_DK, C_DV), F32)
    S = list(lax.fori_loop(1, ng, body, (s0, s0)))
    for j in range(grp):
        S = recur((ng - 1) * grp + j, S)

    osum = o_ref[0] + o_ref[1]
    on = osum * lax.rsqrt(jnp.mean(osum * osum, axis=-1, keepdims=True) + EPS) * nw_ref[...]
    y_ref[...] = (on * _silu(z_ref[...])).astype(y_ref.dtype)


def _gdn_call(p, gcol, grow, par, nw, l, ncc):
    b, t, _ = p.shape
    nc = t // CHUNK
    blk = lambda base: pl.BlockSpec((None, t, LANES), lambda i, h: (i, 0, base // LANES + h))
    big = lambda dt: pltpu.VMEM((2, t, LANES), dt)
    return pl.pallas_call(
        functools.partial(_gdn_kernel, nc=nc, ncc=ncc),
        grid=(b, C_HEADS),
        in_specs=[blk(P_CQ), blk(P_CK), blk(P_CV), blk(P_CZ),
                  pl.BlockSpec((None, None, t, 4), lambda i, h: (i, h, 0, 0)),
                  pl.BlockSpec((None, None, nc, 4, CHUNK), lambda i, h: (i, h, 0, 0, 0)),
                  pl.BlockSpec((None, None, 1, 4), lambda i, h: (l, h, 0, 0)),
                  pl.BlockSpec((None, 1, LANES), lambda i, h: (l, 0, 0))],
        out_specs=pl.BlockSpec((None, t, LANES), lambda i, h: (i, 0, h)),
        out_shape=jax.ShapeDtypeStruct((b, t, BRANCH_W), BF16),
        scratch_shapes=[pltpu.VMEM((2, nc, C_DK, C_DV), F32),
                        pltpu.VMEM((2, nc, C_DK + CHUNK, C_DV), BF16),
                        pltpu.VMEM((2, nc, 8, LANES), F32),
                        big(F32)],
        compiler_params=_params(("arbitrary", "arbitrary")),
        name="gdn",
    )(p, p, p, p, gcol, grow, par, nw)


def _merge_kernel(ya_ref, yb_ref, yc_ref, ga_ref, gb_ref, gc_ref, h_ref, mod_ref, wb_ref, wo_ref, o_ref):
    y = (_sigmoid(ga_ref[...].astype(F32)) * _dot(ya_ref[...], wb_ref[0])
         + _sigmoid(gb_ref[...].astype(F32)) * _dot(yb_ref[...], wb_ref[1])
         + _sigmoid(gc_ref[...].astype(F32)) * _dot(yc_ref[...], wb_ref[2]))
    o_ref[...] = h_ref[...] + mod_ref[2:3, :] * _dot(y, wo_ref[...])


def _mod_row(j, nct):
    return 1 if nct == 0 else jnp.minimum(j // nct, 1)


def _merge_call(ya, yb, yc, gates, h, mod, wb, wo, l, tm, nct, latent_only):
    b, t, d = h.shape
    skip = nct if latent_only else 0
    yblk = lambda: pl.BlockSpec((None, tm, BRANCH_W), lambda i, j: (i, j + skip, 0))
    yb_skip = skip - (t - yb.shape[1]) // tm
    gblk = lambda g: pl.BlockSpec((None, tm, d), lambda i, j: (i, j + skip, g))
    return pl.pallas_call(
        _merge_kernel,
        grid=(b, t // tm - skip),
        in_specs=[yblk(), pl.BlockSpec((None, tm, BRANCH_W), lambda i, j: (i, j + yb_skip, 0)), yblk(),
                  gblk(0), gblk(1), gblk(2),
                  pl.BlockSpec((None, tm, d), lambda i, j: (i, j + skip, 0)),
                  pl.BlockSpec((None, None, None, 6, d), lambda i, j: (l, i, _mod_row(j + skip, nct), 0, 0)),
                  _resident((None, 3, BRANCH_W, d), lambda i, j: (l, 0, 0, 0)),
                  _resident((None, d, d), lambda i, j: (l, 0, 0))],
        out_specs=pl.BlockSpec((None, tm, d), lambda i, j: (i, j, 0)),
        out_shape=jax.ShapeDtypeStruct((b, t - skip * tm, d), F32),
        compiler_params=_params(("arbitrary", "arbitrary")),
        name="merge",
    )(ya, yb, yc, gates, gates, gates, h, mod, wb, wo)


def _ffn_kernel(h_ref, hp_ref, hn_ref, mod_ref, nw_ref, wup_ref, cw_ref, wdn_ref, o_ref, acc_ref,
                *, nct, ntiles):
    tm = h_ref.shape[0]
    j = pl.program_id(1)
    mod = mod_ref[...]
    nw = nw_ref[...]
    h = h_ref[...]
    xn = _modnorm(h, nw, mod[3:4], mod[4:5]).astype(BF16)
    halo = jnp.concatenate([hp_ref[...], hn_ref[...]], axis=0)
    xh = _modnorm(halo, nw, mod[3:4], mod[4:5]).astype(BF16)
    has_prev = jnp.logical_and(j != 0, j != nct).astype(F32)
    has_next = jnp.logical_and(j != nct - 1, j != ntiles - 1).astype(F32)
    rows = lax.broadcasted_iota(jnp.int32, (tm, 1), 0)
    first = rows == 0
    last = rows == tm - 1

    def cols(kind, jf):
        return slice(kind * D_FF + jf * FFN_TF, kind * D_FF + (jf + 1) * FFN_TF)

    x_all = jnp.concatenate([xn, xh], axis=0)

    def up(jf):
        us = [jnp.dot(x_all, wup_ref[:, cols(kind, jf)], preferred_element_type=F32) for kind in range(2)]
        return [(u[:tm], u[tm:]) for u in us]

    def conv(kind, jf, u, uh):
        cw = cw_ref[:, cols(kind, jf)]
        prev = jnp.where(first, uh[7:8] * has_prev, pltpu.roll(u, 1, axis=0))
        nxt = jnp.where(last, uh[8:9] * has_next, pltpu.roll(u, tm - 1, axis=0))
        return prev * cw[0:1] + u * cw[1:2] + nxt * cw[2:3]

    nj = D_FF // FFN_TF
    ahead = [up(jf) for jf in range(min(FFN_AHEAD, nj))]
    acts = []
    for jf in range(nj):
        if jf + FFN_AHEAD < nj:
            ahead.append(up(jf + FFN_AHEAD))
        cur = ahead.pop(0)
        acts.append((conv(0, jf, *cur[0]) * _silu(conv(1, jf, *cur[1]))).astype(BF16))
        if len(acts) == FFN_DOWN_GROUP or jf == nj - 1:
            r1 = (jf + 1) * FFN_TF
            part = jnp.dot(jnp.concatenate(acts, axis=1), wdn_ref[r1 - len(acts) * FFN_TF:r1, :],
                           preferred_element_type=F32)
            if r1 == len(acts) * FFN_TF:
                acc_ref[...] = part
            else:
                acc_ref[...] += part
            acts = []
    o_ref[...] = h + mod[5:6] * acc_ref[...]


def _ffn_call(h, mod, nw, wup, cw, wdn, l, tm, nct):
    b, t, d = h.shape
    ntiles = t // tm
    hb = tm // 8
    return pl.pallas_call(
        functools.partial(_ffn_kernel, nct=nct, ntiles=ntiles),
        grid=(b, ntiles),
        in_specs=[pl.BlockSpec((None, tm, d), lambda i, j: (i, j, 0)),
                  pl.BlockSpec((None, 8, d), lambda i, j: (i, jnp.maximum(j * hb - 1, 0), 0)),
                  pl.BlockSpec((None, 8, d), lambda i, j: (i, jnp.minimum((j + 1) * hb, t // 8 - 1), 0)),
                  pl.BlockSpec((None, None, None, 6, d), lambda i, j: (l, i, _mod_row(j, nct), 0, 0)),
                  pl.BlockSpec((None, 1, d), lambda i, j: (l, 0, 0)),
                  _resident((None,) + wup.shape[1:], lambda i, j: (l, 0, 0)),
                  _resident((None,) + cw.shape[1:], lambda i, j: (l, 0, 0)),
                  _resident((None,) + wdn.shape[1:], lambda i, j: (l, 0, 0))],
        out_specs=pl.BlockSpec((None, tm, d), lambda i, j: (i, j, 0)),
        out_shape=jax.ShapeDtypeStruct((b, t, d), F32),
        scratch_shapes=[pltpu.VMEM((tm, d), F32)],
        compiler_params=_params(("arbitrary", "arbitrary")),
        name="ffn",
    )(h, h, h, mod, nw, wup, cw, wdn)


def _rope_tables(tc, tl):
    rows = tl // GRID_W
    row = jnp.repeat(jnp.arange(rows, dtype=F32), GRID_W)
    col = jnp.tile(jnp.arange(GRID_W, dtype=F32), rows)
    n_freq = B_DH // 4
    inv = ROPE_BASE ** (-jnp.arange(n_freq, dtype=F32) / n_freq)
    ang_r = row[:, None] * inv
    ang_c = col[:, None] * inv
    cos = jnp.concatenate([jnp.cos(ang_r)] * 2 + [jnp.cos(ang_c)] * 2, axis=1)
    sin = jnp.concatenate([-jnp.sin(ang_r), jnp.sin(ang_r), -jnp.sin(ang_c), jnp.sin(ang_c)], axis=1)
    cos = jnp.concatenate([jnp.ones((tc, B_DH), F32), cos], axis=0)
    sin = jnp.concatenate([jnp.zeros((tc, B_DH), F32), sin], axis=0)
    return jnp.tile(cos, (1, 2)), jnp.tile(sin, (1, 2))


def _pack_kernel(w_ref, o_ref):
    def src(c0, c1):
        return w_ref[:, c0:c1].astype(BF16)

    o_ref[:, 0:W_GATES] = src(4384, 7456)
    o_ref[:, W_AV:W_BQ] = src(512, 1536)
    o_ref[:, W_BQ:W_CQ] = src(1552, 2320)
    o_ref[:, W_CQ:W_AQK] = src(2320, 4368)
    for hd in range(A_HEADS):
        c0 = W_AQK + 2 * A_DQK * hd
        o_ref[:, c0:c0 + A_DQK] = src(A_DQK * hd, A_DQK * (hd + 1))
        o_ref[:, c0 + A_DQK:c0 + 2 * A_DQK] = src(256 + A_DQK * hd, 256 + A_DQK * (hd + 1))
    sm = jnp.concatenate([src(1536, 1552), src(4368, 4384)], axis=1)
    r = lax.broadcasted_iota(jnp.int32, (32, LANES), 0)
    c = lax.broadcasted_iota(jnp.int32, (32, LANES), 1)
    perm = jnp.where((c < 32) & (r == (c // 16) * 16 + (c % 4) * 4 + (c % 16) // 4), 1.0, 0.0).astype(BF16)
    o_ref[:, W_SMALL:W_COLS] = jnp.dot(sm, perm, preferred_element_type=F32).astype(BF16)


def _pack_call(w_in):
    depth, d, n = w_in.shape
    rows = 256
    return pl.pallas_call(
        _pack_kernel,
        grid=(depth, d // rows),
        in_specs=[pl.BlockSpec((None, rows, n), lambda l, i: (l, i, 0))],
        out_specs=pl.BlockSpec((None, rows, W_COLS), lambda l, i: (l, i, 0)),
        out_shape=jax.ShapeDtypeStruct((depth, d, W_COLS), BF16),
        compiler_params=_params(("arbitrary", "arbitrary")),
        name="pack_w_in",
    )(w_in)


def kernel(x, c, ctx, c_ctx, norm1_w, norm2_w, ada_w, ada_b, w_in, a_gate_b, a_norm_w, b_qnorm_w,
           b_knorm_w, c_conv_w, c_a_log, c_dt_bias, c_norm_w, w_branch, w_out, w_up, ffn_conv_w, w_down):
    b, tl, d = x.shape
    tc = ctx.shape[1]
    depth = w_in.shape[0]
    t = tc + tl
    tm = 256 if (tc % 256 == 0 and tl % 256 == 0) else 128
    nct = tc // tm
    ncc = tc // CHUNK
    nj = D_FF // FFN_TF

    cc = jnp.zeros((16, d), F32).at[:b].set(c).at[b].set(c_ctx)
    mods = _ada_call(cc, ada_w, ada_b).reshape(depth, 16, 6, d)
    mod = jnp.stack([jnp.broadcast_to(mods[:, b][:, None], (depth, b, 6, d)), mods[:, :b]], axis=2)

    w_in_p = _pack_call(w_in)
    wb = w_branch.astype(BF16)
    wo = w_out.astype(BF16)
    wup = w_up.astype(BF16)
    wdn = w_down.astype(BF16)
    cos2, sin2 = _rope_tables(tc, tl)
    qw2 = jnp.tile(b_qnorm_w, (1, 2))[:, None]
    kw2 = jnp.tile(b_knorm_w, (1, 2))[:, None]
    a_bcol = a_gate_b.transpose(0, 2, 1)[:, :, :, None]
    c_par = jnp.concatenate([c_a_log, c_dt_bias], axis=1).transpose(0, 2, 1)[:, :, None, :]

    h = jnp.concatenate([ctx, x], axis=1)
    for l in range(depth):
        last = l == depth - 1
        gates, p, bq, bk, qkt, a_grow, c_grow, c_gcol = _proj_call(h, mod, norm1_w[:, None], w_in_p, c_conv_w, qw2, kw2, cos2, sin2,
                                                  l, tm, nct)
        ya = _mlstm_call(p, qkt, a_grow, a_bcol, a_norm_w[:, None], l, ncc)
        yb = _gqa_call(bq, bk, p, tc, latent_only=last)
        yc = _gdn_call(p, c_gcol, c_grow, c_par, c_norm_w[:, None], l, ncc)
        h = _merge_call(ya, yb, yc, gates, h, mod, wb, wo, l, tm, nct, latent_only=last)
        h = _ffn_call(h, mod, norm2_w[:, None], wup, ffn_conv_w, wdn, l, tm, 0 if last else nct)
    return h
```
